```python
import jax, jax.numpy as jnp
from jax import lax
import numpy as np

D_MODEL = 1024
BATCH = 2
SEQ = 8192
DEPTH = 2

CHUNK = 64
EPS = 1e-6
CONV_K = 4
A_HEADS = 8
A_HEAD_DIM = 64
A_LEFT_CHUNKS = 8
A_MAX_REL = 256
B_HEADS = 4
B_HEAD_DIM = 128
C_HEADS = 8
C_HEAD_DIM = 64
C_GROUPS = 2
C_STATE = 128
D_HEADS = 8
D_HEAD_DIM = 64
D_QBLOCK = 128
D_FF = -(-8 * D_MODEL // (3 * 256)) * 256

A_W = A_HEADS * A_HEAD_DIM
B_W = B_HEADS * B_HEAD_DIM
C_W = C_HEADS * C_HEAD_DIM
D_W = D_HEADS * D_HEAD_DIM
C_BC = C_GROUPS * C_STATE
C_CONV_CH = C_W + 2 * C_BC
EVEN_SPLIT = (A_W, A_W, A_W, 3 * B_W, B_HEADS, B_HEADS, B_W)
ODD_SPLIT = (C_W, C_CONV_CH, C_HEADS, D_W, D_W, D_W, D_HEADS)
PROJ_EVEN = sum(EVEN_SPLIT)
PROJ_ODD = sum(ODD_SPLIT)
N_EVEN = (DEPTH + 1) // 2
N_ODD = DEPTH // 2

kernel_name = "hybrid_chunk_causal_encoder_trunk"


def rms_norm(x, w):
    xf = x.astype(jnp.float32)
    y = xf * lax.rsqrt(jnp.mean(xf * xf, axis=-1, keepdims=True) + EPS)
    return (y * w.astype(jnp.float32)).astype(x.dtype)


def l2_normalize(x):
    return x * lax.rsqrt(jnp.sum(x * x, axis=-1, keepdims=True) + EPS)


def split_cols(a, sizes):
    return jnp.split(a, np.cumsum(sizes)[:-1].tolist(), axis=-1)


def causal_depthwise_conv(x, w):
    return lax.conv_general_dilated(
        x, w[:, None, :].astype(x.dtype), window_strides=(1,),
        padding=[(CONV_K - 1, 0)], dimension_numbers=("NWC", "WIO", "NWC"),
        feature_group_count=x.shape[-1])


def swiglu(h, w_gate, w_up, w_down):
    return (jax.nn.silu(h @ w_gate) * (h @ w_up)) @ w_down


def chunk_band_attention(q, k, v, rel_bias):
    bsz, t, h, dh = q.shape
    nc = t // CHUNK
    band = A_LEFT_CHUNKS + 1
    qc = q.reshape(bsz, nc, CHUNK, h, dh)
    pad = ((0, 0), (A_LEFT_CHUNKS, 0), (0, 0), (0, 0), (0, 0))
    kp = jnp.pad(k.reshape(bsz, nc, CHUNK, h, dh), pad)
    vp = jnp.pad(v.reshape(bsz, nc, CHUNK, h, dh), pad)
    kb = jnp.concatenate([kp[:, j:j + nc] for j in range(band)], axis=2)
    vb = jnp.concatenate([vp[:, j:j + nc] for j in range(band)], axis=2)
    qi = jnp.arange(CHUNK)
    kj = jnp.arange(band * CHUNK)
    dist = qi[:, None] + A_LEFT_CHUNKS * CHUNK - kj[None, :]
    idx = jnp.clip(dist, -A_MAX_REL, A_MAX_REL) + A_MAX_REL
    bias = rel_bias.astype(jnp.float32)[:, idx]
    valid = (jnp.arange(nc)[:, None] - A_LEFT_CHUNKS + kj[None, :] // CHUNK) >= 0
    s = jnp.einsum("bnqhd,bnkhd->bnhqk", qc, kb).astype(jnp.float32) * (dh ** -0.5) + bias
    s = jnp.where(valid[None, :, None, None, :], s, -jnp.inf)
    p = jax.nn.softmax(s, axis=-1).astype(v.dtype)
    o = jnp.einsum("bnhqk,bnkhd->bnqhd", p, vb)
    return o.reshape(bsz, t, h * dh)


def gated_delta_rule(q, k, v, beta, g):
    f32 = jnp.float32
    bsz, t, h, dk = q.shape
    dv = v.shape[-1]
    nc = t // CHUNK
    q = l2_normalize(q.astype(f32)) * (dk ** -0.5)
    k = l2_normalize(k.astype(f32))
    v = v.astype(f32)

    def to_chunks(a):
        return jnp.moveaxis(a.astype(f32).reshape((bsz, nc, CHUNK) + a.shape[2:]), 3, 1)

    q, k, v, beta, g = (to_chunks(a) for a in (q, k, v, beta, g))
    gc = jnp.cumsum(g, axis=-1)
    causal = jnp.tril(jnp.ones((CHUNK, CHUNK), bool))
    strict = jnp.tril(jnp.ones((CHUNK, CHUNK), bool), -1)
    decay = jnp.exp(jnp.where(causal, gc[..., :, None] - gc[..., None, :], -jnp.inf))
    kk = jnp.einsum("bhnid,bhnjd->bhnij", k, k)
    a_strict = jnp.where(strict, beta[..., :, None] * kk * decay, 0.0)
    m = a_strict + jnp.eye(CHUNK, dtype=f32)
    rhs = jnp.concatenate([v * beta[..., None], k * (beta * jnp.exp(gc))[..., None]], axis=-1)
    sol = lax.linalg.triangular_solve(m, rhs, left_side=True, lower=True, unit_diagonal=True)
    u, w = sol[..., :dv], sol[..., dv:]
    attn = jnp.einsum("bhnid,bhnjd->bhnij", q, k) * decay
    q_dec = q * jnp.exp(gc)[..., None]
    k_st = k * jnp.exp(gc[..., -1:] - gc)[..., None]
    g_last = jnp.exp(gc[..., -1])

    def step(state, inp):
        u_c, w_c, q_c, k_c, a_c, gl = inp
        v_new = u_c - jnp.einsum("bhcd,bhde->bhce", w_c, state)
        o = jnp.einsum("bhcd,bhde->bhce", q_c, state) + jnp.einsum("bhij,bhje->bhie", a_c, v_new)
        state = state * gl[..., None, None] + jnp.einsum("bhcd,bhce->bhde", k_c, v_new)
        return state, o

    xs = tuple(jnp.moveaxis(a, 2, 0) for a in (u, w, q_dec, k_st, attn, g_last))
    s0 = jnp.zeros((bsz, h, dk, dv), f32)
    _, o = lax.scan(step, s0, xs)
    return o.transpose(1, 0, 3, 2, 4).reshape(bsz, t, h, dv)


def ssd_scan(x, dt, a, bm, cm):
    f32 = jnp.float32
    bsz, t, h, p = x.shape
    g, n = bm.shape[2], bm.shape[3]
    hg = h // g
    nc = t // CHUNK
    xc = x.astype(f32).reshape(bsz, nc, CHUNK, g, hg, p)
    dtc = dt.astype(f32).reshape(bsz, nc, CHUNK, g, hg)
    bc = bm.astype(f32).reshape(bsz, nc, CHUNK, g, n)
    cc = cm.astype(f32).reshape(bsz, nc, CHUNK, g, n)
    da_cs = jnp.cumsum(dtc * a.astype(f32).reshape(g, hg), axis=2)
    seg = da_cs[:, :, :, None] - da_cs[:, :, None, :]
    mask = jnp.tril(jnp.ones((CHUNK, CHUNK), bool))[:, :, None, None]
    lmat = jnp.exp(jnp.where(mask, seg, -jnp.inf))
    cb = jnp.einsum("bclgn,bcsgn->bclsg", cc, bc)
    wts = cb[..., None] * lmat * dtc[:, :, None]
    y_diag = jnp.einsum("bclsgh,bcsghp->bclghp", wts, xc)
    decay_states = jnp.exp(da_cs[:, :, -1:] - da_cs)
    states = jnp.einsum("bclgn,bclgh,bclghp->bcghpn", bc, decay_states * dtc, xc)
    chunk_decay = jnp.exp(da_cs[:, :, -1])

    def step(state, inp):
        st, dec = inp
        return state * dec[..., None, None] + st, state

    s0 = jnp.zeros((bsz, g, hg, p, n), f32)
    _, prev = lax.scan(step, s0, (jnp.moveaxis(states, 1, 0), jnp.moveaxis(chunk_decay, 1, 0)))
    prev = jnp.moveaxis(prev, 0, 1)
    y_off = jnp.einsum("bclgn,bcghpn,bclgh->bclghp", cc, prev, jnp.exp(da_cs))
    return (y_diag + y_off).reshape(bsz, t, h, p)


def forgetting_attention(q, k, v, log_f):
    bsz, t, h, dh = q.shape
    f_cum = jnp.cumsum(log_f.astype(jnp.float32), axis=1).transpose(0, 2, 1)
    qh, kh, vh = (a.transpose(0, 2, 1, 3) for a in (q, k, v))
    kpos = jnp.arange(t)
    scale = dh ** -0.5

    def block(i):
        start = i * D_QBLOCK
        qb = lax.dynamic_slice_in_dim(qh, start, D_QBLOCK, axis=2)
        fq = lax.dynamic_slice_in_dim(f_cum, start, D_QBLOCK, axis=2)
        s = jnp.einsum("bhqd,bhkd->bhqk", qb, kh).astype(jnp.float32) * scale
        s = s + fq[..., :, None] - f_cum[:, :, None, :]
        qpos = start + jnp.arange(D_QBLOCK)
        s = jnp.where(kpos[None, :] <= qpos[:, None], s, -jnp.inf)
        pr = jax.nn.softmax(s, axis=-1).astype(vh.dtype)
        return jnp.einsum("bhqk,bhkd->bhqd", pr, vh)

    o = lax.map(block, jnp.arange(t // D_QBLOCK))
    return o.transpose(1, 0, 3, 2, 4).reshape(bsz, t, h * dh)


def even_mixer(h, w_in, rel_bias, conv_w, a_log, dt_bias, norm_w, w_out):
    bsz, t, _ = h.shape
    proj = h @ w_in
    a_q, a_k, a_v, b_qkv, b_beta, b_a, b_z = split_cols(proj, EVEN_SPLIT)
    heads_a = lambda z: z.reshape(bsz, t, A_HEADS, A_HEAD_DIM)
    o_a = chunk_band_attention(heads_a(a_q), heads_a(a_k), heads_a(a_v), rel_bias)
    qkv = jax.nn.silu(causal_depthwise_conv(b_qkv, conv_w))
    b_q, b_k, b_v = (z.reshape(bsz, t, B_HEADS, B_HEAD_DIM) for z in split_cols(qkv, (B_W, B_W, B_W)))
    beta = jax.nn.sigmoid(b_beta.astype(jnp.float32))
    g = -jnp.exp(a_log.astype(jnp.float32)) * jax.nn.softplus(b_a.astype(jnp.float32) + dt_bias.astype(jnp.float32))
    o_b = gated_delta_rule(b_q, b_k, b_v, beta, g).astype(h.dtype)
    o_b = rms_norm(o_b, norm_w) * jax.nn.silu(b_z.reshape(bsz, t, B_HEADS, B_HEAD_DIM))
    return jnp.concatenate([o_a, o_b.reshape(bsz, t, B_W)], axis=-1) @ w_out


def odd_mixer(h, w_in, conv_w, conv_b, dt_bias, a_log, d_skip, norm_w, f_bias, w_out):
    bsz, t, _ = h.shape
    proj = h @ w_in
    c_z, c_xbc, c_dt, d_q, d_k, d_v, d_f = split_cols(proj, ODD_SPLIT)
    xbc = jax.nn.silu(causal_depthwise_conv(c_xbc, conv_w) + conv_b)
    c_x, c_b, c_c = split_cols(xbc, (C_W, C_BC, C_BC))
    c_x = c_x.reshape(bsz, t, C_HEADS, C_HEAD_DIM)
    dt = jax.nn.softplus(c_dt.astype(jnp.float32) + dt_bias.astype(jnp.float32))
    a = -jnp.exp(a_log.astype(jnp.float32))
    y = ssd_scan(c_x, dt, a, c_b.reshape(bsz, t, C_GROUPS, C_STATE), c_c.reshape(bsz, t, C_GROUPS, C_STATE))
    y = (y + d_skip.astype(jnp.float32)[:, None] * c_x.astype(jnp.float32)).astype(h.dtype)
    y = (y.reshape(bsz, t, C_W) * jax.nn.silu(c_z)).reshape(bsz, t, C_GROUPS, C_W // C_GROUPS)
    y = rms_norm(y, norm_w.reshape(C_GROUPS, C_W // C_GROUPS)).reshape(bsz, t, C_W)
    heads_d = lambda z: z.reshape(bsz, t, D_HEADS, D_HEAD_DIM)
    log_f = jax.nn.log_sigmoid(d_f.astype(jnp.float32) + f_bias.astype(jnp.float32))
    o_d = forgetting_attention(heads_d(d_q), heads_d(d_k), heads_d(d_v), log_f)
    return jnp.concatenate([y, o_d], axis=-1) @ w_out


def setup_inputs(seed: int = 0) -> dict:
    key = jax.random.key(seed)
    ks = jax.random.split(key, 24)
    f32 = jnp.float32
    nrm = lambda k, shape, scale: jax.random.normal(k, shape, f32) * scale

    def mamba_dt_bias(k, shape):
        dt = jnp.exp(jax.random.uniform(k, shape, f32, np.log(1e-3), np.log(1e-1)))
        return dt + jnp.log(-jnp.expm1(-dt))

    return {
        "x": nrm(ks[0], (BATCH, SEQ, D_MODEL), 1.0),
        "norm_mix": 1.0 + nrm(ks[1], (DEPTH, D_MODEL), 0.02),
        "norm_ffn": 1.0 + nrm(ks[2], (DEPTH, D_MODEL), 0.02),
        "norm_final": 1.0 + nrm(ks[3], (D_MODEL,), 0.02),
        "ffn_w_gate": nrm(ks[4], (DEPTH, D_MODEL, D_FF), D_MODEL ** -0.5),
        "ffn_w_up": nrm(ks[5], (DEPTH, D_MODEL, D_FF), D_MODEL ** -0.5),
        "ffn_w_down": nrm(ks[6], (DEPTH, D_FF, D_MODEL), D_FF ** -0.5),
        "ab_w_in": nrm(ks[7], (N_EVEN, D_MODEL, PROJ_EVEN), D_MODEL ** -0.5),
        "ab_rel_bias": nrm(ks[8], (N_EVEN, A_HEADS, 2 * A_MAX_REL + 1), 0.2),
        "ab_conv_w": nrm(ks[9], (N_EVEN, CONV_K, 3 * B_W), CONV_K ** -0.5),
        "ab_a_log": jnp.log(jax.random.uniform(ks[10], (N_EVEN, B_HEADS), f32, 1.0, 16.0)),
        "ab_dt_bias": mamba_dt_bias(ks[11], (N_EVEN, B_HEADS)),
        "ab_norm_w": 1.0 + nrm(ks[12], (N_EVEN, B_HEAD_DIM), 0.02),
        "ab_w_out": nrm(ks[13], (N_EVEN, A_W + B_W, D_MODEL), (A_W + B_W) ** -0.5),
        "cd_w_in": nrm(ks[14], (N_ODD, D_MODEL, PROJ_ODD), D_MODEL ** -0.5),
        "cd_conv_w": nrm(ks[15], (N_ODD, CONV_K, C_CONV_CH), CONV_K ** -0.5),
        "cd_conv_b": nrm(ks[16], (N_ODD, C_CONV_CH), 0.02),
        "cd_dt_bias": mamba_dt_bias(ks[17], (N_ODD, C_HEADS)),
        "cd_a_log": jnp.log(jax.random.uniform(ks[18], (N_ODD, C_HEADS), f32, 1.0, 16.0)),
        "cd_d_skip": 1.0 + nrm(ks[19], (N_ODD, C_HEADS), 0.1),
        "cd_norm_w": 1.0 + nrm(ks[20], (N_ODD, C_W), 0.02),
        "cd_f_bias": 2.0 + nrm(ks[21], (N_ODD, D_HEADS), 0.1),
        "cd_w_out": nrm(ks[22], (N_ODD, C_W + D_W, D_MODEL), (C_W + D_W) ** -0.5),
    }


def reference(x, norm_mix, norm_ffn, norm_final, ffn_w_gate, ffn_w_up, ffn_w_down,
              ab_w_in, ab_rel_bias, ab_conv_w, ab_a_log, ab_dt_bias, ab_norm_w, ab_w_out,
              cd_w_in, cd_conv_w, cd_conv_b, cd_dt_bias, cd_a_log, cd_d_skip, cd_norm_w,
              cd_f_bias, cd_w_out):
    for layer in range(DEPTH):
        h = rms_norm(x, norm_mix[layer])
        i = layer // 2
        if layer % 2 == 0:
            x = x + even_mixer(h, ab_w_in[i], ab_rel_bias[i], ab_conv_w[i], ab_a_log[i],
                               ab_dt_bias[i], ab_norm_w[i], ab_w_out[i])
        else:
            x = x + odd_mixer(h, cd_w_in[i], cd_conv_w[i], cd_conv_b[i], cd_dt_bias[i],
                              cd_a_log[i], cd_d_skip[i], cd_norm_w[i], cd_f_bias[i], cd_w_out[i])
        h = rms_norm(x, norm_ffn[layer])
        x = x + swiglu(h, ffn_w_gate[layer], ffn_w_up[layer], ffn_w_down[layer])
    return rms_norm(x, norm_final)
```

```python
import functools

import jax
import jax.numpy as jnp
import numpy as np
from jax import lax
from jax.experimental import pallas as pl
from jax.experimental.pallas import tpu as pltpu

F32 = jnp.float32
BF16 = jnp.bfloat16

D_MODEL = 1024
CHUNK = 64
EPS = 1e-6
CONV_K = 4
A_HEADS, A_HEAD_DIM, A_LEFT_CHUNKS, A_MAX_REL = 8, 64, 8, 256
B_HEADS, B_HEAD_DIM = 4, 128
C_HEADS, C_HEAD_DIM, C_GROUPS, C_STATE = 8, 64, 2, 128
D_HEADS, D_HEAD_DIM = 8, 64
A_W = A_HEADS * A_HEAD_DIM
B_W = B_HEADS * B_HEAD_DIM
C_W = C_HEADS * C_HEAD_DIM
D_W = D_HEADS * D_HEAD_DIM
C_BC = C_GROUPS * C_STATE

LANES = 128
SUBLANES = 8
VMEM_LIMIT_BYTES = 56 * 1024 * 1024
NEG_BIG = -1e30

ROW_TILE = 512
FFN_CHUNK = 256
BAND_TQ = 128
BAND_WIN = BAND_TQ + A_LEFT_CHUNKS * CHUNK
FOX_T = 512
SCAN_ROWS = 512


def _cparams(*sem):
    return pltpu.CompilerParams(dimension_semantics=sem, vmem_limit_bytes=VMEM_LIMIT_BYTES)


def _resident(shape):
    nd = len(shape)
    return pl.BlockSpec(shape, lambda *_: (0,) * nd, pipeline_mode=pl.Buffered(1))


def _dot(a, b):
    return jnp.dot(a, b, preferred_element_type=F32)


def _dot_nt(a, b):
    return lax.dot_general(a, b, (((1,), (1,)), ((), ())), preferred_element_type=F32)


def _dot_tn(a, b):
    return lax.dot_general(a, b, (((0,), (0,)), ((), ())), preferred_element_type=F32)


def _rms(x, w):
    return x * lax.rsqrt(jnp.mean(x * x, axis=-1, keepdims=True) + EPS) * w


def _split3(x):
    hi = x.astype(BF16)
    r1 = x - hi.astype(F32)
    mid = r1.astype(BF16)
    lo = (r1 - mid.astype(F32)).astype(BF16)
    return hi, mid, lo


def _cumsum_rows(tril, x):
    hi, mid, lo = _split3(x)
    return _dot(tril, hi) + _dot(tril, mid) + _dot(tril, lo)


def _tril_masks(n):
    r = lax.broadcasted_iota(jnp.int32, (n, n), 0)
    c = lax.broadcasted_iota(jnp.int32, (n, n), 1)
    return r >= c, r > c


def _norm_proj_kernel(x_ref, nw_ref, *refs):
    n_out = len(refs) // 2
    h = _rms(x_ref[...], nw_ref[...]).astype(BF16)
    for w_ref, o_ref in zip(refs[:n_out], refs[n_out:]):
        o_ref[...] = _dot(h, w_ref[...]).astype(o_ref.dtype)


def _norm_proj(x, norm_w, weights, out_dtypes):
    n, d = x.shape
    in_specs = [pl.BlockSpec((ROW_TILE, d), lambda i: (i, 0)), _resident((1, d))]
    in_specs += [_resident(w.shape) for w in weights]
    out_shape = [jax.ShapeDtypeStruct((n, w.shape[1]), dt) for w, dt in zip(weights, out_dtypes)]
    out_specs = [pl.BlockSpec((ROW_TILE, w.shape[1]), lambda i: (i, 0)) for w in weights]
    return pl.pallas_call(
        _norm_proj_kernel, grid=(n // ROW_TILE,), in_specs=in_specs, out_specs=out_specs,
        out_shape=out_shape, compiler_params=_cparams("parallel"), name="norm_proj",
    )(x, norm_w.reshape(1, d), *weights)


def _tail_kernel(x_ref, a_ref, b_ref, woa_ref, wob_ref, nw_ref, wg_ref, wu_ref, wd_ref, fn_ref,
                 o_ref, acc_ref, *, d_ff, final_norm):
    x1 = x_ref[...] + _dot(a_ref[...], woa_ref[...]) + _dot(b_ref[...], wob_ref[...])
    h = _rms(x1, nw_ref[...]).astype(BF16)
    acc_ref[...] = x1
    ffn = None
    for c in range(d_ff // FFN_CHUNK):
        cols = slice(c * FFN_CHUNK, (c + 1) * FFN_CHUNK)
        g = _dot(h, wg_ref[:, cols])
        u = _dot(h, wu_ref[:, cols])
        part = _dot((g * jax.nn.sigmoid(g) * u).astype(BF16), wd_ref[cols, :])
        ffn = part if ffn is None else ffn + part
    y = acc_ref[...] + ffn
    if final_norm:
        y = _rms(y, fn_ref[...])
    o_ref[...] = y


def _layer_tail(x, mix_a, mix_b, wo_a, wo_b, norm_w, wg, wu, wd, final_w, final_norm):
    n, d = x.shape
    d_ff = wg.shape[1]
    row = lambda w: pl.BlockSpec((ROW_TILE, w), lambda i: (i, 0))
    in_specs = [row(d), row(mix_a.shape[1]), row(mix_b.shape[1]), _resident(wo_a.shape),
                _resident(wo_b.shape), _resident((1, d)), _resident(wg.shape), _resident(wu.shape),
                _resident(wd.shape), _resident((1, d))]
    return pl.pallas_call(
        functools.partial(_tail_kernel, d_ff=d_ff, final_norm=final_norm),
        grid=(n // ROW_TILE,), in_specs=in_specs, out_specs=row(d),
        out_shape=jax.ShapeDtypeStruct((n, d), F32),
        scratch_shapes=[pltpu.VMEM((ROW_TILE, d), F32)],
        compiler_params=_cparams("parallel"), name="layer_tail",
    )(x, mix_a, mix_b, wo_a, wo_b, norm_w.reshape(1, d), wg, wu, wd, final_w.reshape(1, d))


def _conv_kernel(x_ref, halo_ref, w_ref, b_ref, o_ref, pad_ref, *, seq_len):
    tm = x_ref.shape[0]
    at_seq_start = (pl.program_id(0) * tm) % seq_len == 0
    pad_ref[0:SUBLANES, :] = jnp.where(at_seq_start, 0.0, halo_ref[...])
    pad_ref[SUBLANES:SUBLANES + tm, :] = x_ref[...]
    acc = b_ref[...] + w_ref[CONV_K - 1:CONV_K, :] * x_ref[...]
    for back in range(1, CONV_K):
        tap = w_ref[CONV_K - 1 - back:CONV_K - back, :]
        acc = acc + tap * pad_ref[SUBLANES - back:SUBLANES - back + tm, :]
    o_ref[...] = acc * jax.nn.sigmoid(acc)


def _conv_silu(x, w, b, seq_len):
    n, c = x.shape
    tm = ROW_TILE
    halo_blocks = tm // SUBLANES
    return pl.pallas_call(
        functools.partial(_conv_kernel, seq_len=seq_len),
        grid=(n // tm,),
        in_specs=[pl.BlockSpec((tm, c), lambda i: (i, 0)),
                  pl.BlockSpec((SUBLANES, c), lambda i: (jnp.maximum(i * halo_blocks - 1, 0), 0)),
                  _resident((CONV_K, c)), _resident((1, c))],
        out_specs=pl.BlockSpec((tm, c), lambda i: (i, 0)),
        out_shape=jax.ShapeDtypeStruct((n, c), F32),
        scratch_shapes=[pltpu.VMEM((SUBLANES + tm, c), F32)],
        compiler_params=_cparams("parallel"), name="conv_silu",
    )(x, x, w, b.reshape(1, c))


def _band_kernel(q_ref, *refs):
    n_blk = BAND_WIN // BAND_TQ
    k_refs, v_refs = refs[:n_blk], refs[n_blk:2 * n_blk]
    bias_ref, o_ref = refs[2 * n_blk], refs[2 * n_blk + 1]
    i = pl.program_id(2)
    k = jnp.concatenate([r[...] for r in k_refs], axis=0)
    v = jnp.concatenate([r[...] for r in v_refs], axis=0)
    q = q_ref[...] * jnp.asarray(A_HEAD_DIM ** -0.5, BF16)
    lane = lax.broadcasted_iota(jnp.int32, q.shape, 1)
    key_blk = lax.broadcasted_iota(jnp.int32, (1, BAND_WIN), 1) // BAND_TQ
    in_seq = key_blk >= (n_blk - 1) - i
    outs = []
    for half in range(2):
        sel = (lane < A_HEAD_DIM) if half == 0 else (lane >= A_HEAD_DIM)
        s = _dot_nt(jnp.where(sel, q, jnp.zeros_like(q)), k) + bias_ref[half]
        s = jnp.where(in_seq, s, NEG_BIG)
        p = jnp.exp(s - jnp.max(s, axis=-1, keepdims=True))
        outs.append(_dot(p.astype(BF16), v) / jnp.sum(p, axis=-1, keepdims=True))
    o_ref[...] = jnp.where(lane < A_HEAD_DIM, outs[0], outs[1]).astype(o_ref.dtype)


def _band_bias_tiles(rel_bias):
    r = np.arange(BAND_TQ)[:, None]
    c = np.arange(BAND_WIN)[None, :]
    dist = r + A_LEFT_CHUNKS * CHUNK - c
    idx = np.clip(dist, -A_MAX_REL, A_MAX_REL) + A_MAX_REL
    qc, kc = r // CHUNK, c // CHUNK
    allowed = (kc >= qc) & (kc <= qc + A_LEFT_CHUNKS)
    return jnp.where(allowed[None], rel_bias.astype(F32)[:, idx], NEG_BIG)


def _band_attention(q, k, v, rel_bias, bsz, seq_len):
    n_blk = BAND_WIN // BAND_TQ
    nq = seq_len // BAND_TQ
    pairs = A_W // LANES
    q3, k3, v3 = (a.reshape(bsz, seq_len, A_W) for a in (q, k, v))
    blk = (None, BAND_TQ, LANES)
    kv_specs = [pl.BlockSpec(blk, lambda b, p, i, j=j: (b, jnp.maximum(i - (n_blk - 1) + j, 0), p))
                for j in range(n_blk)]
    out = pl.pallas_call(
        _band_kernel, grid=(bsz, pairs, nq),
        in_specs=[pl.BlockSpec(blk, lambda b, p, i: (b, i, p))] + kv_specs + kv_specs
        + [pl.BlockSpec((2, BAND_TQ, BAND_WIN), lambda b, p, i: (p, 0, 0))],
        out_specs=pl.BlockSpec(blk, lambda b, p, i: (b, i, p)),
        out_shape=jax.ShapeDtypeStruct((bsz, seq_len, A_W), BF16),
        compiler_params=_cparams("parallel", "parallel", "parallel"), name="band_attention",
    )(q3, *([k3] * n_blk), *([v3] * n_blk), _band_bias_tiles(rel_bias))
    return out.reshape(bsz * seq_len, A_W)


def _gdn_kernel(qkv_ref, sm_ref, z_ref, par_ref, nw_ref, o_ref, state_ref):
    bsz, rows_per_step = qkv_ref.shape[0], qkv_ref.shape[1]

    @pl.when(pl.program_id(0) == 0)
    def _():
        state_ref[...] = jnp.zeros_like(state_ref)

    causal, strict = _tril_masks(CHUNK)
    tril = jnp.where(causal, 1.0, 0.0).astype(BF16)
    eye = jnp.where(causal & ~strict, 1.0, 0.0)
    a_row = -jnp.exp(par_ref[0:1, :])
    dt_bias_row = par_ref[1:2, :]

    def chunk_step(c, carry):
        rows = pl.ds(pl.multiple_of(c * CHUNK, CHUNK), CHUNK)
        for b in range(bsz):
            sm = sm_ref[b, rows, :]
            beta_all = jax.nn.sigmoid(sm)
            gc_all = _cumsum_rows(tril, a_row * jax.nn.softplus(sm + dt_bias_row))
            gc_all_t = gc_all.T
            for h in range(B_HEADS):
                cols = lambda part: slice(part * B_W + h * B_HEAD_DIM, part * B_W + (h + 1) * B_HEAD_DIM)
                q = qkv_ref[b, rows, cols(0)]
                k = qkv_ref[b, rows, cols(1)]
                v = qkv_ref[b, rows, cols(2)]
                q = q * lax.rsqrt(jnp.sum(q * q, axis=-1, keepdims=True) + EPS) * (B_HEAD_DIM ** -0.5)
                k = k * lax.rsqrt(jnp.sum(k * k, axis=-1, keepdims=True) + EPS)
                beta = beta_all[:, h:h + 1]
                gc = gc_all[:, B_HEADS + h:B_HEADS + h + 1]
                gc_row = gc_all_t[B_HEADS + h:B_HEADS + h + 1, :]
                decay = jnp.exp(jnp.where(causal, gc - gc_row, NEG_BIG))
                kb = k.astype(BF16)
                kk = _dot_nt(kb, kb)
                qk = _dot_nt(q.astype(BF16), kb)
                a_mat = jnp.where(strict, beta * kk * decay, 0.0)
                inv = eye - a_mat
                power = a_mat
                for _ in range(5):
                    power = _dot(power, power)
                    inv = inv + _dot(inv, power)
                exp_gc = jnp.exp(gc)
                u = _dot(inv, v * beta)
                w = _dot(inv, k * (beta * exp_gc))
                attn = (qk * decay).astype(BF16)
                gc_last = gc[CHUNK - 1:CHUNK, :]
                q_dec = (q * exp_gc).astype(BF16)
                k_st = (k * jnp.exp(gc_last - gc)).astype(BF16)
                idx = b * B_HEADS + h
                state = state_ref[idx]
                state_b = state.astype(BF16)
                v_new = u - _dot(w.astype(BF16), state_b)
                v_new_b = v_new.astype(BF16)
                o = _dot(q_dec, state_b) + _dot(attn, v_new_b)
                state_ref[idx] = state * jnp.exp(gc_last) + _dot_tn(k_st, v_new_b)
                zz = z_ref[b, rows, h * B_HEAD_DIM:(h + 1) * B_HEAD_DIM]
                y = _rms(o, nw_ref[...]) * (zz * jax.nn.sigmoid(zz))
                o_ref[b, rows, h * B_HEAD_DIM:(h + 1) * B_HEAD_DIM] = y.astype(o_ref.dtype)
        return carry

    lax.fori_loop(0, rows_per_step // CHUNK, chunk_step, 0)


def _gated_delta(qkv, small, z, a_log, dt_bias, norm_w, bsz, seq_len):
    par = jnp.zeros((2, LANES), F32)
    par = par.at[0, B_HEADS:2 * B_HEADS].set(a_log.astype(F32))
    par = par.at[1, B_HEADS:2 * B_HEADS].set(dt_bias.astype(F32))
    blk = lambda w: pl.BlockSpec((bsz, SCAN_ROWS, w), lambda t: (0, t, 0))
    out = pl.pallas_call(
        _gdn_kernel, grid=(seq_len // SCAN_ROWS,),
        in_specs=[blk(3 * B_W), blk(LANES), blk(B_W), _resident((2, LANES)), _resident((1, B_HEAD_DIM))],
        out_specs=blk(B_W),
        out_shape=jax.ShapeDtypeStruct((bsz, seq_len, B_W), BF16),
        scratch_shapes=[pltpu.VMEM((bsz * B_HEADS, B_HEAD_DIM, B_HEAD_DIM), F32)],
        compiler_params=_cparams("arbitrary"), name="gated_delta",
    )(qkv.reshape(bsz, seq_len, 3 * B_W), small.reshape(bsz, seq_len, LANES),
      z.reshape(bsz, seq_len, B_W), par, norm_w.reshape(1, B_HEAD_DIM).astype(F32))
    return out.reshape(bsz * seq_len, B_W)


def _ssd_kernel(xbc_ref, sm_ref, z_ref, par_ref, skip_ref, nw_ref, o_ref, state_ref):
    bsz, rows_per_step = xbc_ref.shape[0], xbc_ref.shape[1]
    group_w = C_W // C_GROUPS
    heads_per_group = C_HEADS // C_GROUPS

    @pl.when(pl.program_id(0) == 0)
    def _():
        state_ref[...] = jnp.zeros_like(state_ref)

    causal, _ = _tril_masks(CHUNK)
    tril = jnp.where(causal, 1.0, 0.0).astype(BF16)
    a_row = -jnp.exp(par_ref[0:1, :])
    dt_bias_row = par_ref[1:2, :]
    lane = lax.broadcasted_iota(jnp.int32, (CHUNK, LANES), 1)
    low_half = lane < C_HEAD_DIM

    def per_head_lanes(cols, h0):
        return jnp.where(low_half, cols[:, h0:h0 + 1], cols[:, h0 + 1:h0 + 2])

    def chunk_step(c, carry):
        rows = pl.ds(pl.multiple_of(c * CHUNK, CHUNK), CHUNK)
        for b in range(bsz):
            sm = sm_ref[b, rows, :]
            dt_all = jax.nn.softplus(sm + dt_bias_row)
            da_cs = _cumsum_rows(tril, dt_all * a_row)
            da_cs_t = da_cs.T
            dt_t = dt_all.T
            da_last = da_cs[CHUNK - 1:CHUNK, :]
            exp_da = jnp.exp(da_cs)
            to_end = jnp.exp(da_last - da_cs) * dt_all
            chunk_decay = jnp.exp(da_last)
            for g in range(C_GROUPS):
                gl = slice(g * group_w, (g + 1) * group_w)
                bm = xbc_ref[b, rows, C_W + g * C_STATE:C_W + (g + 1) * C_STATE].astype(BF16)
                cm = xbc_ref[b, rows, C_W + C_BC + g * C_STATE:C_W + C_BC + (g + 1) * C_STATE].astype(BF16)
                cb = _dot_nt(cm, bm)
                prev = state_ref[b, :, gl]
                y_off = _dot(cm, prev.astype(BF16))
                ys, xws, decs = [], [], []
                for pr in range(heads_per_group // 2):
                    h0 = g * heads_per_group + 2 * pr
                    pl_ = slice(h0 * C_HEAD_DIM, (h0 + 2) * C_HEAD_DIM)
                    x2 = xbc_ref[b, rows, pl_]
                    x2b = x2.astype(BF16)
                    yd = []
                    for h in (h0, h0 + 1):
                        seg = da_cs[:, h:h + 1] - da_cs_t[h:h + 1, :]
                        wts = cb * jnp.exp(jnp.where(causal, seg, NEG_BIG)) * dt_t[h:h + 1, :]
                        yd.append(_dot(wts.astype(BF16), x2b))
                    y_diag = jnp.where(low_half, yd[0], yd[1])
                    off = y_off[:, 2 * pr * C_HEAD_DIM:(2 * pr + 2) * C_HEAD_DIM] * per_head_lanes(exp_da, h0)
                    y = y_diag + off + skip_ref[:, pl_] * x2
                    zz = z_ref[b, rows, pl_]
                    ys.append(y * (zz * jax.nn.sigmoid(zz)))
                    xws.append((x2 * per_head_lanes(to_end, h0)).astype(BF16))
                    decs.append(jnp.where(low_half[0:1], chunk_decay[:, h0:h0 + 1], chunk_decay[:, h0 + 1:h0 + 2]))
                xw = jnp.concatenate(xws, axis=1)
                dec = jnp.concatenate(decs, axis=1)
                state_ref[b, :, gl] = prev * dec + _dot_tn(bm, xw)
                yg = jnp.concatenate(ys, axis=1)
                o_ref[b, rows, gl] = _rms(yg, nw_ref[:, gl]).astype(o_ref.dtype)
        return carry

    lax.fori_loop(0, rows_per_step // CHUNK, chunk_step, 0)


def _ssd(xbc, small, z, dt_bias, a_log, d_skip, norm_w, bsz, seq_len):
    par = jnp.zeros((2, LANES), F32)
    par = par.at[0, :C_HEADS].set(a_log.astype(F32))
    par = par.at[1, :C_HEADS].set(dt_bias.astype(F32))
    skip = jnp.repeat(d_skip.astype(F32), C_HEAD_DIM).reshape(1, C_W)
    blk = lambda w: pl.BlockSpec((bsz, SCAN_ROWS, w), lambda t: (0, t, 0))
    out = pl.pallas_call(
        _ssd_kernel, grid=(seq_len // SCAN_ROWS,),
        in_specs=[blk(C_W + 2 * C_BC), blk(LANES), blk(C_W), _resident((2, LANES)),
                  _resident((1, C_W)), _resident((1, C_W))],
        out_specs=blk(C_W),
        out_shape=jax.ShapeDtypeStruct((bsz, seq_len, C_W), BF16),
        scratch_shapes=[pltpu.VMEM((bsz, C_STATE, C_W), F32)],
        compiler_params=_cparams("arbitrary"), name="ssd",
    )(xbc.reshape(bsz, seq_len, C_W + 2 * C_BC), small.reshape(bsz, seq_len, LANES),
      z.reshape(bsz, seq_len, C_W), par, skip, norm_w.reshape(1, C_W).astype(F32))
    return out.reshape(bsz * seq_len, C_W)


def _fox_prep_kernel(sm_ref, fb_ref, ft_ref, carry_ref):
    tm = sm_ref.shape[0]

    @pl.when(pl.program_id(1) == 0)
    def _():
        carry_ref[...] = jnp.zeros_like(carry_ref)

    causal, _ = _tril_masks(tm)
    tril = jnp.where(causal, 1.0, 0.0).astype(BF16)
    log_f = jax.nn.log_sigmoid(sm_ref[...] + fb_ref[...])
    f_cum = _cumsum_rows(tril, log_f) + carry_ref[...]
    carry_ref[...] = f_cum[tm - 1:tm, :]
    ft_ref[...] = f_cum.T[D_HEADS:2 * D_HEADS, :]


def _fox_prep(small, f_bias, bsz, seq_len):
    fb = jnp.zeros((1, LANES), F32).at[0, D_HEADS:2 * D_HEADS].set(f_bias.astype(F32))
    nt = seq_len // FOX_T
    return pl.pallas_call(
        _fox_prep_kernel, grid=(bsz, nt),
        in_specs=[pl.BlockSpec((None, FOX_T, LANES), lambda b, t: (b, t, 0)), _resident((1, LANES))],
        out_specs=pl.BlockSpec((None, None, D_HEADS, FOX_T), lambda b, t: (b, t, 0, 0)),
        out_shape=jax.ShapeDtypeStruct((bsz, nt, D_HEADS, FOX_T), F32),
        scratch_shapes=[pltpu.VMEM((1, LANES), F32)],
        compiler_params=_cparams("parallel", "arbitrary"), name="fox_prep",
    )(small.reshape(bsz, seq_len, LANES), fb)


def _fox_kernel(q_ref, k_ref, v_ref, ft_ref, o_ref):
    p = pl.program_id(1)
    i = pl.program_id(2)
    t = FOX_T
    q = q_ref[...] * jnp.asarray(D_HEAD_DIM ** -0.5, BF16)
    lane = lax.broadcasted_iota(jnp.int32, (t, LANES), 1)
    low_half = lane < D_HEAD_DIM
    qs = (jnp.where(low_half, q, jnp.zeros_like(q)), jnp.where(low_half, jnp.zeros_like(q), q))
    on_or_below_diag = _tril_masks(t)[0]

    def step(j, carry, diagonal):
        kj = k_ref[pl.ds(pl.multiple_of(j * t, t), t), :]
        vj = v_ref[pl.ds(pl.multiple_of(j * t, t), t), :]
        new = []
        for half in range(2):
            m, l, acc = carry[3 * half:3 * half + 3]
            s = _dot_nt(qs[half], kj) - ft_ref[j, pl.ds(2 * p + half, 1), :]
            if diagonal:
                s = jnp.where(on_or_below_diag, s, NEG_BIG)
            m_new = jnp.maximum(m, jnp.max(s, axis=-1, keepdims=True))
            alpha = jnp.exp(m - m_new)
            pexp = jnp.exp(s - m_new)
            l = alpha * l + jnp.sum(pexp, axis=-1, keepdims=True)
            acc = alpha * acc + _dot(pexp.astype(BF16), vj)
            new += [m_new, l, acc]
        return tuple(new)

    init = (jnp.full((t, 1), NEG_BIG, F32), jnp.zeros((t, 1), F32), jnp.zeros((t, LANES), F32)) * 2
    carry = lax.fori_loop(0, i, lambda j, c: step(j, c, False), init)
    carry = step(i, carry, True)
    o_ref[...] = jnp.where(low_half, carry[2] / carry[1], carry[5] / carry[4]).astype(o_ref.dtype)


def _fox_attention(q, k, v, ft, bsz, seq_len):
    nt = seq_len // FOX_T
    pairs = D_W // LANES
    q3, k3, v3 = (a.reshape(bsz, seq_len, D_W) for a in (q, k, v))
    out = pl.pallas_call(
        _fox_kernel, grid=(bsz, pairs, nt),
        in_specs=[pl.BlockSpec((None, FOX_T, LANES), lambda b, p, i: (b, i, p)),
                  pl.BlockSpec((None, seq_len, LANES), lambda b, p, i: (b, 0, p)),
                  pl.BlockSpec((None, seq_len, LANES), lambda b, p, i: (b, 0, p)),
                  pl.BlockSpec((None, nt, D_HEADS, FOX_T), lambda b, p, i: (b, 0, 0, 0))],
        out_specs=pl.BlockSpec((None, FOX_T, LANES), lambda b, p, i: (b, i, p)),
        out_shape=jax.ShapeDtypeStruct((bsz, seq_len, D_W), BF16),
        compiler_params=_cparams("parallel", "parallel", "parallel"), name="fox_attention",
    )(q3, k3, v3, ft)
    return out.reshape(bsz * seq_len, D_W)


def _pad_cols(w, width=LANES):
    return jnp.pad(w, ((0, 0), (0, width - w.shape[1])))


def kernel(x, norm_mix, norm_ffn, norm_final, ffn_w_gate, ffn_w_up, ffn_w_down, ab_w_in, ab_rel_bias, ab_conv_w, ab_a_log, ab_dt_bias, ab_norm_w, ab_w_out, cd_w_in, cd_conv_w, cd_conv_b, cd_dt_bias, cd_a_log, cd_d_skip, cd_norm_w, cd_f_bias, cd_w_out):
    bsz, seq_len, d = x.shape
    n = bsz * seq_len
    xf = x.reshape(n, d)
    bf = lambda w: w.astype(BF16)

    w_in = ab_w_in[0]
    o = np.cumsum([0, A_W, A_W, A_W, 3 * B_W, B_HEADS, B_HEADS, B_W])
    weights = [bf(w_in[:, o[0]:o[1]]), bf(w_in[:, o[1]:o[2]]), bf(w_in[:, o[2]:o[3]]),
               bf(w_in[:, o[3]:o[4]]), bf(_pad_cols(w_in[:, o[4]:o[6]])), bf(w_in[:, o[6]:o[7]])]
    a_q, a_k, a_v, b_qkv, b_small, b_z = _norm_proj(
        xf, norm_mix[0], weights, [BF16, BF16, BF16, F32, F32, F32])
    o_a = _band_attention(a_q, a_k, a_v, ab_rel_bias[0], bsz, seq_len)
    qkv = _conv_silu(b_qkv, ab_conv_w[0].astype(F32), jnp.zeros((3 * B_W,), F32), seq_len)
    o_b = _gated_delta(qkv, b_small, b_z, ab_a_log[0], ab_dt_bias[0], ab_norm_w[0], bsz, seq_len)
    w_out = ab_w_out[0]
    xf = _layer_tail(xf, o_a, o_b, bf(w_out[:A_W]), bf(w_out[A_W:]), norm_ffn[0],
                     bf(ffn_w_gate[0]), bf(ffn_w_up[0]), bf(ffn_w_down[0]), norm_final, False)

    w_in = cd_w_in[0]
    o = np.cumsum([0, C_W, C_W + 2 * C_BC, C_HEADS, D_W, D_W, D_W, D_HEADS])
    small_w = jnp.concatenate([w_in[:, o[2]:o[3]], w_in[:, o[6]:o[7]]], axis=1)
    weights = [bf(w_in[:, o[0]:o[1]]), bf(w_in[:, o[1]:o[2]]), bf(_pad_cols(small_w)),
               bf(w_in[:, o[3]:o[4]]), bf(w_in[:, o[4]:o[5]]), bf(w_in[:, o[5]:o[6]])]
    c_z, c_xbc, cd_small, d_q, d_k, d_v = _norm_proj(
        xf, norm_mix[1], weights, [F32, F32, F32, BF16, BF16, BF16])
    xbc = _conv_silu(c_xbc, cd_conv_w[0].astype(F32), cd_conv_b[0].astype(F32), seq_len)
    y_c = _ssd(xbc, cd_small, c_z, cd_dt_bias[0], cd_a_log[0], cd_d_skip[0], cd_norm_w[0], bsz, seq_len)
    ft = _fox_prep(cd_small, cd_f_bias[0], bsz, seq_len)
    o_d = _fox_attention(d_q, d_k, d_v, ft, bsz, seq_len)
    w_out = cd_w_out[0]
    xf = _layer_tail(xf, y_c, o_d, bf(w_out[:C_W]), bf(w_out[C_W:]), norm_ffn[1],
                     bf(ffn_w_gate[1]), bf(ffn_w_up[1]), bf(ffn_w_down[1]), norm_final, True)
    return xf.reshape(bsz, seq_len, d)
```

```python
import functools

import jax
import jax.numpy as jnp
import numpy as np
from jax import lax
from jax.experimental import pallas as pl
from jax.experimental.pallas import tpu as pltpu

F32 = jnp.float32
BF16 = jnp.bfloat16

D_MODEL = 1024
CHUNK = 64
EPS = 1e-6
CONV_K = 4
A_HEADS, A_HEAD_DIM, A_LEFT_CHUNKS, A_MAX_REL = 8, 64, 8, 256
B_HEADS, B_HEAD_DIM = 4, 128
C_HEADS, C_HEAD_DIM, C_GROUPS, C_STATE = 8, 64, 2, 128
D_HEADS, D_HEAD_DIM = 8, 64
A_W = A_HEADS * A_HEAD_DIM
B_W = B_HEADS * B_HEAD_DIM
C_W = C_HEADS * C_HEAD_DIM
D_W = D_HEADS * D_HEAD_DIM
C_BC = C_GROUPS * C_STATE

LANES = 128
SUBLANES = 8
VMEM_LIMIT_BYTES = 56 * 1024 * 1024
NEG_BIG = -1e30

ROW_TILE = 512
FFN_CHUNK = 256
BAND_TQ = 128
BAND_WIN = BAND_TQ + A_LEFT_CHUNKS * CHUNK
FOX_T = 512
SCAN_ROWS = 512


def _cparams(*sem):
    return pltpu.CompilerParams(dimension_semantics=sem, vmem_limit_bytes=VMEM_LIMIT_BYTES)


def _resident(shape):
    nd = len(shape)
    return pl.BlockSpec(shape, lambda *_: (0,) * nd, pipeline_mode=pl.Buffered(1))


def _dot(a, b):
    return jnp.dot(a, b, preferred_element_type=F32)


def _dot_nt(a, b):
    return lax.dot_general(a, b, (((1,), (1,)), ((), ())), preferred_element_type=F32)


def _dot_tn(a, b):
    return lax.dot_general(a, b, (((0,), (0,)), ((), ())), preferred_element_type=F32)


def _rms(x, w):
    return x * lax.rsqrt(jnp.mean(x * x, axis=-1, keepdims=True) + EPS) * w


def _split3(x):
    hi = x.astype(BF16)
    r1 = x - hi.astype(F32)
    mid = r1.astype(BF16)
    lo = (r1 - mid.astype(F32)).astype(BF16)
    return hi, mid, lo


def _cumsum_rows(tril, x):
    hi, mid, lo = _split3(x)
    return _dot(tril, hi) + _dot(tril, mid) + _dot(tril, lo)


def _tril_masks(n):
    r = lax.broadcasted_iota(jnp.int32, (n, n), 0)
    c = lax.broadcasted_iota(jnp.int32, (n, n), 1)
    return r >= c, r > c


def _norm_proj_kernel(x_ref, nw_ref, *refs):
    n_out = len(refs) // 2
    h = _rms(x_ref[...], nw_ref[...]).astype(BF16)
    for w_ref, o_ref in zip(refs[:n_out], refs[n_out:]):
        o_ref[...] = _dot(h, w_ref[...]).astype(o_ref.dtype)


def _norm_proj(x, norm_w, weights, out_dtypes):
    n, d = x.shape
    in_specs = [pl.BlockSpec((ROW_TILE, d), lambda i: (i, 0)), _resident((1, d))]
    in_specs += [_resident(w.shape) for w in weights]
    out_shape = [jax.ShapeDtypeStruct((n, w.shape[1]), dt) for w, dt in zip(weights, out_dtypes)]
    out_specs = [pl.BlockSpec((ROW_TILE, w.shape[1]), lambda i: (i, 0)) for w in weights]
    return pl.pallas_call(
        _norm_proj_kernel, grid=(n // ROW_TILE,), in_specs=in_specs, out_specs=out_specs,
        out_shape=out_shape, compiler_params=_cparams("parallel"), name="norm_proj",
    )(x, norm_w.reshape(1, d), *weights)


def _tail_kernel(x_ref, a_ref, b_ref, woa_ref, wob_ref, nw_ref, wg_ref, wu_ref, wd_ref, fn_ref,
                 o_ref, acc_ref, *, d_ff, final_norm):
    x1 = x_ref[...] + _dot(a_ref[...], woa_ref[...]) + _dot(b_ref[...], wob_ref[...])
    h = _rms(x1, nw_ref[...]).astype(BF16)
    acc_ref[...] = x1
    ffn = None
    for c in range(d_ff // FFN_CHUNK):
        cols = slice(c * FFN_CHUNK, (c + 1) * FFN_CHUNK)
        g = _dot(h, wg_ref[:, cols])
        u = _dot(h, wu_ref[:, cols])
        part = _dot((g * jax.nn.sigmoid(g) * u).astype(BF16), wd_ref[cols, :])
        ffn = part if ffn is None else ffn + part
    y = acc_ref[...] + ffn
    if final_norm:
        y = _rms(y, fn_ref[...])
    o_ref[...] = y


def _layer_tail(x, mix_a, mix_b, wo_a, wo_b, norm_w, wg, wu, wd, final_w, final_norm):
    n, d = x.shape
    d_ff = wg.shape[1]
    row = lambda w: pl.BlockSpec((ROW_TILE, w), lambda i: (i, 0))
    in_specs = [row(d), row(mix_a.shape[1]), row(mix_b.shape[1]), _resident(wo_a.shape),
                _resident(wo_b.shape), _resident((1, d)), _resident(wg.shape), _resident(wu.shape),
                _resident(wd.shape), _resident((1, d))]
    return pl.pallas_call(
        functools.partial(_tail_kernel, d_ff=d_ff, final_norm=final_norm),
        grid=(n // ROW_TILE,), in_specs=in_specs, out_specs=row(d),
        out_shape=jax.ShapeDtypeStruct((n, d), F32),
        scratch_shapes=[pltpu.VMEM((ROW_TILE, d), F32)],
        compiler_params=_cparams("parallel"), name="layer_tail",
    )(x, mix_a, mix_b, wo_a, wo_b, norm_w.reshape(1, d), wg, wu, wd, final_w.reshape(1, d))


def _conv_kernel(x_ref, halo_ref, w_ref, b_ref, o_ref, pad_ref, *, seq_len):
    tm = x_ref.shape[0]
    at_seq_start = (pl.program_id(0) * tm) % seq_len == 0
    pad_ref[0:SUBLANES, :] = jnp.where(at_seq_start, 0.0, halo_ref[...])
    pad_ref[SUBLANES:SUBLANES + tm, :] = x_ref[...]
    acc = b_ref[...] + w_ref[CONV_K - 1:CONV_K, :] * x_ref[...]
    for back in range(1, CONV_K):
        tap = w_ref[CONV_K - 1 - back:CONV_K - back, :]
        acc = acc + tap * pad_ref[SUBLANES - back:SUBLANES - back + tm, :]
    o_ref[...] = acc * jax.nn.sigmoid(acc)


def _conv_silu(x, w, b, seq_len):
    n, c = x.shape
    tm = ROW_TILE
    halo_blocks = tm // SUBLANES
    return pl.pallas_call(
        functools.partial(_conv_kernel, seq_len=seq_len),
        grid=(n // tm,),
        in_specs=[pl.BlockSpec((tm, c), lambda i: (i, 0)),
                  pl.BlockSpec((SUBLANES, c), lambda i: (jnp.maximum(i * halo_blocks - 1, 0), 0)),
                  _resident((CONV_K, c)), _resident((1, c))],
        out_specs=pl.BlockSpec((tm, c), lambda i: (i, 0)),
        out_shape=jax.ShapeDtypeStruct((n, c), F32),
        scratch_shapes=[pltpu.VMEM((SUBLANES + tm, c), F32)],
        compiler_params=_cparams("parallel"), name="conv_silu",
    )(x, x, w, b.reshape(1, c))


def _band_kernel(q_ref, *refs):
    n_blk = BAND_WIN // BAND_TQ
    k_refs, v_refs = refs[:n_blk], refs[n_blk:2 * n_blk]
    bias_ref, o_ref = refs[2 * n_blk], refs[2 * n_blk + 1]
    i = pl.program_id(2)
    k = jnp.concatenate([r[...] for r in k_refs], axis=0)
    v = jnp.concatenate([r[...] for r in v_refs], axis=0)
    q = q_ref[...] * jnp.asarray(A_HEAD_DIM ** -0.5, BF16)
    lane = lax.broadcasted_iota(jnp.int32, q.shape, 1)
    key_blk = lax.broadcasted_iota(jnp.int32, (1, BAND_WIN), 1) // BAND_TQ
    in_seq = key_blk >= (n_blk - 1) - i
    outs = []
    for half in range(2):
        sel = (lane < A_HEAD_DIM) if half == 0 else (lane >= A_HEAD_DIM)
        s = _dot_nt(jnp.where(sel, q, jnp.zeros_like(q)), k) + bias_ref[half]
        s = jnp.where(in_seq, s, NEG_BIG)
        p = jnp.exp(s - jnp.max(s, axis=-1, keepdims=True))
        outs.append(_dot(p.astype(BF16), v) / jnp.sum(p, axis=-1, keepdims=True))
    o_ref[...] = jnp.where(lane < A_HEAD_DIM, outs[0], outs[1]).astype(o_ref.dtype)


def _band_bias_tiles(rel_bias):
    r = np.arange(BAND_TQ)[:, None]
    c = np.arange(BAND_WIN)[None, :]
    offs = np.arange(-(BAND_TQ - 1), BAND_WIN)
    idx = np.clip(A_LEFT_CHUNKS * CHUNK - offs, -A_MAX_REL, A_MAX_REL) + A_MAX_REL
    period = offs.size + 1
    per_off = jnp.pad(rel_bias.astype(F32)[:, idx], ((0, 0), (0, 1)))
    skew = jnp.tile(per_off, (1, BAND_TQ))[:, :BAND_TQ * (period - 1)]
    skew = skew.reshape(-1, BAND_TQ, period - 1)[:, :, BAND_TQ - 1:BAND_TQ - 1 + BAND_WIN]
    qc, kc = r // CHUNK, c // CHUNK
    allowed = (kc >= qc) & (kc <= qc + A_LEFT_CHUNKS)
    return jnp.where(allowed[None], skew, NEG_BIG)


def _band_attention(q, k, v, rel_bias, bsz, seq_len):
    n_blk = BAND_WIN // BAND_TQ
    nq = seq_len // BAND_TQ
    pairs = A_W // LANES
    q3, k3, v3 = (a.reshape(bsz, seq_len, A_W) for a in (q, k, v))
    blk = (None, BAND_TQ, LANES)
    kv_specs = [pl.BlockSpec(blk, lambda b, p, i, j=j: (b, jnp.maximum(i - (n_blk - 1) + j, 0), p))
                for j in range(n_blk)]
    out = pl.pallas_call(
        _band_kernel, grid=(bsz, pairs, nq),
        in_specs=[pl.BlockSpec(blk, lambda b, p, i: (b, i, p))] + kv_specs + kv_specs
        + [pl.BlockSpec((2, BAND_TQ, BAND_WIN), lambda b, p, i: (p, 0, 0))],
        out_specs=pl.BlockSpec(blk, lambda b, p, i: (b, i, p)),
        out_shape=jax.ShapeDtypeStruct((bsz, seq_len, A_W), BF16),
        compiler_params=_cparams("parallel", "parallel", "parallel"), name="band_attention",
    )(q3, *([k3] * n_blk), *([v3] * n_blk), _band_bias_tiles(rel_bias))
    return out.reshape(bsz * seq_len, A_W)


def _gdn_kernel(qkv_ref, sm_ref, z_ref, par_ref, nw_ref, o_ref, state_ref, lhs_ref, add_ref, gl_ref):
    bsz, rows_per_step = qkv_ref.shape[0], qkv_ref.shape[1]
    n_chunks = rows_per_step // CHUNK
    units = [(b, h) for b in range(bsz) for h in range(B_HEADS)]

    @pl.when(pl.program_id(0) == 0)
    def _():
        state_ref[...] = jnp.zeros_like(state_ref)

    causal, strict = _tril_masks(CHUNK)
    tril = jnp.where(causal, 1.0, 0.0).astype(BF16)
    eye = jnp.where(causal & ~strict, 1.0, 0.0)
    a_row = -jnp.exp(par_ref[0:1, :])
    dt_bias_row = par_ref[1:2, :]

    def head_cols(part, h):
        return slice(part * B_W + h * B_HEAD_DIM, part * B_W + (h + 1) * B_HEAD_DIM)

    def build(c, carry):
        rows = pl.ds(pl.multiple_of(c * CHUNK, CHUNK), CHUNK)
        sm = [sm_ref[b, rows, :] for b in range(bsz)]
        beta_all = [jax.nn.sigmoid(x) for x in sm]
        gc_all = [_cumsum_rows(tril, a_row * jax.nn.softplus(x + dt_bias_row)) for x in sm]
        gc_all_t = [x.T for x in gc_all]
        q = [qkv_ref[b, rows, head_cols(0, h)] for b, h in units]
        k = [qkv_ref[b, rows, head_cols(1, h)] for b, h in units]
        v = [qkv_ref[b, rows, head_cols(2, h)] for b, h in units]
        q = [x * lax.rsqrt(jnp.sum(x * x, axis=-1, keepdims=True) + EPS) * (B_HEAD_DIM ** -0.5) for x in q]
        k = [x * lax.rsqrt(jnp.sum(x * x, axis=-1, keepdims=True) + EPS) for x in k]
        beta = [beta_all[b][:, h:h + 1] for b, h in units]
        gc = [gc_all[b][:, B_HEADS + h:B_HEADS + h + 1] for b, h in units]
        gc_row = [gc_all_t[b][B_HEADS + h:B_HEADS + h + 1, :] for b, h in units]
        decay = [jnp.exp(jnp.where(causal, g - gr, NEG_BIG)) for g, gr in zip(gc, gc_row)]
        kb = [x.astype(BF16) for x in k]
        kk = [_dot_nt(x, x) for x in kb]
        qk = [_dot_nt(x.astype(BF16), y) for x, y in zip(q, kb)]
        a_mat = [jnp.where(strict, bt * x * d, 0.0) for bt, x, d in zip(beta, kk, decay)]
        inv = [eye - a for a in a_mat]
        power = a_mat
        for _ in range(5):
            power = [_dot(x, x) for x in power]
            inv = [i + _dot(i, x) for i, x in zip(inv, power)]
        exp_gc = [jnp.exp(g) for g in gc]
        rhs = [jnp.concatenate([ki * (bt * e), vi * bt], axis=1) for ki, vi, bt, e in zip(k, v, beta, exp_gc)]
        wu = [_dot(i, r).astype(BF16) for i, r in zip(inv, rhs)]
        attn = [(x * d).astype(BF16) for x, d in zip(qk, decay)]
        gc_last = [g[CHUNK - 1:CHUNK, :] for g in gc]
        k_st = [(ki * jnp.exp(gl - g)).astype(BF16) for ki, gl, g in zip(k, gc_last, gc)]
        top = [_dot_tn(x, y) for x, y in zip(k_st, wu)]
        bot = [_dot(x, y) for x, y in zip(attn, wu)]
        for i in range(len(units)):
            slot = c * len(units) + i
            lhs_ref[slot, 0:B_HEAD_DIM, :] = (-top[i][:, :B_HEAD_DIM]).astype(BF16)
            lhs_ref[slot, B_HEAD_DIM:, :] = (q[i] * exp_gc[i] - bot[i][:, :B_HEAD_DIM]).astype(BF16)
            add_ref[slot, 0:B_HEAD_DIM, :] = top[i][:, B_HEAD_DIM:]
            add_ref[slot, B_HEAD_DIM:, :] = bot[i][:, B_HEAD_DIM:]
            gl_ref[slot] = jnp.broadcast_to(jnp.exp(gc_last[i]), (SUBLANES, LANES))
        return carry

    lax.fori_loop(0, n_chunks, build, 0)

    def scan(c, carry):
        rows = pl.ds(pl.multiple_of(c * CHUNK, CHUNK), CHUNK)
        states = [state_ref[i] for i in range(len(units))]
        res = [_dot(lhs_ref[c * len(units) + i], states[i].astype(BF16)) + add_ref[c * len(units) + i]
               for i in range(len(units))]
        for i, (b, h) in enumerate(units):
            state_ref[i] = gl_ref[c * len(units) + i][0:1, :] * states[i] + res[i][:B_HEAD_DIM]
            zz = z_ref[b, rows, head_cols(0, h)]
            y = _rms(res[i][B_HEAD_DIM:], nw_ref[...]) * (zz * jax.nn.sigmoid(zz))
            o_ref[b, rows, head_cols(0, h)] = y.astype(o_ref.dtype)
        return carry

    lax.fori_loop(0, n_chunks, scan, 0)


def _gated_delta(qkv, small, z, a_log, dt_bias, norm_w, bsz, seq_len):
    par = jnp.zeros((2, LANES), F32)
    par = par.at[0, B_HEADS:2 * B_HEADS].set(a_log.astype(F32))
    par = par.at[1, B_HEADS:2 * B_HEADS].set(dt_bias.astype(F32))
    blk = lambda w: pl.BlockSpec((bsz, SCAN_ROWS, w), lambda t: (0, t, 0))
    n_units = bsz * B_HEADS
    n_slots = n_units * (SCAN_ROWS // CHUNK)
    out = pl.pallas_call(
        _gdn_kernel, grid=(seq_len // SCAN_ROWS,),
        in_specs=[blk(3 * B_W), blk(LANES), blk(B_W), _resident((2, LANES)), _resident((1, B_HEAD_DIM))],
        out_specs=blk(B_W),
        out_shape=jax.ShapeDtypeStruct((bsz, seq_len, B_W), BF16),
        scratch_shapes=[pltpu.VMEM((n_units, B_HEAD_DIM, B_HEAD_DIM), F32),
                        pltpu.VMEM((n_slots, B_HEAD_DIM + CHUNK, B_HEAD_DIM), BF16),
                        pltpu.VMEM((n_slots, B_HEAD_DIM + CHUNK, B_HEAD_DIM), F32),
                        pltpu.VMEM((n_slots, SUBLANES, LANES), F32)],
        compiler_params=_cparams("arbitrary"), name="gated_delta",
    )(qkv.reshape(bsz, seq_len, 3 * B_W), small.reshape(bsz, seq_len, LANES),
      z.reshape(bsz, seq_len, B_W), par, norm_w.reshape(1, B_HEAD_DIM).astype(F32))
    return out.reshape(bsz * seq_len, B_W)


def _ssd_kernel(xbc_ref, sm_ref, z_ref, par_ref, skip_ref, nw_ref, o_ref, state_ref):
    bsz, rows_per_step = xbc_ref.shape[0], xbc_ref.shape[1]
    group_w = C_W // C_GROUPS
    heads_per_group = C_HEADS // C_GROUPS

    @pl.when(pl.program_id(0) == 0)
    def _():
        state_ref[...] = jnp.zeros_like(state_ref)

    causal, _ = _tril_masks(CHUNK)
    tril = jnp.where(causal, 1.0, 0.0).astype(BF16)
    a_row = -jnp.exp(par_ref[0:1, :])
    dt_bias_row = par_ref[1:2, :]
    lane = lax.broadcasted_iota(jnp.int32, (CHUNK, LANES), 1)
    low_half = lane < C_HEAD_DIM

    def per_head_lanes(cols, h0):
        return jnp.where(low_half, cols[:, h0:h0 + 1], cols[:, h0 + 1:h0 + 2])

    def chunk_step(c, carry):
        rows = pl.ds(pl.multiple_of(c * CHUNK, CHUNK), CHUNK)
        for b in range(bsz):
            sm = sm_ref[b, rows, :]
            dt_all = jax.nn.softplus(sm + dt_bias_row)
            da_cs = _cumsum_rows(tril, dt_all * a_row)
            da_cs_t = da_cs.T
            dt_t = dt_all.T
            da_last = da_cs[CHUNK - 1:CHUNK, :]
            exp_da = jnp.exp(da_cs)
            to_end = jnp.exp(da_last - da_cs) * dt_all
            chunk_decay = jnp.exp(da_last)
            for g in range(C_GROUPS):
                gl = slice(g * group_w, (g + 1) * group_w)
                bm = xbc_ref[b, rows, C_W + g * C_STATE:C_W + (g + 1) * C_STATE].astype(BF16)
                cm = xbc_ref[b, rows, C_W + C_BC + g * C_STATE:C_W + C_BC + (g + 1) * C_STATE].astype(BF16)
                cb = _dot_nt(cm, bm)
                prev = state_ref[b, :, gl]
                y_off = _dot(cm, prev.astype(BF16))
                ys, xws, decs = [], [], []
                for pr in range(heads_per_group // 2):
                    h0 = g * heads_per_group + 2 * pr
                    pl_ = slice(h0 * C_HEAD_DIM, (h0 + 2) * C_HEAD_DIM)
                    x2 = xbc_ref[b, rows, pl_]
                    x2b = x2.astype(BF16)
                    yd = []
                    for h in (h0, h0 + 1):
                        seg = da_cs[:, h:h + 1] - da_cs_t[h:h + 1, :]
                        wts = cb * jnp.exp(jnp.where(causal, seg, NEG_BIG)) * dt_t[h:h + 1, :]
                        yd.append(_dot(wts.astype(BF16), x2b))
                    y_diag = jnp.where(low_half, yd[0], yd[1])
                    off = y_off[:, 2 * pr * C_HEAD_DIM:(2 * pr + 2) * C_HEAD_DIM] * per_head_lanes(exp_da, h0)
                    y = y_diag + off + skip_ref[:, pl_] * x2
                    zz = z_ref[b, rows, pl_]
                    ys.append(y * (zz * jax.nn.sigmoid(zz)))
                    xws.append((x2 * per_head_lanes(to_end, h0)).astype(BF16))
                    decs.append(jnp.where(low_half[0:1], chunk_decay[:, h0:h0 + 1], chunk_decay[:, h0 + 1:h0 + 2]))
                xw = jnp.concatenate(xws, axis=1)
                dec = jnp.concatenate(decs, axis=1)
                state_ref[b, :, gl] = prev * dec + _dot_tn(bm, xw)
                yg = jnp.concatenate(ys, axis=1)
                o_ref[b, rows, gl] = _rms(yg, nw_ref[:, gl]).astype(o_ref.dtype)
        return carry

    lax.fori_loop(0, rows_per_step // CHUNK, chunk_step, 0)


def _ssd(xbc, small, z, dt_bias, a_log, d_skip, norm_w, bsz, seq_len):
    par = jnp.zeros((2, LANES), F32)
    par = par.at[0, :C_HEADS].set(a_log.astype(F32))
    par = par.at[1, :C_HEADS].set(dt_bias.astype(F32))
    skip = jnp.repeat(d_skip.astype(F32), C_HEAD_DIM).reshape(1, C_W)
    blk = lambda w: pl.BlockSpec((bsz, SCAN_ROWS, w), lambda t: (0, t, 0))
    out = pl.pallas_call(
        _ssd_kernel, grid=(seq_len // SCAN_ROWS,),
        in_specs=[blk(C_W + 2 * C_BC), blk(LANES), blk(C_W), _resident((2, LANES)),
                  _resident((1, C_W)), _resident((1, C_W))],
        out_specs=blk(C_W),
        out_shape=jax.ShapeDtypeStruct((bsz, seq_len, C_W), BF16),
        scratch_shapes=[pltpu.VMEM((bsz, C_STATE, C_W), F32)],
        compiler_params=_cparams("arbitrary"), name="ssd",
    )(xbc.reshape(bsz, seq_len, C_W + 2 * C_BC), small.reshape(bsz, seq_len, LANES),
      z.reshape(bsz, seq_len, C_W), par, skip, norm_w.reshape(1, C_W).astype(F32))
    return out.reshape(bsz * seq_len, C_W)


def _fox_prep_kernel(sm_ref, fb_ref, ft_ref, carry_ref):
    tm = sm_ref.shape[0]

    @pl.when(pl.program_id(1) == 0)
    def _():
        carry_ref[...] = jnp.zeros_like(carry_ref)

    causal, _ = _tril_masks(tm)
    tril = jnp.where(causal, 1.0, 0.0).astype(BF16)
    log_f = jax.nn.log_sigmoid(sm_ref[...] + fb_ref[...])
    f_cum = _cumsum_rows(tril, log_f) + carry_ref[...]
    carry_ref[...] = f_cum[tm - 1:tm, :]
    ft_ref[...] = f_cum.T[D_HEADS:2 * D_HEADS, :]


def _fox_prep(small, f_bias, bsz, seq_len):
    fb = jnp.zeros((1, LANES), F32).at[0, D_HEADS:2 * D_HEADS].set(f_bias.astype(F32))
    nt = seq_len // FOX_T
    return pl.pallas_call(
        _fox_prep_kernel, grid=(bsz, nt),
        in_specs=[pl.BlockSpec((None, FOX_T, LANES), lambda b, t: (b, t, 0)), _resident((1, LANES))],
        out_specs=pl.BlockSpec((None, None, D_HEADS, FOX_T), lambda b, t: (b, t, 0, 0)),
        out_shape=jax.ShapeDtypeStruct((bsz, nt, D_HEADS, FOX_T), F32),
        scratch_shapes=[pltpu.VMEM((1, LANES), F32)],
        compiler_params=_cparams("parallel", "arbitrary"), name="fox_prep",
    )(small.reshape(bsz, seq_len, LANES), fb)


def _fox_kernel(q_ref, k_ref, v_ref, ft_ref, o_ref):
    p = pl.program_id(1)
    i = pl.program_id(2)
    t = FOX_T
    q = q_ref[...] * jnp.asarray(D_HEAD_DIM ** -0.5, BF16)
    lane = lax.broadcasted_iota(jnp.int32, (t, LANES), 1)
    low_half = lane < D_HEAD_DIM
    qs = (jnp.where(low_half, q, jnp.zeros_like(q)), jnp.where(low_half, jnp.zeros_like(q), q))
    on_or_below_diag = _tril_masks(t)[0]

    def step(j, carry, diagonal):
        kj = k_ref[pl.ds(pl.multiple_of(j * t, t), t), :]
        vj = v_ref[pl.ds(pl.multiple_of(j * t, t), t), :]
        new = []
        for half in range(2):
            m, l, acc = carry[3 * half:3 * half + 3]
            s = _dot_nt(qs[half], kj) - ft_ref[j, pl.ds(2 * p + half, 1), :]
            if diagonal:
                s = jnp.where(on_or_below_diag, s, NEG_BIG)
            m_new = jnp.maximum(m, jnp.max(s, axis=-1, keepdims=True))
            alpha = jnp.exp(m - m_new)
            pexp = jnp.exp(s - m_new)
            l = alpha * l + jnp.sum(pexp, axis=-1, keepdims=True)
            acc = alpha * acc + _dot(pexp.astype(BF16), vj)
            new += [m_new, l, acc]
        return tuple(new)

    init = (jnp.full((t, 1), NEG_BIG, F32), jnp.zeros((t, 1), F32), jnp.zeros((t, LANES), F32)) * 2
    carry = lax.fori_loop(0, i, lambda j, c: step(j, c, False), init)
    carry = step(i, carry, True)
    o_ref[...] = jnp.where(low_half, carry[2] / carry[1], carry[5] / carry[4]).astype(o_ref.dtype)


def _fox_attention(q, k, v, ft, bsz, seq_len):
    nt = seq_len // FOX_T
    pairs = D_W // LANES
    q3, k3, v3 = (a.reshape(bsz, seq_len, D_W) for a in (q, k, v))
    out = pl.pallas_call(
        _fox_kernel, grid=(bsz, pairs, nt),
        in_specs=[pl.BlockSpec((None, FOX_T, LANES), lambda b, p, i: (b, i, p)),
                  pl.BlockSpec((None, seq_len, LANES), lambda b, p, i: (b, 0, p)),
                  pl.BlockSpec((None, seq_len, LANES), lambda b, p, i: (b, 0, p)),
                  pl.BlockSpec((None, nt, D_HEADS, FOX_T), lambda b, p, i: (b, 0, 0, 0))],
        out_specs=pl.BlockSpec((None, FOX_T, LANES), lambda b, p, i: (b, i, p)),
        out_shape=jax.ShapeDtypeStruct((bsz, seq_len, D_W), BF16),
        compiler_params=_cparams("parallel", "parallel", "parallel"), name="fox_attention",
    )(q3, k3, v3, ft)
    return out.reshape(bsz * seq_len, D_W)


def _pad_cols(w, width=LANES):
    return jnp.pad(w, ((0, 0), (0, width - w.shape[1])))


def kernel(x, norm_mix, norm_ffn, norm_final, ffn_w_gate, ffn_w_up, ffn_w_down, ab_w_in, ab_rel_bias, ab_conv_w, ab_a_log, ab_dt_bias, ab_norm_w, ab_w_out, cd_w_in, cd_conv_w, cd_conv_b, cd_dt_bias, cd_a_log, cd_d_skip, cd_norm_w, cd_f_bias, cd_w_out):
    bsz, seq_len, d = x.shape
    n = bsz * seq_len
    xf = x.reshape(n, d)
    bf = lambda w: w.astype(BF16)

    w_in = ab_w_in[0]
    o = np.cumsum([0, A_W, A_W, A_W, 3 * B_W, B_HEADS, B_HEADS, B_W])
    weights = [bf(w_in[:, o[0]:o[1]]), bf(w_in[:, o[1]:o[2]]), bf(w_in[:, o[2]:o[3]]),
               bf(w_in[:, o[3]:o[4]]), bf(_pad_cols(w_in[:, o[4]:o[6]])), bf(w_in[:, o[6]:o[7]])]
    a_q, a_k, a_v, b_qkv, b_small, b_z = _norm_proj(
        xf, norm_mix[0], weights, [BF16, BF16, BF16, F32, F32, F32])
    o_a = _band_attention(a_q, a_k, a_v, ab_rel_bias[0], bsz, seq_len)
    qkv = _conv_silu(b_qkv, ab_conv_w[0].astype(F32), jnp.zeros((3 * B_W,), F32), seq_len)
    o_b = _gated_delta(qkv, b_small, b_z, ab_a_log[0], ab_dt_bias[0], ab_norm_w[0], bsz, seq_len)
    w_out = ab_w_out[0]
    xf = _layer_tail(xf, o_a, o_b, bf(w_out[:A_W]), bf(w_out[A_W:]), norm_ffn[0],
                     bf(ffn_w_gate[0]), bf(ffn_w_up[0]), bf(ffn_w_down[0]), norm_final, False)

    w_in = cd_w_in[0]
    o = np.cumsum([0, C_W, C_W + 2 * C_BC, C_HEADS, D_W, D_W, D_W, D_HEADS])
    small_w = jnp.concatenate([w_in[:, o[2]:o[3]], w_in[:, o[6]:o[7]]], axis=1)
    weights = [bf(w_in[:, o[0]:o[1]]), bf(w_in[:, o[1]:o[2]]), bf(_pad_cols(small_w)),
               bf(w_in[:, o[3]:o[4]]), bf(w_in[:, o[4]:o[5]]), bf(w_in[:, o[5]:o[6]])]
    c_z, c_xbc, cd_small, d_q, d_k, d_v = _norm_proj(
        xf, norm_mix[1], weights, [F32, F32, F32, BF16, BF16, BF16])
    xbc = _conv_silu(c_xbc, cd_conv_w[0].astype(F32), cd_conv_b[0].astype(F32), seq_len)
    y_c = _ssd(xbc, cd_small, c_z, cd_dt_bias[0], cd_a_log[0], cd_d_skip[0], cd_norm_w[0], bsz, seq_len)
    ft = _fox_prep(cd_small, cd_f_bias[0], bsz, seq_len)
    o_d = _fox_attention(d_q, d_k, d_v, ft, bsz, seq_len)
    w_out = cd_w_out[0]
    xf = _layer_tail(xf, y_c, o_d, bf(w_out[:C_W]), bf(w_out[C_W:]), norm_ffn[1],
                     bf(ffn_w_gate[1]), bf(ffn_w_up[1]), bf(ffn_w_down[1]), norm_final, True)
    return xf.reshape(bsz, seq_len, d)
```

```python
import functools

import jax
import jax.numpy as jnp
import numpy as np
from jax import lax
from jax.experimental import pallas as pl
from jax.experimental.pallas import tpu as pltpu

F32 = jnp.float32
BF16 = jnp.bfloat16

D_MODEL = 1024
CHUNK = 64
EPS = 1e-6
CONV_K = 4
A_HEADS, A_HEAD_DIM, A_LEFT_CHUNKS, A_MAX_REL = 8, 64, 8, 256
B_HEADS, B_HEAD_DIM = 4, 128
C_HEADS, C_HEAD_DIM, C_GROUPS, C_STATE = 8, 64, 2, 128
D_HEADS, D_HEAD_DIM = 8, 64
A_W = A_HEADS * A_HEAD_DIM
B_W = B_HEADS * B_HEAD_DIM
C_W = C_HEADS * C_HEAD_DIM
D_W = D_HEADS * D_HEAD_DIM
C_BC = C_GROUPS * C_STATE

LANES = 128
SUBLANES = 8
VMEM_LIMIT_BYTES = 56 * 1024 * 1024
NEG_BIG = -1e30

ROW_TILE = 512
FFN_CHUNK = 256
BAND_TQ = 128
BAND_WIN = BAND_TQ + A_LEFT_CHUNKS * CHUNK
FOX_T = 512
FOX_ROWS = 32
SCAN_ROWS = 512


def _cparams(*sem):
    return pltpu.CompilerParams(dimension_semantics=sem, vmem_limit_bytes=VMEM_LIMIT_BYTES)


def _resident(shape):
    nd = len(shape)
    return pl.BlockSpec(shape, lambda *_: (0,) * nd, pipeline_mode=pl.Buffered(1))


def _dot(a, b):
    return jnp.dot(a, b, preferred_element_type=F32)


def _dot_nt(a, b):
    return lax.dot_general(a, b, (((1,), (1,)), ((), ())), preferred_element_type=F32)


def _dot_tn(a, b):
    return lax.dot_general(a, b, (((0,), (0,)), ((), ())), preferred_element_type=F32)


def _rms(x, w):
    return x * lax.rsqrt(jnp.mean(x * x, axis=-1, keepdims=True) + EPS) * w


def _split3(x):
    hi = x.astype(BF16)
    r1 = x - hi.astype(F32)
    mid = r1.astype(BF16)
    lo = (r1 - mid.astype(F32)).astype(BF16)
    return hi, mid, lo


def _cumsum_rows(tril, x):
    hi, mid, lo = _split3(x)
    return _dot(tril, hi) + _dot(tril, mid) + _dot(tril, lo)


def _tril_masks(n):
    r = lax.broadcasted_iota(jnp.int32, (n, n), 0)
    c = lax.broadcasted_iota(jnp.int32, (n, n), 1)
    return r >= c, r > c


def _norm_proj_kernel(x_ref, nw_ref, *refs):
    n_out = len(refs) // 2
    h = _rms(x_ref[...], nw_ref[...]).astype(BF16)
    for w_ref, o_ref in zip(refs[:n_out], refs[n_out:]):
        o_ref[...] = _dot(h, w_ref[...]).astype(o_ref.dtype)


def _norm_proj(x, norm_w, weights, out_dtypes):
    n, d = x.shape
    in_specs = [pl.BlockSpec((ROW_TILE, d), lambda i: (i, 0)), _resident((1, d))]
    in_specs += [_resident(w.shape) for w in weights]
    out_shape = [jax.ShapeDtypeStruct((n, w.shape[1]), dt) for w, dt in zip(weights, out_dtypes)]
    out_specs = [pl.BlockSpec((ROW_TILE, w.shape[1]), lambda i: (i, 0)) for w in weights]
    return pl.pallas_call(
        _norm_proj_kernel, grid=(n // ROW_TILE,), in_specs=in_specs, out_specs=out_specs,
        out_shape=out_shape, compiler_params=_cparams("parallel"), name="norm_proj",
    )(x, norm_w.reshape(1, d), *weights)


def _tail_kernel(x_ref, a_ref, b_ref, woa_ref, wob_ref, nw_ref, wg_ref, wu_ref, wd_ref, fn_ref,
                 o_ref, acc_ref, *, d_ff, final_norm):
    x1 = x_ref[...] + _dot(a_ref[...], woa_ref[...]) + _dot(b_ref[...], wob_ref[...])
    h = _rms(x1, nw_ref[...]).astype(BF16)
    acc_ref[...] = x1
    ffn = None
    for c in range(d_ff // FFN_CHUNK):
        cols = slice(c * FFN_CHUNK, (c + 1) * FFN_CHUNK)
        g = _dot(h, wg_ref[:, cols])
        u = _dot(h, wu_ref[:, cols])
        part = _dot((g * jax.nn.sigmoid(g) * u).astype(BF16), wd_ref[cols, :])
        ffn = part if ffn is None else ffn + part
    y = acc_ref[...] + ffn
    if final_norm:
        y = _rms(y, fn_ref[...])
    o_ref[...] = y


def _layer_tail(x, mix_a, mix_b, wo_a, wo_b, norm_w, wg, wu, wd, final_w, final_norm):
    n, d = x.shape
    d_ff = wg.shape[1]
    row = lambda w: pl.BlockSpec((ROW_TILE, w), lambda i: (i, 0))
    in_specs = [row(d), row(mix_a.shape[1]), row(mix_b.shape[1]), _resident(wo_a.shape),
                _resident(wo_b.shape), _resident((1, d)), _resident(wg.shape), _resident(wu.shape),
                _resident(wd.shape), _resident((1, d))]
    return pl.pallas_call(
        functools.partial(_tail_kernel, d_ff=d_ff, final_norm=final_norm),
        grid=(n // ROW_TILE,), in_specs=in_specs, out_specs=row(d),
        out_shape=jax.ShapeDtypeStruct((n, d), F32),
        scratch_shapes=[pltpu.VMEM((ROW_TILE, d), F32)],
        compiler_params=_cparams("parallel"), name="layer_tail",
    )(x, mix_a, mix_b, wo_a, wo_b, norm_w.reshape(1, d), wg, wu, wd, final_w.reshape(1, d))


def _conv_kernel(x_ref, halo_ref, w_ref, b_ref, o_ref, pad_ref, *, seq_len):
    tm = x_ref.shape[0]
    at_seq_start = (pl.program_id(0) * tm) % seq_len == 0
    pad_ref[0:SUBLANES, :] = jnp.where(at_seq_start, 0.0, halo_ref[...])
    pad_ref[SUBLANES:SUBLANES + tm, :] = x_ref[...]
    acc = b_ref[...] + w_ref[CONV_K - 1:CONV_K, :] * x_ref[...]
    for back in range(1, CONV_K):
        tap = w_ref[CONV_K - 1 - back:CONV_K - back, :]
        acc = acc + tap * pad_ref[SUBLANES - back:SUBLANES - back + tm, :]
    o_ref[...] = acc * jax.nn.sigmoid(acc)


def _conv_silu(x, w, b, seq_len):
    n, c = x.shape
    tm = ROW_TILE
    halo_blocks = tm // SUBLANES
    return pl.pallas_call(
        functools.partial(_conv_kernel, seq_len=seq_len),
        grid=(n // tm,),
        in_specs=[pl.BlockSpec((tm, c), lambda i: (i, 0)),
                  pl.BlockSpec((SUBLANES, c), lambda i: (jnp.maximum(i * halo_blocks - 1, 0), 0)),
                  _resident((CONV_K, c)), _resident((1, c))],
        out_specs=pl.BlockSpec((tm, c), lambda i: (i, 0)),
        out_shape=jax.ShapeDtypeStruct((n, c), F32),
        scratch_shapes=[pltpu.VMEM((SUBLANES + tm, c), F32)],
        compiler_params=_cparams("parallel"), name="conv_silu",
    )(x, x, w, b.reshape(1, c))


def _band_kernel(q_ref, *refs):
    n_blk = BAND_WIN // BAND_TQ
    k_refs, v_refs = refs[:n_blk], refs[n_blk:2 * n_blk]
    bias_ref, o_ref, s_ref, p_ref = refs[2 * n_blk:]
    i = pl.program_id(1)
    lane = lax.broadcasted_iota(jnp.int32, (BAND_TQ, LANES), 1)
    low_half = lane < A_HEAD_DIM
    win_lane = lax.broadcasted_iota(jnp.int32, (BAND_WIN, LANES), 1)
    key_blk = lax.broadcasted_iota(jnp.int32, (1, BAND_WIN), 1) // BAND_TQ
    before_start = jnp.where(key_blk >= (n_blk - 1) - i, 0.0, NEG_BIG)
    pairs = A_W // LANES
    n_rb = BAND_TQ // FOX_ROWS

    v_aug = []
    for pr in range(pairs):
        cols = slice(pr * LANES, (pr + 1) * LANES)
        k = jnp.concatenate([r[:, cols] for r in k_refs], axis=0)
        v = jnp.concatenate([r[:, cols] for r in v_refs], axis=0)
        q = q_ref[:, cols] * jnp.asarray(A_HEAD_DIM ** -0.5, BF16)
        zero, one = jnp.zeros_like(q), jnp.ones_like(v)
        s_ref[2 * pr] = _dot_nt(jnp.where(low_half, q, zero), k) + (bias_ref[2 * pr] + before_start)
        s_ref[2 * pr + 1] = _dot_nt(jnp.where(low_half, zero, q), k) + (bias_ref[2 * pr + 1] + before_start)
        v_aug += [jnp.where(win_lane < A_HEAD_DIM, v, one), jnp.where(win_lane < A_HEAD_DIM, one, v)]

    blocks = [(h, rb) for h in range(A_HEADS) for rb in range(n_rb)]
    rows = lambda rb: slice(rb * FOX_ROWS, (rb + 1) * FOX_ROWS)
    row_max = [jnp.max(s_ref[h, rows(rb), :], axis=-1, keepdims=True) for h, rb in blocks]
    for (h, rb), mx in zip(blocks, row_max):
        p_ref[h, rows(rb), :] = jnp.exp(s_ref[h, rows(rb), :] - mx).astype(BF16)

    for pr in range(pairs):
        o_even = _dot(p_ref[2 * pr], v_aug[2 * pr])
        o_odd = _dot(p_ref[2 * pr + 1], v_aug[2 * pr + 1])
        out = jnp.where(low_half, o_even / pltpu.roll(o_even, A_HEAD_DIM, axis=1),
                        o_odd / pltpu.roll(o_odd, A_HEAD_DIM, axis=1))
        o_ref[:, pr * LANES:(pr + 1) * LANES] = out.astype(o_ref.dtype)


def _band_bias_tiles(rel_bias):
    r = np.arange(BAND_TQ)[:, None]
    c = np.arange(BAND_WIN)[None, :]
    offs = np.arange(-(BAND_TQ - 1), BAND_WIN)
    idx = np.clip(A_LEFT_CHUNKS * CHUNK - offs, -A_MAX_REL, A_MAX_REL) + A_MAX_REL
    period = offs.size + 1
    per_off = jnp.pad(rel_bias.astype(F32)[:, idx], ((0, 0), (0, 1)))
    skew = jnp.tile(per_off, (1, BAND_TQ))[:, :BAND_TQ * (period - 1)]
    skew = skew.reshape(-1, BAND_TQ, period - 1)[:, :, BAND_TQ - 1:BAND_TQ - 1 + BAND_WIN]
    qc, kc = r // CHUNK, c // CHUNK
    allowed = (kc >= qc) & (kc <= qc + A_LEFT_CHUNKS)
    return jnp.where(allowed[None], skew, NEG_BIG)


def _band_attention(q, k, v, rel_bias, bsz, seq_len):
    n_blk = BAND_WIN // BAND_TQ
    nq = seq_len // BAND_TQ
    q3, k3, v3 = (a.reshape(bsz, seq_len, A_W) for a in (q, k, v))
    blk = (None, BAND_TQ, A_W)
    kv_specs = [pl.BlockSpec(blk, lambda b, i, j=j: (b, jnp.maximum(i - (n_blk - 1) + j, 0), 0))
                for j in range(n_blk)]
    out = pl.pallas_call(
        _band_kernel, grid=(bsz, nq),
        in_specs=[pl.BlockSpec(blk, lambda b, i: (b, i, 0))] + kv_specs + kv_specs
        + [_resident((A_HEADS, BAND_TQ, BAND_WIN))],
        out_specs=pl.BlockSpec(blk, lambda b, i: (b, i, 0)),
        out_shape=jax.ShapeDtypeStruct((bsz, seq_len, A_W), BF16),
        scratch_shapes=[pltpu.VMEM((A_HEADS, BAND_TQ, BAND_WIN), F32),
                        pltpu.VMEM((A_HEADS, BAND_TQ, BAND_WIN), BF16)],
        compiler_params=_cparams("parallel", "parallel"), name="band_attention",
    )(q3, *([k3] * n_blk), *([v3] * n_blk), _band_bias_tiles(rel_bias))
    return out.reshape(bsz * seq_len, A_W)


def _gdn_kernel(qkv_ref, sm_ref, z_ref, par_ref, nw_ref, o_ref, state_ref, lhs_ref, add_ref, gl_ref):
    bsz, rows_per_step = qkv_ref.shape[0], qkv_ref.shape[1]
    n_chunks = rows_per_step // CHUNK
    units = [(b, h) for b in range(bsz) for h in range(B_HEADS)]

    @pl.when(pl.program_id(0) == 0)
    def _():
        state_ref[...] = jnp.zeros_like(state_ref)

    causal, strict = _tril_masks(CHUNK)
    tril = jnp.where(causal, 1.0, 0.0).astype(BF16)
    eye = jnp.where(causal & ~strict, 1.0, 0.0)
    a_row = -jnp.exp(par_ref[0:1, :])
    dt_bias_row = par_ref[1:2, :]

    def head_cols(part, h):
        return slice(part * B_W + h * B_HEAD_DIM, part * B_W + (h + 1) * B_HEAD_DIM)

    def build(c, carry):
        rows = pl.ds(pl.multiple_of(c * CHUNK, CHUNK), CHUNK)
        sm = [sm_ref[b, rows, :] for b in range(bsz)]
        beta_all = [jax.nn.sigmoid(x) for x in sm]
        gc_all = [_cumsum_rows(tril, a_row * jax.nn.softplus(x + dt_bias_row)) for x in sm]
        gc_all_t = [x.T for x in gc_all]
        q = [qkv_ref[b, rows, head_cols(0, h)] for b, h in units]
        k = [qkv_ref[b, rows, head_cols(1, h)] for b, h in units]
        v = [qkv_ref[b, rows, head_cols(2, h)] for b, h in units]
        q = [x * lax.rsqrt(jnp.sum(x * x, axis=-1, keepdims=True) + EPS) * (B_HEAD_DIM ** -0.5) for x in q]
        k = [x * lax.rsqrt(jnp.sum(x * x, axis=-1, keepdims=True) + EPS) for x in k]
        beta = [beta_all[b][:, h:h + 1] for b, h in units]
        gc = [gc_all[b][:, B_HEADS + h:B_HEADS + h + 1] for b, h in units]
        gc_row = [gc_all_t[b][B_HEADS + h:B_HEADS + h + 1, :] for b, h in units]
        decay = [jnp.exp(jnp.where(causal, g - gr, NEG_BIG)) for g, gr in zip(gc, gc_row)]
        kb = [x.astype(BF16) for x in k]
        kk = [_dot_nt(x, x) for x in kb]
        qk = [_dot_nt(x.astype(BF16), y) for x, y in zip(q, kb)]
        a_mat = [jnp.where(strict, bt * x * d, 0.0) for bt, x, d in zip(beta, kk, decay)]
        inv = [eye - a for a in a_mat]
        power = a_mat
        for _ in range(5):
            power = [_dot(x, x) for x in power]
            inv = [i + _dot(i, x) for i, x in zip(inv, power)]
        exp_gc = [jnp.exp(g) for g in gc]
        rhs = [jnp.concatenate([ki * (bt * e), vi * bt], axis=1) for ki, vi, bt, e in zip(k, v, beta, exp_gc)]
        wu = [_dot(i, r).astype(BF16) for i, r in zip(inv, rhs)]
        attn = [(x * d).astype(BF16) for x, d in zip(qk, decay)]
        gc_last = [g[CHUNK - 1:CHUNK, :] for g in gc]
        k_st = [(ki * jnp.exp(gl - g)).astype(BF16) for ki, gl, g in zip(k, gc_last, gc)]
        top = [_dot_tn(x, y) for x, y in zip(k_st, wu)]
        bot = [_dot(x, y) for x, y in zip(attn, wu)]
        for i in range(len(units)):
            slot = c * len(units) + i
            lhs_ref[slot, 0:B_HEAD_DIM, :] = (-top[i][:, :B_HEAD_DIM]).astype(BF16)
            lhs_ref[slot, B_HEAD_DIM:, :] = (q[i] * exp_gc[i] - bot[i][:, :B_HEAD_DIM]).astype(BF16)
            add_ref[slot, 0:B_HEAD_DIM, :] = top[i][:, B_HEAD_DIM:]
            add_ref[slot, B_HEAD_DIM:, :] = bot[i][:, B_HEAD_DIM:]
            gl_ref[slot] = jnp.broadcast_to(jnp.exp(gc_last[i]), (SUBLANES, LANES))
        return carry

    lax.fori_loop(0, n_chunks, build, 0)

    def scan(c, carry):
        rows = pl.ds(pl.multiple_of(c * CHUNK, CHUNK), CHUNK)
        states = [state_ref[i] for i in range(len(units))]
        res = [_dot(lhs_ref[c * len(units) + i], states[i].astype(BF16)) + add_ref[c * len(units) + i]
               for i in range(len(units))]
        for i, (b, h) in enumerate(units):
            state_ref[i] = gl_ref[c * len(units) + i][0:1, :] * states[i] + res[i][:B_HEAD_DIM]
            zz = z_ref[b, rows, head_cols(0, h)]
            y = _rms(res[i][B_HEAD_DIM:], nw_ref[...]) * (zz * jax.nn.sigmoid(zz))
            o_ref[b, rows, head_cols(0, h)] = y.astype(o_ref.dtype)
        return carry

    lax.fori_loop(0, n_chunks, scan, 0)


def _gated_delta(qkv, small, z, a_log, dt_bias, norm_w, bsz, seq_len):
    par = jnp.zeros((2, LANES), F32)
    par = par.at[0, B_HEADS:2 * B_HEADS].set(a_log.astype(F32))
    par = par.at[1, B_HEADS:2 * B_HEADS].set(dt_bias.astype(F32))
    blk = lambda w: pl.BlockSpec((bsz, SCAN_ROWS, w), lambda t: (0, t, 0))
    n_units = bsz * B_HEADS
    n_slots = n_units * (SCAN_ROWS // CHUNK)
    out = pl.pallas_call(
        _gdn_kernel, grid=(seq_len // SCAN_ROWS,),
        in_specs=[blk(3 * B_W), blk(LANES), blk(B_W), _resident((2, LANES)), _resident((1, B_HEAD_DIM))],
        out_specs=blk(B_W),
        out_shape=jax.ShapeDtypeStruct((bsz, seq_len, B_W), BF16),
        scratch_shapes=[pltpu.VMEM((n_units, B_HEAD_DIM, B_HEAD_DIM), F32),
                        pltpu.VMEM((n_slots, B_HEAD_DIM + CHUNK, B_HEAD_DIM), BF16),
                        pltpu.VMEM((n_slots, B_HEAD_DIM + CHUNK, B_HEAD_DIM), F32),
                        pltpu.VMEM((n_slots, SUBLANES, LANES), F32)],
        compiler_params=_cparams("arbitrary"), name="gated_delta",
    )(qkv.reshape(bsz, seq_len, 3 * B_W), small.reshape(bsz, seq_len, LANES),
      z.reshape(bsz, seq_len, B_W), par, norm_w.reshape(1, B_HEAD_DIM).astype(F32))
    return out.reshape(bsz * seq_len, B_W)


def _ssd_kernel(xbc_ref, sm_ref, z_ref, par_ref, skip_ref, nw_ref, o_ref, state_ref):
    bsz, rows_per_step = xbc_ref.shape[0], xbc_ref.shape[1]
    group_w = C_W // C_GROUPS
    heads_per_group = C_HEADS // C_GROUPS

    @pl.when(pl.program_id(0) == 0)
    def _():
        state_ref[...] = jnp.zeros_like(state_ref)

    causal, _ = _tril_masks(CHUNK)
    tril = jnp.where(causal, 1.0, 0.0).astype(BF16)
    a_row = -jnp.exp(par_ref[0:1, :])
    dt_bias_row = par_ref[1:2, :]
    lane = lax.broadcasted_iota(jnp.int32, (CHUNK, LANES), 1)
    low_half = lane < C_HEAD_DIM

    def per_head_lanes(cols, h0):
        return jnp.where(low_half, cols[:, h0:h0 + 1], cols[:, h0 + 1:h0 + 2])

    def chunk_step(c, carry):
        rows = pl.ds(pl.multiple_of(c * CHUNK, CHUNK), CHUNK)
        for b in range(bsz):
            sm = sm_ref[b, rows, :]
            dt_all = jax.nn.softplus(sm + dt_bias_row)
            da_cs = _cumsum_rows(tril, dt_all * a_row)
            da_cs_t = da_cs.T
            dt_t = dt_all.T
            da_last = da_cs[CHUNK - 1:CHUNK, :]
            exp_da = jnp.exp(da_cs)
            to_end = jnp.exp(da_last - da_cs) * dt_all
            chunk_decay = jnp.exp(da_last)
            for g in range(C_GROUPS):
                gl = slice(g * group_w, (g + 1) * group_w)
                bm = xbc_ref[b, rows, C_W + g * C_STATE:C_W + (g + 1) * C_STATE].astype(BF16)
                cm = xbc_ref[b, rows, C_W + C_BC + g * C_STATE:C_W + C_BC + (g + 1) * C_STATE].astype(BF16)
                cb = _dot_nt(cm, bm)
                prev = state_ref[b, :, gl]
                y_off = _dot(cm, prev.astype(BF16))
                ys, xws, decs = [], [], []
                for pr in range(heads_per_group // 2):
                    h0 = g * heads_per_group + 2 * pr
                    pl_ = slice(h0 * C_HEAD_DIM, (h0 + 2) * C_HEAD_DIM)
                    x2 = xbc_ref[b, rows, pl_]
                    x2b = x2.astype(BF16)
                    yd = []
                    for h in (h0, h0 + 1):
                        seg = da_cs[:, h:h + 1] - da_cs_t[h:h + 1, :]
                        wts = cb * jnp.exp(jnp.where(causal, seg, NEG_BIG)) * dt_t[h:h + 1, :]
                        yd.append(_dot(wts.astype(BF16), x2b))
                    y_diag = jnp.where(low_half, yd[0], yd[1])
                    off = y_off[:, 2 * pr * C_HEAD_DIM:(2 * pr + 2) * C_HEAD_DIM] * per_head_lanes(exp_da, h0)
                    y = y_diag + off + skip_ref[:, pl_] * x2
                    zz = z_ref[b, rows, pl_]
                    ys.append(y * (zz * jax.nn.sigmoid(zz)))
                    xws.append((x2 * per_head_lanes(to_end, h0)).astype(BF16))
                    decs.append(jnp.where(low_half[0:1], chunk_decay[:, h0:h0 + 1], chunk_decay[:, h0 + 1:h0 + 2]))
                xw = jnp.concatenate(xws, axis=1)
                dec = jnp.concatenate(decs, axis=1)
                state_ref[b, :, gl] = prev * dec + _dot_tn(bm, xw)
                yg = jnp.concatenate(ys, axis=1)
                o_ref[b, rows, gl] = _rms(yg, nw_ref[:, gl]).astype(o_ref.dtype)
        return carry

    lax.fori_loop(0, rows_per_step // CHUNK, chunk_step, 0)


def _ssd(xbc, small, z, dt_bias, a_log, d_skip, norm_w, bsz, seq_len):
    par = jnp.zeros((2, LANES), F32)
    par = par.at[0, :C_HEADS].set(a_log.astype(F32))
    par = par.at[1, :C_HEADS].set(dt_bias.astype(F32))
    skip = jnp.repeat(d_skip.astype(F32), C_HEAD_DIM).reshape(1, C_W)
    blk = lambda w: pl.BlockSpec((bsz, SCAN_ROWS, w), lambda t: (0, t, 0))
    out = pl.pallas_call(
        _ssd_kernel, grid=(seq_len // SCAN_ROWS,),
        in_specs=[blk(C_W + 2 * C_BC), blk(LANES), blk(C_W), _resident((2, LANES)),
                  _resident((1, C_W)), _resident((1, C_W))],
        out_specs=blk(C_W),
        out_shape=jax.ShapeDtypeStruct((bsz, seq_len, C_W), BF16),
        scratch_shapes=[pltpu.VMEM((bsz, C_STATE, C_W), F32)],
        compiler_params=_cparams("arbitrary"), name="ssd",
    )(xbc.reshape(bsz, seq_len, C_W + 2 * C_BC), small.reshape(bsz, seq_len, LANES),
      z.reshape(bsz, seq_len, C_W), par, skip, norm_w.reshape(1, C_W).astype(F32))
    return out.reshape(bsz * seq_len, C_W)


FOX_F_PIECES = 3


def _fox_f_lane(h):
    return (h // 2) * LANES + (D_HEAD_DIM if h % 2 == 0 else 0)


def _fox_prep_kernel(sm_ref, fb_ref, place_ref, fk_ref, carry_ref):
    tm = sm_ref.shape[0]

    @pl.when(pl.program_id(1) == 0)
    def _():
        carry_ref[...] = jnp.zeros_like(carry_ref)

    causal, _ = _tril_masks(tm)
    tril = jnp.where(causal, 1.0, 0.0).astype(BF16)
    log_f = jax.nn.log_sigmoid(sm_ref[...] + fb_ref[...])
    f_cum = _cumsum_rows(tril, log_f) + carry_ref[...]
    carry_ref[...] = f_cum[tm - 1:tm, :]
    pieces = jnp.concatenate(_split3(-f_cum), axis=1)
    fk_ref[...] = _dot(pieces, place_ref[...]).astype(fk_ref.dtype)


def _fox_prep(small, f_bias, bsz, seq_len):
    fb = jnp.zeros((1, LANES), F32).at[0, D_HEADS:2 * D_HEADS].set(f_bias.astype(F32))
    place = np.zeros((FOX_F_PIECES * LANES, D_W), np.float32)
    for h in range(D_HEADS):
        for piece in range(FOX_F_PIECES):
            place[piece * LANES + D_HEADS + h, _fox_f_lane(h) + piece] = 1.0
    nt = seq_len // FOX_T
    return pl.pallas_call(
        _fox_prep_kernel, grid=(bsz, nt),
        in_specs=[pl.BlockSpec((None, FOX_T, LANES), lambda b, t: (b, t, 0)), _resident((1, LANES)),
                  _resident(place.shape)],
        out_specs=pl.BlockSpec((None, FOX_T, D_W), lambda b, t: (b, t, 0)),
        out_shape=jax.ShapeDtypeStruct((bsz, seq_len, D_W), BF16),
        scratch_shapes=[pltpu.VMEM((1, LANES), F32)],
        compiler_params=_cparams("parallel", "arbitrary"), name="fox_prep",
    )(small.reshape(bsz, seq_len, LANES), fb, jnp.asarray(place, BF16))


def _fox_kernel(q_ref, k_ref, v_ref, fk_ref, o_ref,
                sa_ref, sb_ref, pa_ref, pb_ref, aa_ref, ab_ref, m_ref, acc_ref):
    i = pl.program_id(2)
    t = FOX_T
    n_rb = t // FOX_ROWS
    q = q_ref[...] * jnp.asarray(D_HEAD_DIM ** -0.5, BF16)
    lane = lax.broadcasted_iota(jnp.int32, (t, LANES), 1)
    low_half = lane < D_HEAD_DIM
    zero, one = jnp.zeros_like(q), jnp.ones_like(q)
    ones_upto = lambda n: jnp.where(lane < n, 1.0, 0.0).astype(BF16)
    qs = (jnp.where(low_half, q, ones_upto(_fox_f_lane(0) + FOX_F_PIECES)),
          jnp.where(low_half, ones_upto(_fox_f_lane(1) + FOX_F_PIECES), q))
    row_in_blk = lax.broadcasted_iota(jnp.int32, (FOX_ROWS, t), 0)
    col = lax.broadcasted_iota(jnp.int32, (FOX_ROWS, t), 1)

    m_ref[...] = jnp.full(m_ref.shape, NEG_BIG, F32)
    acc_ref[...] = jnp.zeros_like(acc_ref)

    def tile_rows(j):
        return pl.ds(pl.multiple_of(jnp.maximum(j, 0) * t, t), t)

    def scores(j, s_ref):
        kj = k_ref[tile_rows(j), :]
        fj = jnp.where(j >= 0, fk_ref[tile_rows(j), :], jnp.asarray(NEG_BIG, BF16))
        ks = (jnp.where(low_half, kj, fj), jnp.where(low_half, fj, kj))
        for half in range(2):
            s_ref[half] = _dot_nt(qs[half], ks[half])

    def softmax_rows(s_ref, p_ref, a_ref, diagonal):
        blocks = [(half, rb) for half in range(2) for rb in range(n_rb)]

        def shifted(half, rb):
            s = s_ref[half, rb * FOX_ROWS:(rb + 1) * FOX_ROWS, :]
            if diagonal:
                s = jnp.where(col <= row_in_blk + rb * FOX_ROWS, s, NEG_BIG)
            return s

        row_max = [jnp.max(shifted(half, rb), axis=-1, keepdims=True) for half, rb in blocks]
        for (half, rb), mx in zip(blocks, row_max):
            rows = slice(rb * FOX_ROWS, (rb + 1) * FOX_ROWS)
            m_old = m_ref[half, rows, :]
            m_new = jnp.maximum(m_old, mx)
            m_ref[half, rows, :] = m_new
            a_ref[half, rows, :] = jnp.exp(m_old - m_new)
            p_ref[half, rows, :] = jnp.exp(shifted(half, rb) - m_new).astype(BF16)

    def accumulate(j, p_ref, a_ref):
        vj = v_ref[tile_rows(j), :]
        v_aug = (jnp.where(low_half, vj, one), jnp.where(low_half, one, vj))
        for half in range(2):
            acc_ref[half] = a_ref[half] * acc_ref[half] + _dot(p_ref[half], v_aug[half])

    scores(i, sb_ref)
    scores(i - 1, sa_ref)
    softmax_rows(sb_ref, pb_ref, ab_ref, True)

    def pair(n, carry):
        j = i - 1 - 2 * n
        scores(j - 1, sb_ref)
        softmax_rows(sa_ref, pa_ref, aa_ref, False)
        accumulate(j + 1, pb_ref, ab_ref)
        scores(j - 2, sa_ref)
        softmax_rows(sb_ref, pb_ref, ab_ref, False)
        accumulate(j, pa_ref, aa_ref)
        return carry

    n_pairs = (i + 1) // 2
    lax.fori_loop(0, n_pairs, pair, 0)
    accumulate(i - 2 * n_pairs, pb_ref, ab_ref)
    acc0, acc1 = acc_ref[0], acc_ref[1]
    out = jnp.where(low_half, acc0 / pltpu.roll(acc0, D_HEAD_DIM, axis=1), acc1 / pltpu.roll(acc1, D_HEAD_DIM, axis=1))
    o_ref[...] = out.astype(o_ref.dtype)


def _fox_attention(q, k, v, fk, bsz, seq_len):
    nt = seq_len // FOX_T
    pairs = D_W // LANES
    q3, k3, v3 = (a.reshape(bsz, seq_len, D_W) for a in (q, k, v))
    whole_seq = pl.BlockSpec((None, seq_len, LANES), lambda b, p, i: (b, 0, p))
    out = pl.pallas_call(
        _fox_kernel, grid=(bsz, pairs, nt),
        in_specs=[pl.BlockSpec((None, FOX_T, LANES), lambda b, p, i: (b, i, p)), whole_seq, whole_seq, whole_seq],
        out_specs=pl.BlockSpec((None, FOX_T, LANES), lambda b, p, i: (b, i, p)),
        out_shape=jax.ShapeDtypeStruct((bsz, seq_len, D_W), BF16),
        scratch_shapes=[pltpu.VMEM((2, FOX_T, FOX_T), F32)] * 2 + [pltpu.VMEM((2, FOX_T, FOX_T), BF16)] * 2
        + [pltpu.VMEM((2, FOX_T, 1), F32)] * 3 + [pltpu.VMEM((2, FOX_T, LANES), F32)],
        compiler_params=_cparams("parallel", "parallel", "parallel"), name="fox_attention",
    )(q3, k3, v3, fk)
    return out.reshape(bsz * seq_len, D_W)


def _pad_cols(w, width=LANES):
    return jnp.pad(w, ((0, 0), (0, width - w.shape[1])))


def kernel(x, norm_mix, norm_ffn, norm_final, ffn_w_gate, ffn_w_up, ffn_w_down, ab_w_in, ab_rel_bias, ab_conv_w, ab_a_log, ab_dt_bias, ab_norm_w, ab_w_out, cd_w_in, cd_conv_w, cd_conv_b, cd_dt_bias, cd_a_log, cd_d_skip, cd_norm_w, cd_f_bias, cd_w_out):
    bsz, seq_len, d = x.shape
    n = bsz * seq_len
    xf = x.reshape(n, d)
    bf = lambda w: w.astype(BF16)

    w_in = ab_w_in[0]
    o = np.cumsum([0, A_W, A_W, A_W, 3 * B_W, B_HEADS, B_HEADS, B_W])
    weights = [bf(w_in[:, o[0]:o[1]]), bf(w_in[:, o[1]:o[2]]), bf(w_in[:, o[2]:o[3]]),
               bf(w_in[:, o[3]:o[4]]), bf(_pad_cols(w_in[:, o[4]:o[6]])), bf(w_in[:, o[6]:o[7]])]
    a_q, a_k, a_v, b_qkv, b_small, b_z = _norm_proj(
        xf, norm_mix[0], weights, [BF16, BF16, BF16, F32, F32, F32])
    o_a = _band_attention(a_q, a_k, a_v, ab_rel_bias[0], bsz, seq_len)
    qkv = _conv_silu(b_qkv, ab_conv_w[0].astype(F32), jnp.zeros((3 * B_W,), F32), seq_len)
    o_b = _gated_delta(qkv, b_small, b_z, ab_a_log[0], ab_dt_bias[0], ab_norm_w[0], bsz, seq_len)
    w_out = ab_w_out[0]
    xf = _layer_tail(xf, o_a, o_b, bf(w_out[:A_W]), bf(w_out[A_W:]), norm_ffn[0],
                     bf(ffn_w_gate[0]), bf(ffn_w_up[0]), bf(ffn_w_down[0]), norm_final, False)

    w_in = cd_w_in[0]
    o = np.cumsum([0, C_W, C_W + 2 * C_BC, C_HEADS, D_W, D_W, D_W, D_HEADS])
    small_w = jnp.concatenate([w_in[:, o[2]:o[3]], w_in[:, o[6]:o[7]]], axis=1)
    weights = [bf(w_in[:, o[0]:o[1]]), bf(w_in[:, o[1]:o[2]]), bf(_pad_cols(small_w)),
               bf(w_in[:, o[3]:o[4]]), bf(w_in[:, o[4]:o[5]]), bf(w_in[:, o[5]:o[6]])]
    c_z, c_xbc, cd_small, d_q, d_k, d_v = _norm_proj(
        xf, norm_mix[1], weights, [F32, F32, F32, BF16, BF16, BF16])
    xbc = _conv_silu(c_xbc, cd_conv_w[0].astype(F32), cd_conv_b[0].astype(F32), seq_len)
    y_c = _ssd(xbc, cd_small, c_z, cd_dt_bias[0], cd_a_log[0], cd_d_skip[0], cd_norm_w[0], bsz, seq_len)
    fk = _fox_prep(cd_small, cd_f_bias[0], bsz, seq_len)
    o_d = _fox_attention(d_q, d_k, d_v, fk, bsz, seq_len)
    w_out = cd_w_out[0]
    xf = _layer_tail(xf, y_c, o_d, bf(w_out[:C_W]), bf(w_out[C_W:]), norm_ffn[1],
                     bf(ffn_w_gate[1]), bf(ffn_w_up[1]), bf(ffn_w_down[1]), norm_final, True)
    return xf.reshape(bsz, seq_len, d)
```

```python
import functools

import jax
import jax.numpy as jnp
import numpy as np
from jax import lax
from jax.experimental import pallas as pl
from jax.experimental.pallas import tpu as pltpu

F32 = jnp.float32
BF16 = jnp.bfloat16

D_MODEL = 1024
CHUNK = 64
EPS = 1e-6
CONV_K = 4
A_HEADS, A_HEAD_DIM, A_LEFT_CHUNKS, A_MAX_REL = 8, 64, 8, 256
B_HEADS, B_HEAD_DIM = 4, 128
C_HEADS, C_HEAD_DIM, C_GROUPS, C_STATE = 8, 64, 2, 128
D_HEADS, D_HEAD_DIM = 8, 64
A_W = A_HEADS * A_HEAD_DIM
B_W = B_HEADS * B_HEAD_DIM
C_W = C_HEADS * C_HEAD_DIM
D_W = D_HEADS * D_HEAD_DIM
C_BC = C_GROUPS * C_STATE

LANES = 128
SUBLANES = 8
VMEM_LIMIT_BYTES = 56 * 1024 * 1024
NEG_BIG = -1e30

ROW_TILE = 512
FFN_CHUNK = 256
BAND_TQ = 128
BAND_WIN = BAND_TQ + A_LEFT_CHUNKS * CHUNK
FOX_T = 512
FOX_ROWS = 32
SCAN_ROWS = 512
SSD_CHUNKS = 2
GDN_BUILD_CHUNKS = 2


def _cparams(*sem):
    return pltpu.CompilerParams(dimension_semantics=sem, vmem_limit_bytes=VMEM_LIMIT_BYTES)


def _resident(shape):
    nd = len(shape)
    return pl.BlockSpec(shape, lambda *_: (0,) * nd, pipeline_mode=pl.Buffered(1))


def _dot(a, b):
    return jnp.dot(a, b, preferred_element_type=F32)


def _dot_nt(a, b):
    return lax.dot_general(a, b, (((1,), (1,)), ((), ())), preferred_element_type=F32)


def _dot_tn(a, b):
    return lax.dot_general(a, b, (((0,), (0,)), ((), ())), preferred_element_type=F32)


def _rms(x, w):
    return x * lax.rsqrt(jnp.mean(x * x, axis=-1, keepdims=True) + EPS) * w


def _split3(x):
    hi = x.astype(BF16)
    r1 = x - hi.astype(F32)
    mid = r1.astype(BF16)
    lo = (r1 - mid.astype(F32)).astype(BF16)
    return hi, mid, lo


def _cumsum_rows(tril, x):
    hi, mid, lo = _split3(x)
    return _dot(tril, hi) + _dot(tril, mid) + _dot(tril, lo)


def _tril_masks(n):
    r = lax.broadcasted_iota(jnp.int32, (n, n), 0)
    c = lax.broadcasted_iota(jnp.int32, (n, n), 1)
    return r >= c, r > c


def _norm_proj_kernel(x_ref, nw_ref, *refs):
    n_out = len(refs) // 2
    h = _rms(x_ref[...], nw_ref[...]).astype(BF16)
    for w_ref, o_ref in zip(refs[:n_out], refs[n_out:]):
        o_ref[...] = _dot(h, w_ref[...]).astype(o_ref.dtype)


def _norm_proj(x, norm_w, weights, out_dtypes):
    n, d = x.shape
    in_specs = [pl.BlockSpec((ROW_TILE, d), lambda i: (i, 0)), _resident((1, d))]
    in_specs += [_resident(w.shape) for w in weights]
    out_shape = [jax.ShapeDtypeStruct((n, w.shape[1]), dt) for w, dt in zip(weights, out_dtypes)]
    out_specs = [pl.BlockSpec((ROW_TILE, w.shape[1]), lambda i: (i, 0)) for w in weights]
    return pl.pallas_call(
        _norm_proj_kernel, grid=(n // ROW_TILE,), in_specs=in_specs, out_specs=out_specs,
        out_shape=out_shape, compiler_params=_cparams("parallel"), name="norm_proj",
    )(x, norm_w.reshape(1, d), *weights)


def _tail_kernel(x_ref, a_ref, b_ref, woa_ref, wob_ref, nw_ref, wg_ref, wu_ref, wd_ref, fn_ref,
                 o_ref, acc_ref, *, d_ff, final_norm):
    x1 = x_ref[...] + _dot(a_ref[...], woa_ref[...]) + _dot(b_ref[...], wob_ref[...])
    h = _rms(x1, nw_ref[...]).astype(BF16)
    acc_ref[...] = x1
    ffn = None
    for c in range(d_ff // FFN_CHUNK):
        cols = slice(c * FFN_CHUNK, (c + 1) * FFN_CHUNK)
        g = _dot(h, wg_ref[:, cols])
        u = _dot(h, wu_ref[:, cols])
        part = _dot((g * jax.nn.sigmoid(g) * u).astype(BF16), wd_ref[cols, :])
        ffn = part if ffn is None else ffn + part
    y = acc_ref[...] + ffn
    if final_norm:
        y = _rms(y, fn_ref[...])
    o_ref[...] = y


def _layer_tail(x, mix_a, mix_b, wo_a, wo_b, norm_w, wg, wu, wd, final_w, final_norm):
    n, d = x.shape
    d_ff = wg.shape[1]
    row = lambda w: pl.BlockSpec((ROW_TILE, w), lambda i: (i, 0))
    in_specs = [row(d), row(mix_a.shape[1]), row(mix_b.shape[1]), _resident(wo_a.shape),
                _resident(wo_b.shape), _resident((1, d)), _resident(wg.shape), _resident(wu.shape),
                _resident(wd.shape), _resident((1, d))]
    return pl.pallas_call(
        functools.partial(_tail_kernel, d_ff=d_ff, final_norm=final_norm),
        grid=(n // ROW_TILE,), in_specs=in_specs, out_specs=row(d),
        out_shape=jax.ShapeDtypeStruct((n, d), F32),
        scratch_shapes=[pltpu.VMEM((ROW_TILE, d), F32)],
        compiler_params=_cparams("parallel"), name="layer_tail",
    )(x, mix_a, mix_b, wo_a, wo_b, norm_w.reshape(1, d), wg, wu, wd, final_w.reshape(1, d))


def _band_kernel(q_ref, *refs):
    n_blk = BAND_WIN // BAND_TQ
    k_refs, v_refs = refs[:n_blk], refs[n_blk:2 * n_blk]
    bias_ref, o_ref, s_ref, p_ref = refs[2 * n_blk:]
    i = pl.program_id(1)
    lane = lax.broadcasted_iota(jnp.int32, (BAND_TQ, LANES), 1)
    low_half = lane < A_HEAD_DIM
    win_lane = lax.broadcasted_iota(jnp.int32, (BAND_WIN, LANES), 1)
    key_blk = lax.broadcasted_iota(jnp.int32, (1, BAND_WIN), 1) // BAND_TQ
    before_start = jnp.where(key_blk >= (n_blk - 1) - i, 0.0, NEG_BIG)
    pairs = A_W // LANES
    n_rb = BAND_TQ // FOX_ROWS

    v_aug = []
    for pr in range(pairs):
        cols = slice(pr * LANES, (pr + 1) * LANES)
        k = jnp.concatenate([r[:, cols] for r in k_refs], axis=0)
        v = jnp.concatenate([r[:, cols] for r in v_refs], axis=0)
        q = q_ref[:, cols] * jnp.asarray(A_HEAD_DIM ** -0.5, BF16)
        zero, one = jnp.zeros_like(q), jnp.ones_like(v)
        s_ref[2 * pr] = _dot_nt(jnp.where(low_half, q, zero), k) + (bias_ref[2 * pr] + before_start)
        s_ref[2 * pr + 1] = _dot_nt(jnp.where(low_half, zero, q), k) + (bias_ref[2 * pr + 1] + before_start)
        v_aug += [jnp.where(win_lane < A_HEAD_DIM, v, one), jnp.where(win_lane < A_HEAD_DIM, one, v)]

    blocks = [(h, rb) for h in range(A_HEADS) for rb in range(n_rb)]
    rows = lambda rb: slice(rb * FOX_ROWS, (rb + 1) * FOX_ROWS)
    row_max = [jnp.max(s_ref[h, rows(rb), :], axis=-1, keepdims=True) for h, rb in blocks]
    for (h, rb), mx in zip(blocks, row_max):
        p_ref[h, rows(rb), :] = jnp.exp(s_ref[h, rows(rb), :] - mx).astype(BF16)

    for pr in range(pairs):
        o_even = _dot(p_ref[2 * pr], v_aug[2 * pr])
        o_odd = _dot(p_ref[2 * pr + 1], v_aug[2 * pr + 1])
        out = jnp.where(low_half, o_even / pltpu.roll(o_even, A_HEAD_DIM, axis=1),
                        o_odd / pltpu.roll(o_odd, A_HEAD_DIM, axis=1))
        o_ref[:, pr * LANES:(pr + 1) * LANES] = out.astype(o_ref.dtype)


def _band_bias_tiles(rel_bias):
    r = np.arange(BAND_TQ)[:, None]
    c = np.arange(BAND_WIN)[None, :]
    offs = np.arange(-(BAND_TQ - 1), BAND_WIN)
    idx = np.clip(A_LEFT_CHUNKS * CHUNK - offs, -A_MAX_REL, A_MAX_REL) + A_MAX_REL
    period = offs.size + 1
    per_off = jnp.pad(rel_bias.astype(F32)[:, idx], ((0, 0), (0, 1)))
    skew = jnp.tile(per_off, (1, BAND_TQ))[:, :BAND_TQ * (period - 1)]
    skew = skew.reshape(-1, BAND_TQ, period - 1)[:, :, BAND_TQ - 1:BAND_TQ - 1 + BAND_WIN]
    qc, kc = r // CHUNK, c // CHUNK
    allowed = (kc >= qc) & (kc <= qc + A_LEFT_CHUNKS)
    return jnp.where(allowed[None], skew, NEG_BIG)


def _band_attention(q, k, v, rel_bias, bsz, seq_len):
    n_blk = BAND_WIN // BAND_TQ
    nq = seq_len // BAND_TQ
    q3, k3, v3 = (a.reshape(bsz, seq_len, A_W) for a in (q, k, v))
    blk = (None, BAND_TQ, A_W)
    kv_specs = [pl.BlockSpec(blk, lambda b, i, j=j: (b, jnp.maximum(i - (n_blk - 1) + j, 0), 0))
                for j in range(n_blk)]
    out = pl.pallas_call(
        _band_kernel, grid=(bsz, nq),
        in_specs=[pl.BlockSpec(blk, lambda b, i: (b, i, 0))] + kv_specs + kv_specs
        + [_resident((A_HEADS, BAND_TQ, BAND_WIN))],
        out_specs=pl.BlockSpec(blk, lambda b, i: (b, i, 0)),
        out_shape=jax.ShapeDtypeStruct((bsz, seq_len, A_W), BF16),
        scratch_shapes=[pltpu.VMEM((A_HEADS, BAND_TQ, BAND_WIN), F32),
                        pltpu.VMEM((A_HEADS, BAND_TQ, BAND_WIN), BF16)],
        compiler_params=_cparams("parallel", "parallel"), name="band_attention",
    )(q3, *([k3] * n_blk), *([v3] * n_blk), _band_bias_tiles(rel_bias))
    return out.reshape(bsz * seq_len, A_W)


def _stage_conv_input(x_ref, pad_ref):
    rows = x_ref.shape[1]

    @pl.when(pl.program_id(0) == 0)
    def _():
        pad_ref[:, 0:SUBLANES, :] = jnp.zeros((pad_ref.shape[0], SUBLANES, pad_ref.shape[2]), F32)

    @pl.when(pl.program_id(0) > 0)
    def _():
        pad_ref[:, 0:SUBLANES, :] = pad_ref[:, rows:rows + SUBLANES, :]

    pad_ref[:, SUBLANES:, :] = x_ref[...]


def _conv_silu_rows(pad_ref, w_ref, b, row0, cols, bias_ref=None):
    window = pad_ref[b, pl.ds(row0, SUBLANES + CHUNK), cols]
    acc = None
    for back in range(CONV_K):
        term = w_ref[CONV_K - 1 - back:CONV_K - back, cols] * window[SUBLANES - back:SUBLANES - back + CHUNK]
        acc = term if acc is None else acc + term
    if bias_ref is not None:
        acc = acc + bias_ref[:, cols]
    return acc * jax.nn.sigmoid(acc)


def _gdn_kernel(qkv_ref, sm_ref, z_ref, cw_ref, par_ref, nw_ref, o_ref,
                pad_ref, state_ref, lhs_ref, add_ref, gl_ref):
    bsz, rows_per_step = qkv_ref.shape[0], qkv_ref.shape[1]
    n_chunks = rows_per_step // CHUNK
    heads = [(b, h) for b in range(bsz) for h in range(B_HEADS)]
    units = [(cc, b, h) for cc in range(GDN_BUILD_CHUNKS) for b, h in heads]

    @pl.when(pl.program_id(0) == 0)
    def _():
        state_ref[...] = jnp.zeros_like(state_ref)

    _stage_conv_input(qkv_ref, pad_ref)
    causal, strict = _tril_masks(CHUNK)
    tril = jnp.where(causal, 1.0, 0.0).astype(BF16)
    eye = jnp.where(causal & ~strict, 1.0, 0.0)
    a_row = -jnp.exp(par_ref[0:1, :])
    dt_bias_row = par_ref[1:2, :]

    def head_cols(part, h):
        return slice(part * B_W + h * B_HEAD_DIM, part * B_W + (h + 1) * B_HEAD_DIM)

    def build(cp, carry):
        row0 = [pl.multiple_of((cp * GDN_BUILD_CHUNKS + cc) * CHUNK, CHUNK) for cc in range(GDN_BUILD_CHUNKS)]
        sm = {(cc, b): sm_ref[b, pl.ds(row0[cc], CHUNK), :] for cc in range(GDN_BUILD_CHUNKS) for b in range(bsz)}
        beta_all = {key: jax.nn.sigmoid(x) for key, x in sm.items()}
        gc_all = {key: _cumsum_rows(tril, a_row * jax.nn.softplus(x + dt_bias_row)) for key, x in sm.items()}
        gc_all_t = {key: x.T for key, x in gc_all.items()}
        q, k, v = ([_conv_silu_rows(pad_ref, cw_ref, b, row0[cc], head_cols(part, h)) for cc, b, h in units]
                   for part in range(3))
        q = [x * lax.rsqrt(jnp.sum(x * x, axis=-1, keepdims=True) + EPS) * (B_HEAD_DIM ** -0.5) for x in q]
        k = [x * lax.rsqrt(jnp.sum(x * x, axis=-1, keepdims=True) + EPS) for x in k]
        beta = [beta_all[cc, b][:, h:h + 1] for cc, b, h in units]
        gc = [gc_all[cc, b][:, B_HEADS + h:B_HEADS + h + 1] for cc, b, h in units]
        gc_row = [gc_all_t[cc, b][B_HEADS + h:B_HEADS + h + 1, :] for cc, b, h in units]
        decay = [jnp.exp(jnp.where(causal, g - gr, NEG_BIG)) for g, gr in zip(gc, gc_row)]
        kb = [x.astype(BF16) for x in k]
        kk = [_dot_nt(x, x) for x in kb]
        qk = [_dot_nt(x.astype(BF16), y) for x, y in zip(q, kb)]
        a_mat = [jnp.where(strict, bt * x * d, 0.0) for bt, x, d in zip(beta, kk, decay)]
        inv = [eye - a for a in a_mat]
        power = a_mat
        for _ in range(5):
            power = [_dot(x, x) for x in power]
            inv = [i + _dot(i, x) for i, x in zip(inv, power)]
        exp_gc = [jnp.exp(g) for g in gc]
        rhs = [jnp.concatenate([ki * (bt * e), vi * bt], axis=1) for ki, vi, bt, e in zip(k, v, beta, exp_gc)]
        wu = [_dot(i, r).astype(BF16) for i, r in zip(inv, rhs)]
        attn = [(x * d).astype(BF16) for x, d in zip(qk, decay)]
        gc_last = [g[CHUNK - 1:CHUNK, :] for g in gc]
        k_st = [(ki * jnp.exp(gl - g)).astype(BF16) for ki, gl, g in zip(k, gc_last, gc)]
        top = [_dot_tn(x, y) for x, y in zip(k_st, wu)]
        bot = [_dot(x, y) for x, y in zip(attn, wu)]
        for i in range(len(units)):
            slot = cp * len(units) + i
            lhs_ref[slot, 0:B_HEAD_DIM, :] = (-top[i][:, :B_HEAD_DIM]).astype(BF16)
            lhs_ref[slot, B_HEAD_DIM:, :] = (q[i] * exp_gc[i] - bot[i][:, :B_HEAD_DIM]).astype(BF16)
            add_ref[slot, 0:B_HEAD_DIM, :] = top[i][:, B_HEAD_DIM:]
            add_ref[slot, B_HEAD_DIM:, :] = bot[i][:, B_HEAD_DIM:]
            gl_ref[slot] = jnp.broadcast_to(jnp.exp(gc_last[i]), (SUBLANES, LANES))
        return carry

    lax.fori_loop(0, n_chunks // GDN_BUILD_CHUNKS, build, 0)

    def scan(c, carry):
        rows = pl.ds(pl.multiple_of(c * CHUNK, CHUNK), CHUNK)
        states = [state_ref[i] for i in range(len(heads))]
        res = [_dot(lhs_ref[c * len(heads) + i], states[i].astype(BF16)) + add_ref[c * len(heads) + i]
               for i in range(len(heads))]
        for i, (b, h) in enumerate(heads):
            state_ref[i] = gl_ref[c * len(heads) + i][0:1, :] * states[i] + res[i][:B_HEAD_DIM]
            zz = z_ref[b, rows, head_cols(0, h)]
            y = _rms(res[i][B_HEAD_DIM:], nw_ref[...]) * (zz * jax.nn.sigmoid(zz))
            o_ref[b, rows, head_cols(0, h)] = y.astype(o_ref.dtype)
        return carry

    lax.fori_loop(0, n_chunks, scan, 0)


def _gated_delta(qkv_raw, small, z, conv_w, a_log, dt_bias, norm_w, bsz, seq_len):
    par = jnp.zeros((2, LANES), F32)
    par = par.at[0, B_HEADS:2 * B_HEADS].set(a_log.astype(F32))
    par = par.at[1, B_HEADS:2 * B_HEADS].set(dt_bias.astype(F32))
    blk = lambda w: pl.BlockSpec((bsz, SCAN_ROWS, w), lambda t: (0, t, 0))
    n_units = bsz * B_HEADS
    n_slots = n_units * (SCAN_ROWS // CHUNK)
    out = pl.pallas_call(
        _gdn_kernel, grid=(seq_len // SCAN_ROWS,),
        in_specs=[blk(3 * B_W), blk(LANES), blk(B_W), _resident((CONV_K, 3 * B_W)), _resident((2, LANES)),
                  _resident((1, B_HEAD_DIM))],
        out_specs=blk(B_W),
        out_shape=jax.ShapeDtypeStruct((bsz, seq_len, B_W), BF16),
        scratch_shapes=[pltpu.VMEM((bsz, SUBLANES + SCAN_ROWS, 3 * B_W), F32),
                        pltpu.VMEM((n_units, B_HEAD_DIM, B_HEAD_DIM), F32),
                        pltpu.VMEM((n_slots, B_HEAD_DIM + CHUNK, B_HEAD_DIM), BF16),
                        pltpu.VMEM((n_slots, B_HEAD_DIM + CHUNK, B_HEAD_DIM), F32),
                        pltpu.VMEM((n_slots, SUBLANES, LANES), F32)],
        compiler_params=_cparams("arbitrary"), name="gated_delta",
    )(qkv_raw.reshape(bsz, seq_len, 3 * B_W), small.reshape(bsz, seq_len, LANES),
      z.reshape(bsz, seq_len, B_W), conv_w.astype(F32), par, norm_w.reshape(1, B_HEAD_DIM).astype(F32))
    return out.reshape(bsz * seq_len, B_W)


def _ssd_kernel(xbc_ref, sm_ref, z_ref, cw_ref, cb_ref, par_ref, skip_ref, nw_ref, o_ref, pad_ref, state_ref):
    bsz, rows_per_step = xbc_ref.shape[0], xbc_ref.shape[1]
    group_w = C_W // C_GROUPS
    heads_per_group = C_HEADS // C_GROUPS
    pairs_per_group = heads_per_group // 2

    @pl.when(pl.program_id(0) == 0)
    def _():
        state_ref[...] = jnp.zeros_like(state_ref)

    _stage_conv_input(xbc_ref, pad_ref)
    causal, _ = _tril_masks(CHUNK)
    tril = jnp.where(causal, 1.0, 0.0).astype(BF16)
    a_row = -jnp.exp(par_ref[0:1, :])
    dt_bias_row = par_ref[1:2, :]
    lane = lax.broadcasted_iota(jnp.int32, (CHUNK, LANES), 1)
    low_half = lane < C_HEAD_DIM

    def per_head_lanes(cols, h0):
        return jnp.where(low_half, cols[:, h0:h0 + 1], cols[:, h0 + 1:h0 + 2])

    def pair_cols(pr):
        return slice(pr * LANES, (pr + 1) * LANES)

    def chunk_pair(cp, carry):
        units = [(cc, b) for cc in range(SSD_CHUNKS) for b in range(bsz)]
        groups = [(u, g) for u in units for g in range(C_GROUPS)]
        row0 = [pl.multiple_of((cp * SSD_CHUNKS + cc) * CHUNK, CHUNK) for cc in range(SSD_CHUNKS)]
        conv = lambda u, cols: _conv_silu_rows(pad_ref, cw_ref, u[1], row0[u[0]], cols, cb_ref)
        sm = {u: sm_ref[u[1], pl.ds(row0[u[0]], CHUNK), :] for u in units}
        dt_all = {u: jax.nn.softplus(sm[u] + dt_bias_row) for u in units}
        da_cs = {u: _cumsum_rows(tril, dt_all[u] * a_row) for u in units}
        da_cs_t = {u: da_cs[u].T for u in units}
        dt_t = {u: dt_all[u].T for u in units}
        da_last = {u: da_cs[u][CHUNK - 1:CHUNK, :] for u in units}
        exp_da = {u: jnp.exp(da_cs[u]) for u in units}
        to_end = {u: jnp.exp(da_last[u] - da_cs[u]) * dt_all[u] for u in units}
        chunk_decay = {u: jnp.exp(da_last[u]) for u in units}
        x2 = {(u, pr): conv(u, pair_cols(pr)) for u in units for pr in range(C_HEADS // 2)}
        bm = {(u, g): conv(u, slice(C_W + g * C_STATE, C_W + (g + 1) * C_STATE)).astype(BF16) for u, g in groups}
        cm = {(u, g): conv(u, slice(C_W + C_BC + g * C_STATE, C_W + C_BC + (g + 1) * C_STATE)).astype(BF16)
              for u, g in groups}
        cb = {key: _dot_nt(cm[key], bm[key]) for key in groups}
        xw = {(u, g): jnp.concatenate(
            [(x2[u, g * pairs_per_group + j] * per_head_lanes(to_end[u], g * heads_per_group + 2 * j)).astype(BF16)
             for j in range(pairs_per_group)], axis=1) for u, g in groups}
        inflow = {key: _dot_tn(bm[key], xw[key]) for key in groups}
        y_diag = {}
        for u in units:
            for h in range(C_HEADS):
                seg = da_cs[u][:, h:h + 1] - da_cs_t[u][h:h + 1, :]
                wts = cb[u, h // heads_per_group] * jnp.exp(jnp.where(causal, seg, NEG_BIG)) * dt_t[u][h:h + 1, :]
                y_diag[u, h] = _dot(wts.astype(BF16), x2[u, h // 2].astype(BF16))
        for u in units:
            cc, b = u
            rows = pl.ds(row0[cc], CHUNK)
            for g in range(C_GROUPS):
                gl = slice(g * group_w, (g + 1) * group_w)
                prev = state_ref[b, :, gl]
                y_off = _dot(cm[u, g], prev.astype(BF16))
                dec = jnp.concatenate(
                    [jnp.where(low_half[0:1], chunk_decay[u][:, h0:h0 + 1], chunk_decay[u][:, h0 + 1:h0 + 2])
                     for h0 in range(g * heads_per_group, (g + 1) * heads_per_group, 2)], axis=1)
                state_ref[b, :, gl] = prev * dec + inflow[u, g]
                ys = []
                for j in range(pairs_per_group):
                    pr = g * pairs_per_group + j
                    h0 = 2 * pr
                    y = jnp.where(low_half, y_diag[u, h0], y_diag[u, h0 + 1])
                    y = y + y_off[:, pair_cols(j)] * per_head_lanes(exp_da[u], h0) + skip_ref[:, pair_cols(pr)] * x2[u, pr]
                    zz = z_ref[b, rows, pair_cols(pr)]
                    ys.append(y * (zz * jax.nn.sigmoid(zz)))
                yg = jnp.concatenate(ys, axis=1)
                o_ref[b, rows, gl] = _rms(yg, nw_ref[:, gl]).astype(o_ref.dtype)
        return carry

    lax.fori_loop(0, rows_per_step // (CHUNK * SSD_CHUNKS), chunk_pair, 0)


def _ssd(xbc_raw, small, z, conv_w, conv_b, dt_bias, a_log, d_skip, norm_w, bsz, seq_len):
    par = jnp.zeros((2, LANES), F32)
    par = par.at[0, :C_HEADS].set(a_log.astype(F32))
    par = par.at[1, :C_HEADS].set(dt_bias.astype(F32))
    skip = jnp.repeat(d_skip.astype(F32), C_HEAD_DIM).reshape(1, C_W)
    width = C_W + 2 * C_BC
    blk = lambda w: pl.BlockSpec((bsz, SCAN_ROWS, w), lambda t: (0, t, 0))
    out = pl.pallas_call(
        _ssd_kernel, grid=(seq_len // SCAN_ROWS,),
        in_specs=[blk(width), blk(LANES), blk(C_W), _resident((CONV_K, width)), _resident((1, width)),
                  _resident((2, LANES)), _resident((1, C_W)), _resident((1, C_W))],
        out_specs=blk(C_W),
        out_shape=jax.ShapeDtypeStruct((bsz, seq_len, C_W), BF16),
        scratch_shapes=[pltpu.VMEM((bsz, SUBLANES + SCAN_ROWS, width), F32),
                        pltpu.VMEM((bsz, C_STATE, C_W), F32)],
        compiler_params=_cparams("arbitrary"), name="ssd",
    )(xbc_raw.reshape(bsz, seq_len, width), small.reshape(bsz, seq_len, LANES), z.reshape(bsz, seq_len, C_W),
      conv_w.astype(F32), conv_b.astype(F32).reshape(1, width), par, skip, norm_w.reshape(1, C_W).astype(F32))
    return out.reshape(bsz * seq_len, C_W)


FOX_F_PIECES = 3


def _fox_f_lane(h):
    return (h // 2) * LANES + (D_HEAD_DIM if h % 2 == 0 else 0)


def _fox_prep_kernel(sm_ref, fb_ref, place_ref, fk_ref, carry_ref):
    tm = sm_ref.shape[0]

    @pl.when(pl.program_id(1) == 0)
    def _():
        carry_ref[...] = jnp.zeros_like(carry_ref)

    causal, _ = _tril_masks(tm)
    tril = jnp.where(causal, 1.0, 0.0).astype(BF16)
    log_f = jax.nn.log_sigmoid(sm_ref[...] + fb_ref[...])
    f_cum = _cumsum_rows(tril, log_f) + carry_ref[...]
    carry_ref[...] = f_cum[tm - 1:tm, :]
    pieces = jnp.concatenate(_split3(-f_cum), axis=1)
    fk_ref[...] = _dot(pieces, place_ref[...]).astype(fk_ref.dtype)


def _fox_prep(small, f_bias, bsz, seq_len):
    fb = jnp.zeros((1, LANES), F32).at[0, D_HEADS:2 * D_HEADS].set(f_bias.astype(F32))
    place = np.zeros((FOX_F_PIECES * LANES, D_W), np.float32)
    for h in range(D_HEADS):
        for piece in range(FOX_F_PIECES):
            place[piece * LANES + D_HEADS + h, _fox_f_lane(h) + piece] = 1.0
    nt = seq_len // FOX_T
    return pl.pallas_call(
        _fox_prep_kernel, grid=(bsz, nt),
        in_specs=[pl.BlockSpec((None, FOX_T, LANES), lambda b, t: (b, t, 0)), _resident((1, LANES)),
                  _resident(place.shape)],
        out_specs=pl.BlockSpec((None, FOX_T, D_W), lambda b, t: (b, t, 0)),
        out_shape=jax.ShapeDtypeStruct((bsz, seq_len, D_W), BF16),
        scratch_shapes=[pltpu.VMEM((1, LANES), F32)],
        compiler_params=_cparams("parallel", "arbitrary"), name="fox_prep",
    )(small.reshape(bsz, seq_len, LANES), fb, jnp.asarray(place, BF16))


def _fox_kernel(q_ref, k_ref, v_ref, fk_ref, o_ref,
                sa_ref, sb_ref, pa_ref, pb_ref, aa_ref, ab_ref, m_ref, acc_ref):
    i = pl.program_id(2)
    t = FOX_T
    n_rb = t // FOX_ROWS
    q = q_ref[...] * jnp.asarray(D_HEAD_DIM ** -0.5, BF16)
    lane = lax.broadcasted_iota(jnp.int32, (t, LANES), 1)
    low_half = lane < D_HEAD_DIM
    zero, one = jnp.zeros_like(q), jnp.ones_like(q)
    ones_upto = lambda n: jnp.where(lane < n, 1.0, 0.0).astype(BF16)
    qs = (jnp.where(low_half, q, ones_upto(_fox_f_lane(0) + FOX_F_PIECES)),
          jnp.where(low_half, ones_upto(_fox_f_lane(1) + FOX_F_PIECES), q))
    row_in_blk = lax.broadcasted_iota(jnp.int32, (FOX_ROWS, t), 0)
    col = lax.broadcasted_iota(jnp.int32, (FOX_ROWS, t), 1)

    m_ref[...] = jnp.full(m_ref.shape, NEG_BIG, F32)
    acc_ref[...] = jnp.zeros_like(acc_ref)

    def tile_rows(j):
        return pl.ds(pl.multiple_of(jnp.maximum(j, 0) * t, t), t)

    def scores(j, s_ref):
        kj = k_ref[tile_rows(j), :]
        fj = jnp.where(j >= 0, fk_ref[tile_rows(j), :], jnp.asarray(NEG_BIG, BF16))
        ks = (jnp.where(low_half, kj, fj), jnp.where(low_half, fj, kj))
        for half in range(2):
            s_ref[half] = _dot_nt(qs[half], ks[half])

    def softmax_rows(s_ref, p_ref, a_ref, diagonal):
        blocks = [(half, rb) for half in range(2) for rb in range(n_rb)]

        def shifted(half, rb):
            s = s_ref[half, rb * FOX_ROWS:(rb + 1) * FOX_ROWS, :]
            if diagonal:
                s = jnp.where(col <= row_in_blk + rb * FOX_ROWS, s, NEG_BIG)
            return s

        row_max = [jnp.max(shifted(half, rb), axis=-1, keepdims=True) for half, rb in blocks]
        for (half, rb), mx in zip(blocks, row_max):
            rows = slice(rb * FOX_ROWS, (rb + 1) * FOX_ROWS)
            m_old = m_ref[half, rows, :]
            m_new = jnp.maximum(m_old, mx)
            m_ref[half, rows, :] = m_new
            a_ref[half, rows, :] = jnp.exp(m_old - m_new)
            p_ref[half, rows, :] = jnp.exp(shifted(half, rb) - m_new).astype(BF16)

    def accumulate(j, p_ref, a_ref):
        vj = v_ref[tile_rows(j), :]
        v_aug = (jnp.where(low_half, vj, one), jnp.where(low_half, one, vj))
        for half in range(2):
            acc_ref[half] = a_ref[half] * acc_ref[half] + _dot(p_ref[half], v_aug[half])

    scores(i, sb_ref)
    scores(i - 1, sa_ref)
    softmax_rows(sb_ref, pb_ref, ab_ref, True)

    def pair(n, carry):
        j = i - 1 - 2 * n
        scores(j - 1, sb_ref)
        softmax_rows(sa_ref, pa_ref, aa_ref, False)
        accumulate(j + 1, pb_ref, ab_ref)
        scores(j - 2, sa_ref)
        softmax_rows(sb_ref, pb_ref, ab_ref, False)
        accumulate(j, pa_ref, aa_ref)
        return carry

    n_pairs = (i + 1) // 2
    lax.fori_loop(0, n_pairs, pair, 0)
    accumulate(i - 2 * n_pairs, pb_ref, ab_ref)
    acc0, acc1 = acc_ref[0], acc_ref[1]
    out = jnp.where(low_half, acc0 / pltpu.roll(acc0, D_HEAD_DIM, axis=1), acc1 / pltpu.roll(acc1, D_HEAD_DIM, axis=1))
    o_ref[...] = out.astype(o_ref.dtype)


def _fox_attention(q, k, v, fk, bsz, seq_len):
    nt = seq_len // FOX_T
    pairs = D_W // LANES
    q3, k3, v3 = (a.reshape(bsz, seq_len, D_W) for a in (q, k, v))
    whole_seq = pl.BlockSpec((None, seq_len, LANES), lambda b, p, i: (b, 0, p))
    out = pl.pallas_call(
        _fox_kernel, grid=(bsz, pairs, nt),
        in_specs=[pl.BlockSpec((None, FOX_T, LANES), lambda b, p, i: (b, i, p)), whole_seq, whole_seq, whole_seq],
        out_specs=pl.BlockSpec((None, FOX_T, LANES), lambda b, p, i: (b, i, p)),
        out_shape=jax.ShapeDtypeStruct((bsz, seq_len, D_W), BF16),
        scratch_shapes=[pltpu.VMEM((2, FOX_T, FOX_T), F32)] * 2 + [pltpu.VMEM((2, FOX_T, FOX_T), BF16)] * 2
        + [pltpu.VMEM((2, FOX_T, 1), F32)] * 3 + [pltpu.VMEM((2, FOX_T, LANES), F32)],
        compiler_params=_cparams("parallel", "parallel", "parallel"), name="fox_attention",
    )(q3, k3, v3, fk)
    return out.reshape(bsz * seq_len, D_W)


def _pad_cols(w, width=LANES):
    return jnp.pad(w, ((0, 0), (0, width - w.shape[1])))


def kernel(x, norm_mix, norm_ffn, norm_final, ffn_w_gate, ffn_w_up, ffn_w_down, ab_w_in, ab_rel_bias, ab_conv_w, ab_a_log, ab_dt_bias, ab_norm_w, ab_w_out, cd_w_in, cd_conv_w, cd_conv_b, cd_dt_bias, cd_a_log, cd_d_skip, cd_norm_w, cd_f_bias, cd_w_out):
    bsz, seq_len, d = x.shape
    n = bsz * seq_len
    xf = x.reshape(n, d)
    bf = lambda w: w.astype(BF16)

    w_in = ab_w_in[0]
    o = np.cumsum([0, A_W, A_W, A_W, 3 * B_W, B_HEADS, B_HEADS, B_W])
    weights = [bf(w_in[:, o[0]:o[1]]), bf(w_in[:, o[1]:o[2]]), bf(w_in[:, o[2]:o[3]]),
               bf(w_in[:, o[3]:o[4]]), bf(_pad_cols(w_in[:, o[4]:o[6]])), bf(w_in[:, o[6]:o[7]])]
    a_q, a_k, a_v, b_qkv, b_small, b_z = _norm_proj(
        xf, norm_mix[0], weights, [BF16, BF16, BF16, F32, F32, F32])
    o_a = _band_attention(a_q, a_k, a_v, ab_rel_bias[0], bsz, seq_len)
    o_b = _gated_delta(b_qkv, b_small, b_z, ab_conv_w[0], ab_a_log[0], ab_dt_bias[0], ab_norm_w[0], bsz, seq_len)
    w_out = ab_w_out[0]
    xf = _layer_tail(xf, o_a, o_b, bf(w_out[:A_W]), bf(w_out[A_W:]), norm_ffn[0],
                     bf(ffn_w_gate[0]), bf(ffn_w_up[0]), bf(ffn_w_down[0]), norm_final, False)

    w_in = cd_w_in[0]
    o = np.cumsum([0, C_W, C_W + 2 * C_BC, C_HEADS, D_W, D_W, D_W, D_HEADS])
    small_w = jnp.concatenate([w_in[:, o[2]:o[3]], w_in[:, o[6]:o[7]]], axis=1)
    weights = [bf(w_in[:, o[0]:o[1]]), bf(w_in[:, o[1]:o[2]]), bf(_pad_cols(small_w)),
               bf(w_in[:, o[3]:o[4]]), bf(w_in[:, o[4]:o[5]]), bf(w_in[:, o[5]:o[6]])]
    c_z, c_xbc, cd_small, d_q, d_k, d_v = _norm_proj(
        xf, norm_mix[1], weights, [F32, F32, F32, BF16, BF16, BF16])
    y_c = _ssd(c_xbc, cd_small, c_z, cd_conv_w[0], cd_conv_b[0], cd_dt_bias[0], cd_a_log[0], cd_d_skip[0],
               cd_norm_w[0], bsz, seq_len)
    fk = _fox_prep(cd_small, cd_f_bias[0], bsz, seq_len)
    o_d = _fox_attention(d_q, d_k, d_v, fk, bsz, seq_len)
    w_out = cd_w_out[0]
    xf = _layer_tail(xf, y_c, o_d, bf(w_out[:C_W]), bf(w_out[C_W:]), norm_ffn[1],
                     bf(ffn_w_gate[1]), bf(ffn_w_up[1]), bf(ffn_w_down[1]), norm_final, True)
    return xf.reshape(bsz, seq_len, d)
```

```python
import functools

import jax
import jax.numpy as jnp
import numpy as np
from jax import lax
from jax.experimental import pallas as pl
from jax.experimental.pallas import tpu as pltpu

F32 = jnp.float32
BF16 = jnp.bfloat16

D_MODEL = 1024
CHUNK = 64
EPS = 1e-6
CONV_K = 4
A_HEADS, A_HEAD_DIM, A_LEFT_CHUNKS, A_MAX_REL = 8, 64, 8, 256
B_HEADS, B_HEAD_DIM = 4, 128
C_HEADS, C_HEAD_DIM, C_GROUPS, C_STATE = 8, 64, 2, 128
D_HEADS, D_HEAD_DIM = 8, 64
A_W = A_HEADS * A_HEAD_DIM
B_W = B_HEADS * B_HEAD_DIM
C_W = C_HEADS * C_HEAD_DIM
D_W = D_HEADS * D_HEAD_DIM
C_BC = C_GROUPS * C_STATE

LANES = 128
SUBLANES = 8
VMEM_LIMIT_BYTES = 56 * 1024 * 1024
NEG_BIG = -1e30

ROW_TILE = 512
FFN_CHUNK = 256
BAND_TQ = 128
BAND_WIN = BAND_TQ + A_LEFT_CHUNKS * CHUNK
FOX_T = 512
FOX_ROWS = 32
SCAN_ROWS = 512
SSD_CHUNKS = 2
GDN_BUILD_CHUNKS = 2


def _cparams(*sem):
    return pltpu.CompilerParams(dimension_semantics=sem, vmem_limit_bytes=VMEM_LIMIT_BYTES)


def _resident(shape):
    nd = len(shape)
    return pl.BlockSpec(shape, lambda *_: (0,) * nd, pipeline_mode=pl.Buffered(1))


def _dot(a, b):
    return jnp.dot(a, b, preferred_element_type=F32)


def _dot_nt(a, b):
    return lax.dot_general(a, b, (((1,), (1,)), ((), ())), preferred_element_type=F32)


def _dot_tn(a, b):
    return lax.dot_general(a, b, (((0,), (0,)), ((), ())), preferred_element_type=F32)


def _rms(x, w):
    return x * lax.rsqrt(jnp.mean(x * x, axis=-1, keepdims=True) + EPS) * w


def _split3(x):
    hi = x.astype(BF16)
    r1 = x - hi.astype(F32)
    mid = r1.astype(BF16)
    lo = (r1 - mid.astype(F32)).astype(BF16)
    return hi, mid, lo


def _cumsum_rows(tril, x):
    hi, mid, lo = _split3(x)
    return _dot(tril, hi) + _dot(tril, mid) + _dot(tril, lo)


def _tril_masks(n):
    r = lax.broadcasted_iota(jnp.int32, (n, n), 0)
    c = lax.broadcasted_iota(jnp.int32, (n, n), 1)
    return r >= c, r > c


def _norm_proj_kernel(x_ref, nw_ref, *refs, transposed):
    n_out = len(refs) // 2
    h = _rms(x_ref[...], nw_ref[...]).astype(BF16)
    for w_ref, o_ref, tr in zip(refs[:n_out], refs[n_out:], transposed):
        out = _dot_nt(w_ref[...], h) if tr else _dot(h, w_ref[...])
        o_ref[...] = out.astype(o_ref.dtype)


def _norm_proj(x, norm_w, weights, out_dtypes, transposed=None):
    n, d = x.shape
    transposed = transposed or (False,) * len(weights)
    in_specs = [pl.BlockSpec((ROW_TILE, d), lambda i: (i, 0)), _resident((1, d))]
    in_specs += [_resident(w.shape) for w in weights]
    out_shape, out_specs = [], []
    for w, dt, tr in zip(weights, out_dtypes, transposed):
        if tr:
            out_shape.append(jax.ShapeDtypeStruct((n // ROW_TILE, w.shape[0], ROW_TILE), dt))
            out_specs.append(pl.BlockSpec((None, w.shape[0], ROW_TILE), lambda i: (i, 0, 0)))
        else:
            out_shape.append(jax.ShapeDtypeStruct((n, w.shape[1]), dt))
            out_specs.append(pl.BlockSpec((ROW_TILE, w.shape[1]), lambda i: (i, 0)))
    return pl.pallas_call(
        functools.partial(_norm_proj_kernel, transposed=tuple(transposed)),
        grid=(n // ROW_TILE,), in_specs=in_specs, out_specs=out_specs,
        out_shape=out_shape, compiler_params=_cparams("parallel"), name="norm_proj",
    )(x, norm_w.reshape(1, d), *weights)


def _tail_kernel(x_ref, a_ref, b_ref, woa_ref, wob_ref, nw_ref, wg_ref, wu_ref, wd_ref, fn_ref,
                 o_ref, acc_ref, *, d_ff, final_norm):
    x1 = x_ref[...] + _dot(a_ref[...], woa_ref[...]) + _dot(b_ref[...], wob_ref[...])
    h = _rms(x1, nw_ref[...]).astype(BF16)
    acc_ref[...] = x1
    ffn = None
    for c in range(d_ff // FFN_CHUNK):
        cols = slice(c * FFN_CHUNK, (c + 1) * FFN_CHUNK)
        g = _dot(h, wg_ref[:, cols])
        u = _dot(h, wu_ref[:, cols])
        part = _dot((g * jax.nn.sigmoid(g) * u).astype(BF16), wd_ref[cols, :])
        ffn = part if ffn is None else ffn + part
    y = acc_ref[...] + ffn
    if final_norm:
        y = _rms(y, fn_ref[...])
    o_ref[...] = y


def _layer_tail(x, mix_a, mix_b, wo_a, wo_b, norm_w, wg, wu, wd, final_w, final_norm):
    n, d = x.shape
    d_ff = wg.shape[1]
    row = lambda w: pl.BlockSpec((ROW_TILE, w), lambda i: (i, 0))
    in_specs = [row(d), row(mix_a.shape[1]), row(mix_b.shape[1]), _resident(wo_a.shape),
                _resident(wo_b.shape), _resident((1, d)), _resident(wg.shape), _resident(wu.shape),
                _resident(wd.shape), _resident((1, d))]
    return pl.pallas_call(
        functools.partial(_tail_kernel, d_ff=d_ff, final_norm=final_norm),
        grid=(n // ROW_TILE,), in_specs=in_specs, out_specs=row(d),
        out_shape=jax.ShapeDtypeStruct((n, d), F32),
        scratch_shapes=[pltpu.VMEM((ROW_TILE, d), F32)],
        compiler_params=_cparams("parallel"), name="layer_tail",
    )(x, mix_a, mix_b, wo_a, wo_b, norm_w.reshape(1, d), wg, wu, wd, final_w.reshape(1, d))


def _band_kernel(q_ref, *refs):
    n_blk = BAND_WIN // BAND_TQ
    k_refs, v_refs = refs[:n_blk], refs[n_blk:2 * n_blk]
    bias_ref, o_ref, s_ref, p_ref = refs[2 * n_blk:]
    i = pl.program_id(1)
    lane = lax.broadcasted_iota(jnp.int32, (BAND_TQ, LANES), 1)
    low_half = lane < A_HEAD_DIM
    win_lane = lax.broadcasted_iota(jnp.int32, (BAND_WIN, LANES), 1)
    key_blk = lax.broadcasted_iota(jnp.int32, (1, BAND_WIN), 1) // BAND_TQ
    before_start = jnp.where(key_blk >= (n_blk - 1) - i, 0.0, NEG_BIG)
    pairs = A_W // LANES
    n_rb = BAND_TQ // FOX_ROWS

    v_aug = []
    for pr in range(pairs):
        cols = slice(pr * LANES, (pr + 1) * LANES)
        k = jnp.concatenate([r[:, cols] for r in k_refs], axis=0)
        v = jnp.concatenate([r[:, cols] for r in v_refs], axis=0)
        q = q_ref[:, cols] * jnp.asarray(A_HEAD_DIM ** -0.5, BF16)
        zero, one = jnp.zeros_like(q), jnp.ones_like(v)
        s_ref[2 * pr] = _dot_nt(jnp.where(low_half, q, zero), k) + (bias_ref[2 * pr] + before_start)
        s_ref[2 * pr + 1] = _dot_nt(jnp.where(low_half, zero, q), k) + (bias_ref[2 * pr + 1] + before_start)
        v_aug += [jnp.where(win_lane < A_HEAD_DIM, v, one), jnp.where(win_lane < A_HEAD_DIM, one, v)]

    blocks = [(h, rb) for h in range(A_HEADS) for rb in range(n_rb)]
    rows = lambda rb: slice(rb * FOX_ROWS, (rb + 1) * FOX_ROWS)
    row_max = [jnp.max(s_ref[h, rows(rb), :], axis=-1, keepdims=True) for h, rb in blocks]
    for (h, rb), mx in zip(blocks, row_max):
        p_ref[h, rows(rb), :] = jnp.exp(s_ref[h, rows(rb), :] - mx).astype(BF16)

    for pr in range(pairs):
        o_even = _dot(p_ref[2 * pr], v_aug[2 * pr])
        o_odd = _dot(p_ref[2 * pr + 1], v_aug[2 * pr + 1])
        out = jnp.where(low_half, o_even / pltpu.roll(o_even, A_HEAD_DIM, axis=1),
                        o_odd / pltpu.roll(o_odd, A_HEAD_DIM, axis=1))
        o_ref[:, pr * LANES:(pr + 1) * LANES] = out.astype(o_ref.dtype)


def _band_bias_tiles(rel_bias):
    r = np.arange(BAND_TQ)[:, None]
    c = np.arange(BAND_WIN)[None, :]
    offs = np.arange(-(BAND_TQ - 1), BAND_WIN)
    idx = np.clip(A_LEFT_CHUNKS * CHUNK - offs, -A_MAX_REL, A_MAX_REL) + A_MAX_REL
    period = offs.size + 1
    per_off = jnp.pad(rel_bias.astype(F32)[:, idx], ((0, 0), (0, 1)))
    skew = jnp.tile(per_off, (1, BAND_TQ))[:, :BAND_TQ * (period - 1)]
    skew = skew.reshape(-1, BAND_TQ, period - 1)[:, :, BAND_TQ - 1:BAND_TQ - 1 + BAND_WIN]
    qc, kc = r // CHUNK, c // CHUNK
    allowed = (kc >= qc) & (kc <= qc + A_LEFT_CHUNKS)
    return jnp.where(allowed[None], skew, NEG_BIG)


def _band_attention(q, k, v, rel_bias, bsz, seq_len):
    n_blk = BAND_WIN // BAND_TQ
    nq = seq_len // BAND_TQ
    q3, k3, v3 = (a.reshape(bsz, seq_len, A_W) for a in (q, k, v))
    blk = (None, BAND_TQ, A_W)
    kv_specs = [pl.BlockSpec(blk, lambda b, i, j=j: (b, jnp.maximum(i - (n_blk - 1) + j, 0), 0))
                for j in range(n_blk)]
    out = pl.pallas_call(
        _band_kernel, grid=(bsz, nq),
        in_specs=[pl.BlockSpec(blk, lambda b, i: (b, i, 0))] + kv_specs + kv_specs
        + [_resident((A_HEADS, BAND_TQ, BAND_WIN))],
        out_specs=pl.BlockSpec(blk, lambda b, i: (b, i, 0)),
        out_shape=jax.ShapeDtypeStruct((bsz, seq_len, A_W), BF16),
        scratch_shapes=[pltpu.VMEM((A_HEADS, BAND_TQ, BAND_WIN), F32),
                        pltpu.VMEM((A_HEADS, BAND_TQ, BAND_WIN), BF16)],
        compiler_params=_cparams("parallel", "parallel"), name="band_attention",
    )(q3, *([k3] * n_blk), *([v3] * n_blk), _band_bias_tiles(rel_bias))
    return out.reshape(bsz * seq_len, A_W)


def _stage_conv_input(x_ref, pad_ref):
    rows = x_ref.shape[1]

    @pl.when(pl.program_id(0) == 0)
    def _():
        pad_ref[:, 0:SUBLANES, :] = jnp.zeros((pad_ref.shape[0], SUBLANES, pad_ref.shape[2]), F32)

    @pl.when(pl.program_id(0) > 0)
    def _():
        pad_ref[:, 0:SUBLANES, :] = pad_ref[:, rows:rows + SUBLANES, :]

    pad_ref[:, SUBLANES:, :] = x_ref[...]


def _conv_silu_rows(pad_ref, w_ref, b, row0, cols, bias_ref=None):
    window = pad_ref[b, pl.ds(row0, SUBLANES + CHUNK), cols]
    acc = None
    for back in range(CONV_K):
        term = w_ref[CONV_K - 1 - back:CONV_K - back, cols] * window[SUBLANES - back:SUBLANES - back + CHUNK]
        acc = term if acc is None else acc + term
    if bias_ref is not None:
        acc = acc + bias_ref[:, cols]
    return acc * jax.nn.sigmoid(acc)


def _gdn_kernel(qkv_ref, sm_ref, z_ref, cw_ref, par_ref, nw_ref, o_ref,
                pad_ref, state_ref, lhs_ref, add_ref, gl_ref):
    bsz, rows_per_step = qkv_ref.shape[0], qkv_ref.shape[1]
    n_chunks = rows_per_step // CHUNK
    heads = [(b, h) for b in range(bsz) for h in range(B_HEADS)]
    units = [(cc, b, h) for cc in range(GDN_BUILD_CHUNKS) for b, h in heads]

    @pl.when(pl.program_id(0) == 0)
    def _():
        state_ref[...] = jnp.zeros_like(state_ref)

    _stage_conv_input(qkv_ref, pad_ref)
    causal, strict = _tril_masks(CHUNK)
    tril = jnp.where(causal, 1.0, 0.0).astype(BF16)
    eye = jnp.where(causal & ~strict, 1.0, 0.0)
    a_row = -jnp.exp(par_ref[0:1, :])
    dt_bias_row = par_ref[1:2, :]

    def head_cols(part, h):
        return slice(part * B_W + h * B_HEAD_DIM, part * B_W + (h + 1) * B_HEAD_DIM)

    def build(cp, carry):
        row0 = [pl.multiple_of((cp * GDN_BUILD_CHUNKS + cc) * CHUNK, CHUNK) for cc in range(GDN_BUILD_CHUNKS)]
        sm = {(cc, b): sm_ref[b, pl.ds(row0[cc], CHUNK), :] for cc in range(GDN_BUILD_CHUNKS) for b in range(bsz)}
        beta_all = {key: jax.nn.sigmoid(x) for key, x in sm.items()}
        gc_all = {key: _cumsum_rows(tril, a_row * jax.nn.softplus(x + dt_bias_row)) for key, x in sm.items()}
        gc_all_t = {key: x.T for key, x in gc_all.items()}
        q, k, v = ([_conv_silu_rows(pad_ref, cw_ref, b, row0[cc], head_cols(part, h)) for cc, b, h in units]
                   for part in range(3))
        q = [x * lax.rsqrt(jnp.sum(x * x, axis=-1, keepdims=True) + EPS) * (B_HEAD_DIM ** -0.5) for x in q]
        k = [x * lax.rsqrt(jnp.sum(x * x, axis=-1, keepdims=True) + EPS) for x in k]
        beta = [beta_all[cc, b][:, h:h + 1] for cc, b, h in units]
        gc = [gc_all[cc, b][:, B_HEADS + h:B_HEADS + h + 1] for cc, b, h in units]
        gc_row = [gc_all_t[cc, b][B_HEADS + h:B_HEADS + h + 1, :] for cc, b, h in units]
        decay = [jnp.exp(jnp.where(causal, g - gr, NEG_BIG)) for g, gr in zip(gc, gc_row)]
        kb = [x.astype(BF16) for x in k]
        kk = [_dot_nt(x, x) for x in kb]
        qk = [_dot_nt(x.astype(BF16), y) for x, y in zip(q, kb)]
        a_mat = [jnp.where(strict, bt * x * d, 0.0) for bt, x, d in zip(beta, kk, decay)]
        inv = [eye - a for a in a_mat]
        power = a_mat
        for _ in range(5):
            power = [_dot(x, x) for x in power]
            inv = [i + _dot(i, x) for i, x in zip(inv, power)]
        exp_gc = [jnp.exp(g) for g in gc]
        rhs = [jnp.concatenate([ki * (bt * e), vi * bt], axis=1) for ki, vi, bt, e in zip(k, v, beta, exp_gc)]
        wu = [_dot(i, r).astype(BF16) for i, r in zip(inv, rhs)]
        attn = [(x * d).astype(BF16) for x, d in zip(qk, decay)]
        gc_last = [g[CHUNK - 1:CHUNK, :] for g in gc]
        k_st = [(ki * jnp.exp(gl - g)).astype(BF16) for ki, gl, g in zip(k, gc_last, gc)]
        top = [_dot_tn(x, y) for x, y in zip(k_st, wu)]
        bot = [_dot(x, y) for x, y in zip(attn, wu)]
        for i in range(len(units)):
            slot = cp * len(units) + i
            lhs_ref[slot, 0:B_HEAD_DIM, :] = (-top[i][:, :B_HEAD_DIM]).astype(BF16)
            lhs_ref[slot, B_HEAD_DIM:, :] = (q[i] * exp_gc[i] - bot[i][:, :B_HEAD_DIM]).astype(BF16)
            add_ref[slot, 0:B_HEAD_DIM, :] = top[i][:, B_HEAD_DIM:]
            add_ref[slot, B_HEAD_DIM:, :] = bot[i][:, B_HEAD_DIM:]
            gl_ref[slot] = jnp.broadcast_to(jnp.exp(gc_last[i]), (SUBLANES, LANES))
        return carry

    lax.fori_loop(0, n_chunks // GDN_BUILD_CHUNKS, build, 0)

    def scan(c, carry):
        rows = pl.ds(pl.multiple_of(c * CHUNK, CHUNK), CHUNK)
        states = [state_ref[i] for i in range(len(heads))]
        res = [_dot(lhs_ref[c * len(heads) + i], states[i].astype(BF16)) + add_ref[c * len(heads) + i]
               for i in range(len(heads))]
        for i, (b, h) in enumerate(heads):
            state_ref[i] = gl_ref[c * len(heads) + i][0:1, :] * states[i] + res[i][:B_HEAD_DIM]
            zz = z_ref[b, rows, head_cols(0, h)]
            y = _rms(res[i][B_HEAD_DIM:], nw_ref[...]) * (zz * jax.nn.sigmoid(zz))
            o_ref[b, rows, head_cols(0, h)] = y.astype(o_ref.dtype)
        return carry

    lax.fori_loop(0, n_chunks, scan, 0)


def _gated_delta(qkv_raw, small, z, conv_w, a_log, dt_bias, norm_w, bsz, seq_len):
    par = jnp.zeros((2, LANES), F32)
    par = par.at[0, B_HEADS:2 * B_HEADS].set(a_log.astype(F32))
    par = par.at[1, B_HEADS:2 * B_HEADS].set(dt_bias.astype(F32))
    blk = lambda w: pl.BlockSpec((bsz, SCAN_ROWS, w), lambda t: (0, t, 0))
    n_units = bsz * B_HEADS
    n_slots = n_units * (SCAN_ROWS // CHUNK)
    out = pl.pallas_call(
        _gdn_kernel, grid=(seq_len // SCAN_ROWS,),
        in_specs=[blk(3 * B_W), blk(LANES), blk(B_W), _resident((CONV_K, 3 * B_W)), _resident((2, LANES)),
                  _resident((1, B_HEAD_DIM))],
        out_specs=blk(B_W),
        out_shape=jax.ShapeDtypeStruct((bsz, seq_len, B_W), BF16),
        scratch_shapes=[pltpu.VMEM((bsz, SUBLANES + SCAN_ROWS, 3 * B_W), F32),
                        pltpu.VMEM((n_units, B_HEAD_DIM, B_HEAD_DIM), F32),
                        pltpu.VMEM((n_slots, B_HEAD_DIM + CHUNK, B_HEAD_DIM), BF16),
                        pltpu.VMEM((n_slots, B_HEAD_DIM + CHUNK, B_HEAD_DIM), F32),
                        pltpu.VMEM((n_slots, SUBLANES, LANES), F32)],
        compiler_params=_cparams("arbitrary"), name="gated_delta",
    )(qkv_raw.reshape(bsz, seq_len, 3 * B_W), small.reshape(bsz, seq_len, LANES),
      z.reshape(bsz, seq_len, B_W), conv_w.astype(F32), par, norm_w.reshape(1, B_HEAD_DIM).astype(F32))
    return out.reshape(bsz * seq_len, B_W)


def _ssd_kernel(xbc_ref, sm_ref, z_ref, cw_ref, cb_ref, par_ref, skip_ref, nw_ref, o_ref, pad_ref, state_ref):
    bsz, rows_per_step = xbc_ref.shape[0], xbc_ref.shape[1]
    group_w = C_W // C_GROUPS
    heads_per_group = C_HEADS // C_GROUPS
    pairs_per_group = heads_per_group // 2

    @pl.when(pl.program_id(0) == 0)
    def _():
        state_ref[...] = jnp.zeros_like(state_ref)

    _stage_conv_input(xbc_ref, pad_ref)
    causal, _ = _tril_masks(CHUNK)
    tril = jnp.where(causal, 1.0, 0.0).astype(BF16)
    a_row = -jnp.exp(par_ref[0:1, :])
    dt_bias_row = par_ref[1:2, :]
    lane = lax.broadcasted_iota(jnp.int32, (CHUNK, LANES), 1)
    low_half = lane < C_HEAD_DIM

    def per_head_lanes(cols, h0):
        return jnp.where(low_half, cols[:, h0:h0 + 1], cols[:, h0 + 1:h0 + 2])

    def pair_cols(pr):
        return slice(pr * LANES, (pr + 1) * LANES)

    def chunk_pair(cp, carry):
        units = [(cc, b) for cc in range(SSD_CHUNKS) for b in range(bsz)]
        groups = [(u, g) for u in units for g in range(C_GROUPS)]
        row0 = [pl.multiple_of((cp * SSD_CHUNKS + cc) * CHUNK, CHUNK) for cc in range(SSD_CHUNKS)]
        conv = lambda u, cols: _conv_silu_rows(pad_ref, cw_ref, u[1], row0[u[0]], cols, cb_ref)
        sm = {u: sm_ref[u[1], pl.ds(row0[u[0]], CHUNK), :] for u in units}
        dt_all = {u: jax.nn.softplus(sm[u] + dt_bias_row) for u in units}
        da_cs = {u: _cumsum_rows(tril, dt_all[u] * a_row) for u in units}
        da_cs_t = {u: da_cs[u].T for u in units}
        dt_t = {u: dt_all[u].T for u in units}
        da_last = {u: da_cs[u][CHUNK - 1:CHUNK, :] for u in units}
        exp_da = {u: jnp.exp(da_cs[u]) for u in units}
        to_end = {u: jnp.exp(da_last[u] - da_cs[u]) * dt_all[u] for u in units}
        chunk_decay = {u: jnp.exp(da_last[u]) for u in units}
        x2 = {(u, pr): conv(u, pair_cols(pr)) for u in units for pr in range(C_HEADS // 2)}
        bm = {(u, g): conv(u, slice(C_W + g * C_STATE, C_W + (g + 1) * C_STATE)).astype(BF16) for u, g in groups}
        cm = {(u, g): conv(u, slice(C_W + C_BC + g * C_STATE, C_W + C_BC + (g + 1) * C_STATE)).astype(BF16)
              for u, g in groups}
        cb = {key: _dot_nt(cm[key], bm[key]) for key in groups}
        xw = {(u, g): jnp.concatenate(
            [(x2[u, g * pairs_per_group + j] * per_head_lanes(to_end[u], g * heads_per_group + 2 * j)).astype(BF16)
             for j in range(pairs_per_group)], axis=1) for u, g in groups}
        inflow = {key: _dot_tn(bm[key], xw[key]) for key in groups}
        y_diag = {}
        for u in units:
            for h in range(C_HEADS):
                seg = da_cs[u][:, h:h + 1] - da_cs_t[u][h:h + 1, :]
                wts = cb[u, h // heads_per_group] * jnp.exp(jnp.where(causal, seg, NEG_BIG)) * dt_t[u][h:h + 1, :]
                y_diag[u, h] = _dot(wts.astype(BF16), x2[u, h // 2].astype(BF16))
        for u in units:
            cc, b = u
            rows = pl.ds(row0[cc], CHUNK)
            for g in range(C_GROUPS):
                gl = slice(g * group_w, (g + 1) * group_w)
                prev = state_ref[b, :, gl]
                y_off = _dot(cm[u, g], prev.astype(BF16))
                dec = jnp.concatenate(
                    [jnp.where(low_half[0:1], chunk_decay[u][:, h0:h0 + 1], chunk_decay[u][:, h0 + 1:h0 + 2])
                     for h0 in range(g * heads_per_group, (g + 1) * heads_per_group, 2)], axis=1)
                state_ref[b, :, gl] = prev * dec + inflow[u, g]
                ys = []
                for j in range(pairs_per_group):
                    pr = g * pairs_per_group + j
                    h0 = 2 * pr
                    y = jnp.where(low_half, y_diag[u, h0], y_diag[u, h0 + 1])
                    y = y + y_off[:, pair_cols(j)] * per_head_lanes(exp_da[u], h0) + skip_ref[:, pair_cols(pr)] * x2[u, pr]
                    zz = z_ref[b, rows, pair_cols(pr)]
                    ys.append(y * (zz * jax.nn.sigmoid(zz)))
                yg = jnp.concatenate(ys, axis=1)
                o_ref[b, rows, gl] = _rms(yg, nw_ref[:, gl]).astype(o_ref.dtype)
        return carry

    lax.fori_loop(0, rows_per_step // (CHUNK * SSD_CHUNKS), chunk_pair, 0)


def _ssd(xbc_raw, small, z, conv_w, conv_b, dt_bias, a_log, d_skip, norm_w, bsz, seq_len):
    par = jnp.zeros((2, LANES), F32)
    par = par.at[0, :C_HEADS].set(a_log.astype(F32))
    par = par.at[1, :C_HEADS].set(dt_bias.astype(F32))
    skip = jnp.repeat(d_skip.astype(F32), C_HEAD_DIM).reshape(1, C_W)
    width = C_W + 2 * C_BC
    blk = lambda w: pl.BlockSpec((bsz, SCAN_ROWS, w), lambda t: (0, t, 0))
    out = pl.pallas_call(
        _ssd_kernel, grid=(seq_len // SCAN_ROWS,),
        in_specs=[blk(width), blk(LANES), blk(C_W), _resident((CONV_K, width)), _resident((1, width)),
                  _resident((2, LANES)), _resident((1, C_W)), _resident((1, C_W))],
        out_specs=blk(C_W),
        out_shape=jax.ShapeDtypeStruct((bsz, seq_len, C_W), BF16),
        scratch_shapes=[pltpu.VMEM((bsz, SUBLANES + SCAN_ROWS, width), F32),
                        pltpu.VMEM((bsz, C_STATE, C_W), F32)],
        compiler_params=_cparams("arbitrary"), name="ssd",
    )(xbc_raw.reshape(bsz, seq_len, width), small.reshape(bsz, seq_len, LANES), z.reshape(bsz, seq_len, C_W),
      conv_w.astype(F32), conv_b.astype(F32).reshape(1, width), par, skip, norm_w.reshape(1, C_W).astype(F32))
    return out.reshape(bsz * seq_len, C_W)


FOX_F_PIECES = 3


def _fox_f_lane(h):
    return (h // 2) * LANES + (D_HEAD_DIM if h % 2 == 0 else 0)


def _fox_prep_kernel(sm_ref, fb_ref, place_ref, fk_ref, carry_ref):
    tm = sm_ref.shape[0]

    @pl.when(pl.program_id(1) == 0)
    def _():
        carry_ref[...] = jnp.zeros_like(carry_ref)

    causal, _ = _tril_masks(tm)
    tril = jnp.where(causal, 1.0, 0.0).astype(BF16)
    log_f = jax.nn.log_sigmoid(sm_ref[...] + fb_ref[...])
    f_cum = _cumsum_rows(tril, log_f) + carry_ref[...]
    carry_ref[...] = f_cum[tm - 1:tm, :]
    pieces = jnp.concatenate(_split3(-f_cum), axis=1)
    fk_ref[...] = _dot(pieces, place_ref[...]).astype(fk_ref.dtype)


def _fox_prep(small, f_bias, bsz, seq_len):
    fb = jnp.zeros((1, LANES), F32).at[0, D_HEADS:2 * D_HEADS].set(f_bias.astype(F32))
    place = np.zeros((FOX_F_PIECES * LANES, D_W), np.float32)
    for h in range(D_HEADS):
        for piece in range(FOX_F_PIECES):
            place[piece * LANES + D_HEADS + h, _fox_f_lane(h) + piece] = 1.0
    nt = seq_len // FOX_T
    return pl.pallas_call(
        _fox_prep_kernel, grid=(bsz, nt),
        in_specs=[pl.BlockSpec((None, FOX_T, LANES), lambda b, t: (b, t, 0)), _resident((1, LANES)),
                  _resident(place.shape)],
        out_specs=pl.BlockSpec((None, FOX_T, D_W), lambda b, t: (b, t, 0)),
        out_shape=jax.ShapeDtypeStruct((bsz, seq_len, D_W), BF16),
        scratch_shapes=[pltpu.VMEM((1, LANES), F32)],
        compiler_params=_cparams("parallel", "arbitrary"), name="fox_prep",
    )(small.reshape(bsz, seq_len, LANES), fb, jnp.asarray(place, BF16))


def _fox_kernel(q_ref, k_ref, vt_ref, fk_ref, o_ref,
                kk_ref, vv_ref, sa_ref, sb_ref, pa_ref, pb_ref, aa_ref, ab_ref, m_ref, acc_ref):
    i = pl.program_id(2)
    t = FOX_T
    nt = k_ref.shape[0] // t
    n_rb = t // FOX_ROWS
    lane = lax.broadcasted_iota(jnp.int32, (t, LANES), 1)
    low_half = lane < D_HEAD_DIM
    top_half = lax.broadcasted_iota(jnp.int32, (LANES, t), 0) < D_HEAD_DIM

    @pl.when(i == 0)
    def _():
        for jt in range(nt):
            rows = slice(jt * t, (jt + 1) * t)
            kt, ft = k_ref[rows, :], fk_ref[rows, :]
            kk_ref[0, jt] = jnp.where(low_half, kt, ft)
            kk_ref[1, jt] = jnp.where(low_half, ft, kt)
            vt = vt_ref[jt]
            one = jnp.ones_like(vt)
            vv_ref[0, jt] = jnp.where(top_half, vt, one)
            vv_ref[1, jt] = jnp.where(top_half, one, vt)
        kk_ref[0, nt] = jnp.where(low_half, 0.0, NEG_BIG).astype(BF16)
        kk_ref[1, nt] = jnp.where(low_half, NEG_BIG, 0.0).astype(BF16)

    q = q_ref[...] * jnp.asarray(D_HEAD_DIM ** -0.5, BF16)
    ones_upto = lambda n: jnp.where(lane < n, 1.0, 0.0).astype(BF16)
    qs = (jnp.where(low_half, q, ones_upto(_fox_f_lane(0) + FOX_F_PIECES)),
          jnp.where(low_half, ones_upto(_fox_f_lane(1) + FOX_F_PIECES), q))
    key_in_blk = lax.broadcasted_iota(jnp.int32, (FOX_ROWS, t), 0)
    query = lax.broadcasted_iota(jnp.int32, (FOX_ROWS, t), 1)

    m_ref[...] = jnp.full(m_ref.shape, NEG_BIG, F32)
    acc_ref[...] = jnp.zeros_like(acc_ref)

    def scores(j, s_ref):
        jj = jnp.where(j >= 0, j, nt)
        for half in range(2):
            s_ref[half] = _dot_nt(kk_ref[half, jj], qs[half])

    def softmax_cols(s_ref, p_ref, a_ref, diagonal):
        def block(half, rb):
            s = s_ref[half, rb * FOX_ROWS:(rb + 1) * FOX_ROWS, :]
            if diagonal:
                s = jnp.where(key_in_blk + rb * FOX_ROWS <= query, s, NEG_BIG)
            return s

        tile_max = []
        for half in range(2):
            m8 = None
            for rb in range(n_rb):
                s = block(half, rb)
                b8 = jnp.maximum(jnp.maximum(s[0:8], s[8:16]), jnp.maximum(s[16:24], s[24:32]))
                m8 = b8 if m8 is None else jnp.maximum(m8, b8)
            tile_max.append(jnp.max(m8, axis=0, keepdims=True))
        for half in range(2):
            m_old = m_ref[half]
            m_new = jnp.maximum(m_old, tile_max[half])
            m_ref[half] = m_new
            a_ref[half] = jnp.exp(m_old - m_new)
            for rb in range(n_rb):
                p_ref[half, rb * FOX_ROWS:(rb + 1) * FOX_ROWS, :] = jnp.exp(block(half, rb) - m_new).astype(BF16)

    def accumulate(j, p_ref, a_ref):
        jj = jnp.maximum(j, 0)
        for half in range(2):
            acc_ref[half] = a_ref[half] * acc_ref[half] + _dot(vv_ref[half, jj], p_ref[half])

    scores(i, sb_ref)
    scores(i - 1, sa_ref)
    softmax_cols(sb_ref, pb_ref, ab_ref, True)

    def pair(n, carry):
        j = i - 1 - 2 * n
        scores(j - 1, sb_ref)
        softmax_cols(sa_ref, pa_ref, aa_ref, False)
        accumulate(j + 1, pb_ref, ab_ref)
        scores(j - 2, sa_ref)
        softmax_cols(sb_ref, pb_ref, ab_ref, False)
        accumulate(j, pa_ref, aa_ref)
        return carry

    n_pairs = (i + 1) // 2
    lax.fori_loop(0, n_pairs, pair, 0)
    accumulate(i - 2 * n_pairs, pb_ref, ab_ref)
    acc0, acc1 = acc_ref[0], acc_ref[1]
    out_t = jnp.where(top_half, acc0 / acc0[D_HEAD_DIM:D_HEAD_DIM + 1, :], acc1 / acc1[0:1, :])
    o_ref[...] = out_t.T.astype(o_ref.dtype)


def _fox_attention(q, k, v_t, fk, bsz, seq_len):
    nt = seq_len // FOX_T
    pairs = D_W // LANES
    q3, k3 = (a.reshape(bsz, seq_len, D_W) for a in (q, k))
    whole_seq = pl.BlockSpec((None, seq_len, LANES), lambda b, p, i: (b, 0, p))
    out = pl.pallas_call(
        _fox_kernel, grid=(bsz, pairs, nt),
        in_specs=[pl.BlockSpec((None, FOX_T, LANES), lambda b, p, i: (b, i, p)), whole_seq,
                  pl.BlockSpec((nt, LANES, FOX_T), lambda b, p, i: (b, p, 0)), whole_seq],
        out_specs=pl.BlockSpec((None, FOX_T, LANES), lambda b, p, i: (b, i, p)),
        out_shape=jax.ShapeDtypeStruct((bsz, seq_len, D_W), BF16),
        scratch_shapes=[pltpu.VMEM((2, nt + 1, FOX_T, LANES), BF16), pltpu.VMEM((2, nt, LANES, FOX_T), BF16)]
        + [pltpu.VMEM((2, FOX_T, FOX_T), F32)] * 2 + [pltpu.VMEM((2, FOX_T, FOX_T), BF16)] * 2
        + [pltpu.VMEM((2, 1, FOX_T), F32)] * 3 + [pltpu.VMEM((2, LANES, FOX_T), F32)],
        compiler_params=_cparams("parallel", "parallel", "arbitrary"), name="fox_attention",
    )(q3, k3, v_t, fk)
    return out.reshape(bsz * seq_len, D_W)


def _pad_cols(w, width=LANES):
    return jnp.pad(w, ((0, 0), (0, width - w.shape[1])))


def kernel(x, norm_mix, norm_ffn, norm_final, ffn_w_gate, ffn_w_up, ffn_w_down, ab_w_in, ab_rel_bias, ab_conv_w, ab_a_log, ab_dt_bias, ab_norm_w, ab_w_out, cd_w_in, cd_conv_w, cd_conv_b, cd_dt_bias, cd_a_log, cd_d_skip, cd_norm_w, cd_f_bias, cd_w_out):
    bsz, seq_len, d = x.shape
    n = bsz * seq_len
    xf = x.reshape(n, d)
    bf = lambda w: w.astype(BF16)

    w_in = ab_w_in[0]
    o = np.cumsum([0, A_W, A_W, A_W, 3 * B_W, B_HEADS, B_HEADS, B_W])
    weights = [bf(w_in[:, o[0]:o[1]]), bf(w_in[:, o[1]:o[2]]), bf(w_in[:, o[2]:o[3]]),
               bf(w_in[:, o[3]:o[4]]), bf(_pad_cols(w_in[:, o[4]:o[6]])), bf(w_in[:, o[6]:o[7]])]
    a_q, a_k, a_v, b_qkv, b_small, b_z = _norm_proj(
        xf, norm_mix[0], weights, [BF16, BF16, BF16, F32, F32, F32])
    o_a = _band_attention(a_q, a_k, a_v, ab_rel_bias[0], bsz, seq_len)
    o_b = _gated_delta(b_qkv, b_small, b_z, ab_conv_w[0], ab_a_log[0], ab_dt_bias[0], ab_norm_w[0], bsz, seq_len)
    w_out = ab_w_out[0]
    xf = _layer_tail(xf, o_a, o_b, bf(w_out[:A_W]), bf(w_out[A_W:]), norm_ffn[0],
                     bf(ffn_w_gate[0]), bf(ffn_w_up[0]), bf(ffn_w_down[0]), norm_final, False)

    w_in = cd_w_in[0]
    o = np.cumsum([0, C_W, C_W + 2 * C_BC, C_HEADS, D_W, D_W, D_W, D_HEADS])
    small_w = jnp.concatenate([w_in[:, o[2]:o[3]], w_in[:, o[6]:o[7]]], axis=1)
    weights = [bf(w_in[:, o[0]:o[1]]), bf(w_in[:, o[1]:o[2]]), bf(_pad_cols(small_w)),
               bf(w_in[:, o[3]:o[4]]), bf(w_in[:, o[4]:o[5]]), bf(w_in[:, o[5]:o[6]].T)]
    c_z, c_xbc, cd_small, d_q, d_k, d_vt = _norm_proj(
        xf, norm_mix[1], weights, [F32, F32, F32, BF16, BF16, BF16], transposed=(False,) * 5 + (True,))
    y_c = _ssd(c_xbc, cd_small, c_z, cd_conv_w[0], cd_conv_b[0], cd_dt_bias[0], cd_a_log[0], cd_d_skip[0],
               cd_norm_w[0], bsz, seq_len)
    fk = _fox_prep(cd_small, cd_f_bias[0], bsz, seq_len)
    o_d = _fox_attention(d_q, d_k, d_vt, fk, bsz, seq_len)
    w_out = cd_w_out[0]
    xf = _layer_tail(xf, y_c, o_d, bf(w_out[:C_W]), bf(w_out[C_W:]), norm_ffn[1],
                     bf(ffn_w_gate[1]), bf(ffn_w_up[1]), bf(ffn_w_down[1]), norm_final, True)
    return xf.reshape(bsz, seq_len, d)
```

```python
import functools

import jax
import jax.numpy as jnp
import numpy as np
from jax import lax
from jax.experimental import pallas as pl
from jax.experimental.pallas import tpu as pltpu

F32 = jnp.float32
BF16 = jnp.bfloat16

D_MODEL = 1024
CHUNK = 64
EPS = 1e-6
CONV_K = 4
A_HEADS, A_HEAD_DIM, A_LEFT_CHUNKS, A_MAX_REL = 8, 64, 8, 256
B_HEADS, B_HEAD_DIM = 4, 128
C_HEADS, C_HEAD_DIM, C_GROUPS, C_STATE = 8, 64, 2, 128
D_HEADS, D_HEAD_DIM = 8, 64
A_W = A_HEADS * A_HEAD_DIM
B_W = B_HEADS * B_HEAD_DIM
C_W = C_HEADS * C_HEAD_DIM
D_W = D_HEADS * D_HEAD_DIM
C_BC = C_GROUPS * C_STATE

LANES = 128
SUBLANES = 8
VMEM_LIMIT_BYTES = 56 * 1024 * 1024
NEG_BIG = -1e30
LOG2_E = float(np.log2(np.e))

ROW_TILE = 512
FFN_CHUNK = 256
BAND_TQ = 256
BAND_WIN = BAND_TQ + A_LEFT_CHUNKS * CHUNK
FOX_T = 512
FOX_ROWS = 32
SCAN_ROWS = 512
SSD_CHUNKS = 4
GDN_BUILD_CHUNKS = 4


def _cparams(*sem):
    return pltpu.CompilerParams(dimension_semantics=sem, vmem_limit_bytes=VMEM_LIMIT_BYTES)


def _resident(shape):
    nd = len(shape)
    return pl.BlockSpec(shape, lambda *_: (0,) * nd, pipeline_mode=pl.Buffered(1))


def _dot(a, b):
    return jnp.dot(a, b, preferred_element_type=F32)


def _dot_nt(a, b):
    return lax.dot_general(a, b, (((1,), (1,)), ((), ())), preferred_element_type=F32)


def _dot_tn(a, b):
    return lax.dot_general(a, b, (((0,), (0,)), ((), ())), preferred_element_type=F32)


def _rms(x, w):
    return x * lax.rsqrt(jnp.mean(x * x, axis=-1, keepdims=True) + EPS) * w


def _split3(x):
    hi = x.astype(BF16)
    r1 = x - hi.astype(F32)
    mid = r1.astype(BF16)
    lo = (r1 - mid.astype(F32)).astype(BF16)
    return hi, mid, lo


def _cumsum_rows(tril, x):
    hi, mid, lo = _split3(x)
    return _dot(tril, hi) + _dot(tril, mid) + _dot(tril, lo)


def _tril_masks(n):
    r = lax.broadcasted_iota(jnp.int32, (n, n), 0)
    c = lax.broadcasted_iota(jnp.int32, (n, n), 1)
    return r >= c, r > c


def _norm_proj_kernel(x_ref, nw_ref, *refs, transposed):
    n_out = len(refs) // 2
    h = _rms(x_ref[...], nw_ref[...]).astype(BF16)
    for w_ref, o_ref, tr in zip(refs[:n_out], refs[n_out:], transposed):
        out = _dot_nt(w_ref[...], h) if tr else _dot(h, w_ref[...])
        o_ref[...] = out.astype(o_ref.dtype)


def _norm_proj(x, norm_w, weights, out_dtypes, transposed=None):
    n, d = x.shape
    transposed = transposed or (False,) * len(weights)
    in_specs = [pl.BlockSpec((ROW_TILE, d), lambda i: (i, 0)), _resident((1, d))]
    in_specs += [_resident(w.shape) for w in weights]
    out_shape, out_specs = [], []
    for w, dt, tr in zip(weights, out_dtypes, transposed):
        if tr:
            out_shape.append(jax.ShapeDtypeStruct((n // ROW_TILE, w.shape[0], ROW_TILE), dt))
            out_specs.append(pl.BlockSpec((None, w.shape[0], ROW_TILE), lambda i: (i, 0, 0)))
        else:
            out_shape.append(jax.ShapeDtypeStruct((n, w.shape[1]), dt))
            out_specs.append(pl.BlockSpec((ROW_TILE, w.shape[1]), lambda i: (i, 0)))
    return pl.pallas_call(
        functools.partial(_norm_proj_kernel, transposed=tuple(transposed)),
        grid=(n // ROW_TILE,), in_specs=in_specs, out_specs=out_specs,
        out_shape=out_shape, compiler_params=_cparams("parallel"), name="norm_proj",
    )(x, norm_w.reshape(1, d), *weights)


def _tail_kernel(x_ref, a_ref, b_ref, woa_ref, wob_ref, nw_ref, wg_ref, wu_ref, wd_ref, fn_ref,
                 o_ref, acc_ref, *, d_ff, final_norm):
    x1 = x_ref[...] + _dot(a_ref[...], woa_ref[...]) + _dot(b_ref[...], wob_ref[...])
    h = _rms(x1, nw_ref[...]).astype(BF16)
    acc_ref[...] = x1
    ffn = None
    for c in range(d_ff // FFN_CHUNK):
        cols = slice(c * FFN_CHUNK, (c + 1) * FFN_CHUNK)
        g = _dot(h, wg_ref[:, cols])
        u = _dot(h, wu_ref[:, cols])
        part = _dot((g * jax.nn.sigmoid(g) * u).astype(BF16), wd_ref[cols, :])
        ffn = part if ffn is None else ffn + part
    y = acc_ref[...] + ffn
    if final_norm:
        y = _rms(y, fn_ref[...])
    o_ref[...] = y


def _layer_tail(x, mix_a, mix_b, wo_a, wo_b, norm_w, wg, wu, wd, final_w, final_norm):
    n, d = x.shape
    d_ff = wg.shape[1]
    row = lambda w: pl.BlockSpec((ROW_TILE, w), lambda i: (i, 0))
    in_specs = [row(d), row(mix_a.shape[1]), row(mix_b.shape[1]), _resident(wo_a.shape),
                _resident(wo_b.shape), _resident((1, d)), _resident(wg.shape), _resident(wu.shape),
                _resident(wd.shape), _resident((1, d))]
    return pl.pallas_call(
        functools.partial(_tail_kernel, d_ff=d_ff, final_norm=final_norm),
        grid=(n // ROW_TILE,), in_specs=in_specs, out_specs=row(d),
        out_shape=jax.ShapeDtypeStruct((n, d), F32),
        scratch_shapes=[pltpu.VMEM((ROW_TILE, d), F32)],
        compiler_params=_cparams("parallel"), name="layer_tail",
    )(x, mix_a, mix_b, wo_a, wo_b, norm_w.reshape(1, d), wg, wu, wd, final_w.reshape(1, d))


def _band_kernel(q_ref, *refs):
    n_blk = BAND_WIN // BAND_TQ
    k_refs, vt_refs = refs[:n_blk], refs[n_blk:2 * n_blk]
    bias_ref, o_ref, s_ref, p_ref = refs[2 * n_blk:]
    i = pl.program_id(1)
    lane = lax.broadcasted_iota(jnp.int32, (BAND_TQ, LANES), 1)
    low_half = lane < A_HEAD_DIM
    top_half = lax.broadcasted_iota(jnp.int32, (LANES, BAND_WIN), 0) < A_HEAD_DIM
    pairs = A_W // LANES
    n_rb = BAND_WIN // FOX_ROWS
    fill = [jnp.where(i - (n_blk - 1) + j >= 0, 0.0, NEG_BIG) for j in range(n_blk)]
    one_at = lambda n: jnp.where(lane == n, 1.0, 0.0).astype(BF16)

    v_aug = []
    for pr in range(pairs):
        cols = slice(pr * LANES, (pr + 1) * LANES)
        q = q_ref[:, cols]
        k_blocks = [r[:, cols] for r in k_refs]
        k_even = jnp.concatenate([jnp.where(low_half, kb, f.astype(BF16)) for kb, f in zip(k_blocks, fill)], axis=0)
        k_odd = jnp.concatenate([jnp.where(low_half, f.astype(BF16), kb) for kb, f in zip(k_blocks, fill)], axis=0)
        s_ref[2 * pr] = _dot_nt(k_even, jnp.where(low_half, q, one_at(A_HEAD_DIM))) + bias_ref[2 * pr]
        s_ref[2 * pr + 1] = _dot_nt(k_odd, jnp.where(low_half, one_at(0), q)) + bias_ref[2 * pr + 1]
        vt = jnp.concatenate([r[cols, :] for r in vt_refs], axis=1)
        one = jnp.ones_like(vt)
        v_aug += [jnp.where(top_half, vt, one), jnp.where(top_half, one, vt)]

    rows = lambda rb: slice(rb * FOX_ROWS, (rb + 1) * FOX_ROWS)
    col_max = []
    for h in range(A_HEADS):
        m8 = None
        for rb in range(n_rb):
            s = s_ref[h, rows(rb), :]
            b8 = jnp.maximum(jnp.maximum(s[0:8], s[8:16]), jnp.maximum(s[16:24], s[24:32]))
            m8 = b8 if m8 is None else jnp.maximum(m8, b8)
        col_max.append(jnp.max(m8, axis=0, keepdims=True))
    for h in range(A_HEADS):
        for rb in range(n_rb):
            p_ref[h, rows(rb), :] = jnp.exp2(s_ref[h, rows(rb), :] - col_max[h]).astype(BF16)

    out_top = lax.broadcasted_iota(jnp.int32, (LANES, BAND_TQ), 0) < A_HEAD_DIM
    for pr in range(pairs):
        o_even = _dot(v_aug[2 * pr], p_ref[2 * pr])
        o_odd = _dot(v_aug[2 * pr + 1], p_ref[2 * pr + 1])
        out_t = jnp.where(out_top, o_even / o_even[A_HEAD_DIM:A_HEAD_DIM + 1, :], o_odd / o_odd[0:1, :])
        o_ref[:, pr * LANES:(pr + 1) * LANES] = out_t.T.astype(o_ref.dtype)


def _band_bias_tiles(rel_bias):
    r = np.arange(BAND_TQ)[:, None]
    c = np.arange(BAND_WIN)[None, :]
    offs = np.arange(-(BAND_TQ - 1), BAND_WIN)
    idx = np.clip(A_LEFT_CHUNKS * CHUNK - offs, -A_MAX_REL, A_MAX_REL) + A_MAX_REL
    period = offs.size + 1
    per_off = jnp.pad(rel_bias.astype(F32)[:, idx], ((0, 0), (0, 1)))
    skew = jnp.tile(per_off, (1, BAND_TQ))[:, :BAND_TQ * (period - 1)]
    skew = skew.reshape(-1, BAND_TQ, period - 1)[:, :, BAND_TQ - 1:BAND_TQ - 1 + BAND_WIN]
    qc, kc = r // CHUNK, c // CHUNK
    allowed = (kc >= qc) & (kc <= qc + A_LEFT_CHUNKS)
    return jnp.where(allowed[None], skew * LOG2_E, NEG_BIG).transpose(0, 2, 1)


def _band_attention(q, k, v_t, rel_bias, bsz, seq_len):
    n_blk = BAND_WIN // BAND_TQ
    nq = seq_len // BAND_TQ
    per_tile = ROW_TILE // BAND_TQ
    tiles_per_seq = seq_len // ROW_TILE
    q3, k3 = (a.reshape(bsz, seq_len, A_W) for a in (q, k))
    blk = (None, BAND_TQ, A_W)
    key_blk = lambda i, j: jnp.maximum(i - (n_blk - 1) + j, 0)
    k_specs = [pl.BlockSpec(blk, lambda b, i, j=j: (b, key_blk(i, j), 0)) for j in range(n_blk)]
    vt_specs = [pl.BlockSpec((None, A_W, BAND_TQ),
                             lambda b, i, j=j: (b * tiles_per_seq + key_blk(i, j) // per_tile, 0, key_blk(i, j) % per_tile))
                for j in range(n_blk)]
    out = pl.pallas_call(
        _band_kernel, grid=(bsz, nq),
        in_specs=[pl.BlockSpec(blk, lambda b, i: (b, i, 0))] + k_specs + vt_specs
        + [_resident((A_HEADS, BAND_WIN, BAND_TQ))],
        out_specs=pl.BlockSpec(blk, lambda b, i: (b, i, 0)),
        out_shape=jax.ShapeDtypeStruct((bsz, seq_len, A_W), BF16),
        scratch_shapes=[pltpu.VMEM((A_HEADS, BAND_WIN, BAND_TQ), F32),
                        pltpu.VMEM((A_HEADS, BAND_WIN, BAND_TQ), BF16)],
        compiler_params=_cparams("parallel", "parallel"), name="band_attention",
    )(q3, *([k3] * n_blk), *([v_t] * n_blk), _band_bias_tiles(rel_bias))
    return out.reshape(bsz * seq_len, A_W)


def _stage_conv_input(x_ref, pad_ref):
    rows = x_ref.shape[1]

    @pl.when(pl.program_id(0) == 0)
    def _():
        pad_ref[:, 0:SUBLANES, :] = jnp.zeros((pad_ref.shape[0], SUBLANES, pad_ref.shape[2]), F32)

    @pl.when(pl.program_id(0) > 0)
    def _():
        pad_ref[:, 0:SUBLANES, :] = pad_ref[:, rows:rows + SUBLANES, :]

    pad_ref[:, SUBLANES:, :] = x_ref[...]


def _conv_silu_rows(pad_ref, w_ref, b, row0, cols, bias_ref=None):
    window = pad_ref[b, pl.ds(row0, SUBLANES + CHUNK), cols]
    acc = None
    for back in range(CONV_K):
        term = w_ref[CONV_K - 1 - back:CONV_K - back, cols] * window[SUBLANES - back:SUBLANES - back + CHUNK]
        acc = term if acc is None else acc + term
    if bias_ref is not None:
        acc = acc + bias_ref[:, cols]
    return acc * jax.nn.sigmoid(acc)


def _gdn_kernel(qkv_ref, sm_ref, z_ref, cw_ref, par_ref, nw_ref, o_ref,
                pad_ref, state_ref, lhs_ref, add_ref, gl_ref):
    bsz, rows_per_step = qkv_ref.shape[0], qkv_ref.shape[1]
    n_chunks = rows_per_step // CHUNK
    heads = [(b, h) for b in range(bsz) for h in range(B_HEADS)]
    units = [(cc, b, h) for cc in range(GDN_BUILD_CHUNKS) for b, h in heads]

    @pl.when(pl.program_id(0) == 0)
    def _():
        state_ref[...] = jnp.zeros_like(state_ref)

    _stage_conv_input(qkv_ref, pad_ref)
    causal, strict = _tril_masks(CHUNK)
    tril = jnp.where(causal, 1.0, 0.0).astype(BF16)
    eye = jnp.where(causal & ~strict, 1.0, 0.0)
    a_row = -jnp.exp(par_ref[0:1, :])
    dt_bias_row = par_ref[1:2, :]

    def head_cols(part, h):
        return slice(part * B_W + h * B_HEAD_DIM, part * B_W + (h + 1) * B_HEAD_DIM)

    def build(cp, carry):
        row0 = [pl.multiple_of((cp * GDN_BUILD_CHUNKS + cc) * CHUNK, CHUNK) for cc in range(GDN_BUILD_CHUNKS)]
        sm = {(cc, b): sm_ref[b, pl.ds(row0[cc], CHUNK), :] for cc in range(GDN_BUILD_CHUNKS) for b in range(bsz)}
        beta_all = {key: jax.nn.sigmoid(x) for key, x in sm.items()}
        gc_all = {key: _cumsum_rows(tril, a_row * jax.nn.softplus(x + dt_bias_row)) for key, x in sm.items()}
        gc_all_t = {key: x.T for key, x in gc_all.items()}
        q, k, v = ([_conv_silu_rows(pad_ref, cw_ref, b, row0[cc], head_cols(part, h)) for cc, b, h in units]
                   for part in range(3))
        q = [x * lax.rsqrt(jnp.sum(x * x, axis=-1, keepdims=True) + EPS) * (B_HEAD_DIM ** -0.5) for x in q]
        k = [x * lax.rsqrt(jnp.sum(x * x, axis=-1, keepdims=True) + EPS) for x in k]
        beta = [beta_all[cc, b][:, h:h + 1] for cc, b, h in units]
        gc = [gc_all[cc, b][:, B_HEADS + h:B_HEADS + h + 1] for cc, b, h in units]
        gc_row = [gc_all_t[cc, b][B_HEADS + h:B_HEADS + h + 1, :] for cc, b, h in units]
        decay = [jnp.exp(jnp.where(causal, g - gr, NEG_BIG)) for g, gr in zip(gc, gc_row)]
        kb = [x.astype(BF16) for x in k]
        kk = [_dot_nt(x, x) for x in kb]
        qk = [_dot_nt(x.astype(BF16), y) for x, y in zip(q, kb)]
        a_mat = [jnp.where(strict, bt * x * d, 0.0) for bt, x, d in zip(beta, kk, decay)]
        inv = [eye - a for a in a_mat]
        power = a_mat
        for _ in range(5):
            power = [_dot(x, x) for x in power]
            inv = [i + _dot(i, x) for i, x in zip(inv, power)]
        exp_gc = [jnp.exp(g) for g in gc]
        rhs = [jnp.concatenate([ki * (bt * e), vi * bt], axis=1) for ki, vi, bt, e in zip(k, v, beta, exp_gc)]
        wu = [_dot(i, r).astype(BF16) for i, r in zip(inv, rhs)]
        attn = [(x * d).astype(BF16) for x, d in zip(qk, decay)]
        gc_last = [g[CHUNK - 1:CHUNK, :] for g in gc]
        k_st = [(ki * jnp.exp(gl - g)).astype(BF16) for ki, gl, g in zip(k, gc_last, gc)]
        top = [_dot_tn(x, y) for x, y in zip(k_st, wu)]
        bot = [_dot(x, y) for x, y in zip(attn, wu)]
        for i in range(len(units)):
            slot = cp * len(units) + i
            lhs_ref[slot, 0:B_HEAD_DIM, :] = (-top[i][:, :B_HEAD_DIM]).astype(BF16)
            lhs_ref[slot, B_HEAD_DIM:, :] = (q[i] * exp_gc[i] - bot[i][:, :B_HEAD_DIM]).astype(BF16)
            add_ref[slot, 0:B_HEAD_DIM, :] = top[i][:, B_HEAD_DIM:]
            add_ref[slot, B_HEAD_DIM:, :] = bot[i][:, B_HEAD_DIM:]
            gl_ref[slot] = jnp.broadcast_to(jnp.exp(gc_last[i]), (SUBLANES, LANES))
        return carry

    lax.fori_loop(0, n_chunks // GDN_BUILD_CHUNKS, build, 0)

    def scan(c, carry):
        rows = pl.ds(pl.multiple_of(c * CHUNK, CHUNK), CHUNK)
        states = [state_ref[i] for i in range(len(heads))]
        res = [_dot(lhs_ref[c * len(heads) + i], states[i].astype(BF16)) + add_ref[c * len(heads) + i]
               for i in range(len(heads))]
        for i, (b, h) in enumerate(heads):
            state_ref[i] = gl_ref[c * len(heads) + i][0:1, :] * states[i] + res[i][:B_HEAD_DIM]
            zz = z_ref[b, rows, head_cols(0, h)]
            y = _rms(res[i][B_HEAD_DIM:], nw_ref[...]) * (zz * jax.nn.sigmoid(zz))
            o_ref[b, rows, head_cols(0, h)] = y.astype(o_ref.dtype)
        return carry

    lax.fori_loop(0, n_chunks, scan, 0)


def _gated_delta(qkv_raw, small, z, conv_w, a_log, dt_bias, norm_w, bsz, seq_len):
    par = jnp.zeros((2, LANES), F32)
    par = par.at[0, B_HEADS:2 * B_HEADS].set(a_log.astype(F32))
    par = par.at[1, B_HEADS:2 * B_HEADS].set(dt_bias.astype(F32))
    blk = lambda w: pl.BlockSpec((bsz, SCAN_ROWS, w), lambda t: (0, t, 0))
    n_units = bsz * B_HEADS
    n_slots = n_units * (SCAN_ROWS // CHUNK)
    out = pl.pallas_call(
        _gdn_kernel, grid=(seq_len // SCAN_ROWS,),
        in_specs=[blk(3 * B_W), blk(LANES), blk(B_W), _resident((CONV_K, 3 * B_W)), _resident((2, LANES)),
                  _resident((1, B_HEAD_DIM))],
        out_specs=blk(B_W),
        out_shape=jax.ShapeDtypeStruct((bsz, seq_len, B_W), BF16),
        scratch_shapes=[pltpu.VMEM((bsz, SUBLANES + SCAN_ROWS, 3 * B_W), F32),
                        pltpu.VMEM((n_units, B_HEAD_DIM, B_HEAD_DIM), F32),
                        pltpu.VMEM((n_slots, B_HEAD_DIM + CHUNK, B_HEAD_DIM), BF16),
                        pltpu.VMEM((n_slots, B_HEAD_DIM + CHUNK, B_HEAD_DIM), F32),
                        pltpu.VMEM((n_slots, SUBLANES, LANES), F32)],
        compiler_params=_cparams("arbitrary"), name="gated_delta",
    )(qkv_raw.reshape(bsz, seq_len, 3 * B_W), small.reshape(bsz, seq_len, LANES),
      z.reshape(bsz, seq_len, B_W), conv_w.astype(F32), par, norm_w.reshape(1, B_HEAD_DIM).astype(F32))
    return out.reshape(bsz * seq_len, B_W)


def _ssd_kernel(xbc_ref, sm_ref, z_ref, cw_ref, cb_ref, par_ref, skip_ref, nw_ref, o_ref, pad_ref, state_ref):
    bsz, rows_per_step = xbc_ref.shape[0], xbc_ref.shape[1]
    group_w = C_W // C_GROUPS
    heads_per_group = C_HEADS // C_GROUPS
    pairs_per_group = heads_per_group // 2

    @pl.when(pl.program_id(0) == 0)
    def _():
        state_ref[...] = jnp.zeros_like(state_ref)

    _stage_conv_input(xbc_ref, pad_ref)
    causal, _ = _tril_masks(CHUNK)
    tril = jnp.where(causal, 1.0, 0.0).astype(BF16)
    a_row = -jnp.exp(par_ref[0:1, :])
    dt_bias_row = par_ref[1:2, :]
    lane = lax.broadcasted_iota(jnp.int32, (CHUNK, LANES), 1)
    low_half = lane < C_HEAD_DIM

    def per_head_lanes(cols, h0):
        return jnp.where(low_half, cols[:, h0:h0 + 1], cols[:, h0 + 1:h0 + 2])

    def pair_cols(pr):
        return slice(pr * LANES, (pr + 1) * LANES)

    def chunk_pair(cp, carry):
        units = [(cc, b) for cc in range(SSD_CHUNKS) for b in range(bsz)]
        groups = [(u, g) for u in units for g in range(C_GROUPS)]
        row0 = [pl.multiple_of((cp * SSD_CHUNKS + cc) * CHUNK, CHUNK) for cc in range(SSD_CHUNKS)]
        conv = lambda u, cols: _conv_silu_rows(pad_ref, cw_ref, u[1], row0[u[0]], cols, cb_ref)
        sm = {u: sm_ref[u[1], pl.ds(row0[u[0]], CHUNK), :] for u in units}
        dt_all = {u: jax.nn.softplus(sm[u] + dt_bias_row) for u in units}
        da_cs = {u: _cumsum_rows(tril, dt_all[u] * a_row) for u in units}
        da_cs_t = {u: da_cs[u].T for u in units}
        dt_t = {u: dt_all[u].T for u in units}
        da_last = {u: da_cs[u][CHUNK - 1:CHUNK, :] for u in units}
        exp_da = {u: jnp.exp(da_cs[u]) for u in units}
        to_end = {u: jnp.exp(da_last[u] - da_cs[u]) * dt_all[u] for u in units}
        chunk_decay = {u: jnp.exp(da_last[u]) for u in units}
        x2 = {(u, pr): conv(u, pair_cols(pr)) for u in units for pr in range(C_HEADS // 2)}
        bm = {(u, g): conv(u, slice(C_W + g * C_STATE, C_W + (g + 1) * C_STATE)).astype(BF16) for u, g in groups}
        cm = {(u, g): conv(u, slice(C_W + C_BC + g * C_STATE, C_W + C_BC + (g + 1) * C_STATE)).astype(BF16)
              for u, g in groups}
        cb = {key: _dot_nt(cm[key], bm[key]) for key in groups}
        xw = {(u, g): jnp.concatenate(
            [(x2[u, g * pairs_per_group + j] * per_head_lanes(to_end[u], g * heads_per_group + 2 * j)).astype(BF16)
             for j in range(pairs_per_group)], axis=1) for u, g in groups}
        inflow = {key: _dot_tn(bm[key], xw[key]) for key in groups}
        y_diag = {}
        for u in units:
            for h in range(C_HEADS):
                seg = da_cs[u][:, h:h + 1] - da_cs_t[u][h:h + 1, :]
                wts = cb[u, h // heads_per_group] * jnp.exp(jnp.where(causal, seg, NEG_BIG)) * dt_t[u][h:h + 1, :]
                y_diag[u, h] = _dot(wts.astype(BF16), x2[u, h // 2].astype(BF16))
        for u in units:
            cc, b = u
            rows = pl.ds(row0[cc], CHUNK)
            for g in range(C_GROUPS):
                gl = slice(g * group_w, (g + 1) * group_w)
                prev = state_ref[b, :, gl]
                y_off = _dot(cm[u, g], prev.astype(BF16))
                dec = jnp.concatenate(
                    [jnp.where(low_half[0:1], chunk_decay[u][:, h0:h0 + 1], chunk_decay[u][:, h0 + 1:h0 + 2])
                     for h0 in range(g * heads_per_group, (g + 1) * heads_per_group, 2)], axis=1)
                state_ref[b, :, gl] = prev * dec + inflow[u, g]
                ys = []
                for j in range(pairs_per_group):
                    pr = g * pairs_per_group + j
                    h0 = 2 * pr
                    y = jnp.where(low_half, y_diag[u, h0], y_diag[u, h0 + 1])
                    y = y + y_off[:, pair_cols(j)] * per_head_lanes(exp_da[u], h0) + skip_ref[:, pair_cols(pr)] * x2[u, pr]
                    zz = z_ref[b, rows, pair_cols(pr)]
                    ys.append(y * (zz * jax.nn.sigmoid(zz)))
                yg = jnp.concatenate(ys, axis=1)
                o_ref[b, rows, gl] = _rms(yg, nw_ref[:, gl]).astype(o_ref.dtype)
        return carry

    lax.fori_loop(0, rows_per_step // (CHUNK * SSD_CHUNKS), chunk_pair, 0)


def _ssd(xbc_raw, small, z, conv_w, conv_b, dt_bias, a_log, d_skip, norm_w, bsz, seq_len):
    par = jnp.zeros((2, LANES), F32)
    par = par.at[0, :C_HEADS].set(a_log.astype(F32))
    par = par.at[1, :C_HEADS].set(dt_bias.astype(F32))
    skip = jnp.repeat(d_skip.astype(F32), C_HEAD_DIM).reshape(1, C_W)
    width = C_W + 2 * C_BC
    blk = lambda w: pl.BlockSpec((bsz, SCAN_ROWS, w), lambda t: (0, t, 0))
    out = pl.pallas_call(
        _ssd_kernel, grid=(seq_len // SCAN_ROWS,),
        in_specs=[blk(width), blk(LANES), blk(C_W), _resident((CONV_K, width)), _resident((1, width)),
                  _resident((2, LANES)), _resident((1, C_W)), _resident((1, C_W))],
        out_specs=blk(C_W),
        out_shape=jax.ShapeDtypeStruct((bsz, seq_len, C_W), BF16),
        scratch_shapes=[pltpu.VMEM((bsz, SUBLANES + SCAN_ROWS, width), F32),
                        pltpu.VMEM((bsz, C_STATE, C_W), F32)],
        compiler_params=_cparams("arbitrary"), name="ssd",
    )(xbc_raw.reshape(bsz, seq_len, width), small.reshape(bsz, seq_len, LANES), z.reshape(bsz, seq_len, C_W),
      conv_w.astype(F32), conv_b.astype(F32).reshape(1, width), par, skip, norm_w.reshape(1, C_W).astype(F32))
    return out.reshape(bsz * seq_len, C_W)


FOX_F_PIECES = 3


def _fox_f_lane(h):
    return (h // 2) * LANES + (D_HEAD_DIM if h % 2 == 0 else 0)


def _fox_prep_kernel(sm_ref, fb_ref, place_ref, fk_ref, carry_ref):
    tm = sm_ref.shape[0]

    @pl.when(pl.program_id(1) == 0)
    def _():
        carry_ref[...] = jnp.zeros_like(carry_ref)

    causal, _ = _tril_masks(tm)
    tril = jnp.where(causal, 1.0, 0.0).astype(BF16)
    log_f = jax.nn.log_sigmoid(sm_ref[...] + fb_ref[...])
    f_cum = _cumsum_rows(tril, log_f) + carry_ref[...]
    carry_ref[...] = f_cum[tm - 1:tm, :]
    pieces = jnp.concatenate(_split3(-LOG2_E * f_cum), axis=1)
    fk_ref[...] = _dot(pieces, place_ref[...]).astype(fk_ref.dtype)


def _fox_prep(small, f_bias, bsz, seq_len):
    fb = jnp.zeros((1, LANES), F32).at[0, D_HEADS:2 * D_HEADS].set(f_bias.astype(F32))
    place = np.zeros((FOX_F_PIECES * LANES, D_W), np.float32)
    for h in range(D_HEADS):
        for piece in range(FOX_F_PIECES):
            place[piece * LANES + D_HEADS + h, _fox_f_lane(h) + piece] = 1.0
    nt = seq_len // FOX_T
    return pl.pallas_call(
        _fox_prep_kernel, grid=(bsz, nt),
        in_specs=[pl.BlockSpec((None, FOX_T, LANES), lambda b, t: (b, t, 0)), _resident((1, LANES)),
                  _resident(place.shape)],
        out_specs=pl.BlockSpec((None, FOX_T, D_W), lambda b, t: (b, t, 0)),
        out_shape=jax.ShapeDtypeStruct((bsz, seq_len, D_W), BF16),
        scratch_shapes=[pltpu.VMEM((1, LANES), F32)],
        compiler_params=_cparams("parallel", "arbitrary"), name="fox_prep",
    )(small.reshape(bsz, seq_len, LANES), fb, jnp.asarray(place, BF16))


def _fox_kernel(q_ref, k_ref, vt_ref, fk_ref, o_ref,
                kk_ref, vv_ref, sa_ref, sb_ref, pa_ref, pb_ref, aa_ref, ab_ref, m_ref, acc_ref):
    i = pl.program_id(2)
    t = FOX_T
    nt = k_ref.shape[0] // t
    n_rb = t // FOX_ROWS
    lane = lax.broadcasted_iota(jnp.int32, (t, LANES), 1)
    low_half = lane < D_HEAD_DIM
    top_half = lax.broadcasted_iota(jnp.int32, (LANES, t), 0) < D_HEAD_DIM

    @pl.when(i == 0)
    def _():
        for jt in range(nt):
            rows = slice(jt * t, (jt + 1) * t)
            kt, ft = k_ref[rows, :], fk_ref[rows, :]
            kk_ref[0, jt] = jnp.where(low_half, kt, ft)
            kk_ref[1, jt] = jnp.where(low_half, ft, kt)
            vt = vt_ref[jt]
            one = jnp.ones_like(vt)
            vv_ref[0, jt] = jnp.where(top_half, vt, one)
            vv_ref[1, jt] = jnp.where(top_half, one, vt)
        kk_ref[0, nt] = jnp.where(low_half, 0.0, NEG_BIG).astype(BF16)
        kk_ref[1, nt] = jnp.where(low_half, NEG_BIG, 0.0).astype(BF16)

    q = q_ref[...]
    ones_upto = lambda n: jnp.where(lane < n, 1.0, 0.0).astype(BF16)
    qs = (jnp.where(low_half, q, ones_upto(_fox_f_lane(0) + FOX_F_PIECES)),
          jnp.where(low_half, ones_upto(_fox_f_lane(1) + FOX_F_PIECES), q))
    key_in_blk = lax.broadcasted_iota(jnp.int32, (FOX_ROWS, t), 0)
    query = lax.broadcasted_iota(jnp.int32, (FOX_ROWS, t), 1)

    m_ref[...] = jnp.full(m_ref.shape, NEG_BIG, F32)
    acc_ref[...] = jnp.zeros_like(acc_ref)

    def scores(j, s_ref):
        jj = jnp.where(j >= 0, j, nt)
        for half in range(2):
            s_ref[half] = _dot_nt(kk_ref[half, jj], qs[half])

    def softmax_cols(s_ref, p_ref, a_ref, diagonal):
        def block(half, rb):
            s = s_ref[half, rb * FOX_ROWS:(rb + 1) * FOX_ROWS, :]
            if diagonal:
                s = jnp.where(key_in_blk + rb * FOX_ROWS <= query, s, NEG_BIG)
            return s

        tile_max = []
        for half in range(2):
            m8 = None
            for rb in range(n_rb):
                s = block(half, rb)
                b8 = jnp.maximum(jnp.maximum(s[0:8], s[8:16]), jnp.maximum(s[16:24], s[24:32]))
                m8 = b8 if m8 is None else jnp.maximum(m8, b8)
            tile_max.append(jnp.max(m8, axis=0, keepdims=True))
        for half in range(2):
            m_old = m_ref[half]
            m_new = jnp.maximum(m_old, tile_max[half])
            m_ref[half] = m_new
            a_ref[half] = jnp.exp2(m_old - m_new)
            for rb in range(n_rb):
                p_ref[half, rb * FOX_ROWS:(rb + 1) * FOX_ROWS, :] = jnp.exp2(block(half, rb) - m_new).astype(BF16)

    def accumulate(j, p_ref, a_ref):
        jj = jnp.maximum(j, 0)
        for half in range(2):
            acc_ref[half] = a_ref[half] * acc_ref[half] + _dot(vv_ref[half, jj], p_ref[half])

    scores(i, sb_ref)
    scores(i - 1, sa_ref)
    softmax_cols(sb_ref, pb_ref, ab_ref, True)

    def pair(n, carry):
        j = i - 1 - 2 * n
        scores(j - 1, sb_ref)
        softmax_cols(sa_ref, pa_ref, aa_ref, False)
        accumulate(j + 1, pb_ref, ab_ref)
        scores(j - 2, sa_ref)
        softmax_cols(sb_ref, pb_ref, ab_ref, False)
        accumulate(j, pa_ref, aa_ref)
        return carry

    n_pairs = (i + 1) // 2
    lax.fori_loop(0, n_pairs, pair, 0)
    accumulate(i - 2 * n_pairs, pb_ref, ab_ref)
    acc0, acc1 = acc_ref[0], acc_ref[1]
    out_t = jnp.where(top_half, acc0 / acc0[D_HEAD_DIM:D_HEAD_DIM + 1, :], acc1 / acc1[0:1, :])
    o_ref[...] = out_t.T.astype(o_ref.dtype)


def _fox_attention(q, k, v_t, fk, bsz, seq_len):
    nt = seq_len // FOX_T
    pairs = D_W // LANES
    q3, k3 = (a.reshape(bsz, seq_len, D_W) for a in (q, k))
    whole_seq = pl.BlockSpec((None, seq_len, LANES), lambda b, p, i: (b, 0, p))
    out = pl.pallas_call(
        _fox_kernel, grid=(bsz, pairs, nt),
        in_specs=[pl.BlockSpec((None, FOX_T, LANES), lambda b, p, i: (b, i, p)), whole_seq,
                  pl.BlockSpec((nt, LANES, FOX_T), lambda b, p, i: (b, p, 0)), whole_seq],
        out_specs=pl.BlockSpec((None, FOX_T, LANES), lambda b, p, i: (b, i, p)),
        out_shape=jax.ShapeDtypeStruct((bsz, seq_len, D_W), BF16),
        scratch_shapes=[pltpu.VMEM((2, nt + 1, FOX_T, LANES), BF16), pltpu.VMEM((2, nt, LANES, FOX_T), BF16)]
        + [pltpu.VMEM((2, FOX_T, FOX_T), F32)] * 2 + [pltpu.VMEM((2, FOX_T, FOX_T), BF16)] * 2
        + [pltpu.VMEM((2, 1, FOX_T), F32)] * 3 + [pltpu.VMEM((2, LANES, FOX_T), F32)],
        compiler_params=_cparams("parallel", "parallel", "arbitrary"), name="fox_attention",
    )(q3, k3, v_t, fk)
    return out.reshape(bsz * seq_len, D_W)


def _pad_cols(w, width=LANES):
    return jnp.pad(w, ((0, 0), (0, width - w.shape[1])))


def kernel(x, norm_mix, norm_ffn, norm_final, ffn_w_gate, ffn_w_up, ffn_w_down, ab_w_in, ab_rel_bias, ab_conv_w, ab_a_log, ab_dt_bias, ab_norm_w, ab_w_out, cd_w_in, cd_conv_w, cd_conv_b, cd_dt_bias, cd_a_log, cd_d_skip, cd_norm_w, cd_f_bias, cd_w_out):
    bsz, seq_len, d = x.shape
    n = bsz * seq_len
    xf = x.reshape(n, d)
    bf = lambda w: w.astype(BF16)

    w_in = ab_w_in[0]
    o = np.cumsum([0, A_W, A_W, A_W, 3 * B_W, B_HEADS, B_HEADS, B_W])
    weights = [bf(w_in[:, o[0]:o[1]] * (LOG2_E * A_HEAD_DIM ** -0.5)), bf(w_in[:, o[1]:o[2]]), bf(w_in[:, o[2]:o[3]].T),
               bf(w_in[:, o[3]:o[4]]), bf(_pad_cols(w_in[:, o[4]:o[6]])), bf(w_in[:, o[6]:o[7]])]
    a_q, a_k, a_vt, b_qkv, b_small, b_z = _norm_proj(
        xf, norm_mix[0], weights, [BF16, BF16, BF16, F32, F32, F32], transposed=(False, False, True) + (False,) * 3)
    o_a = _band_attention(a_q, a_k, a_vt, ab_rel_bias[0], bsz, seq_len)
    o_b = _gated_delta(b_qkv, b_small, b_z, ab_conv_w[0], ab_a_log[0], ab_dt_bias[0], ab_norm_w[0], bsz, seq_len)
    w_out = ab_w_out[0]
    xf = _layer_tail(xf, o_a, o_b, bf(w_out[:A_W]), bf(w_out[A_W:]), norm_ffn[0],
                     bf(ffn_w_gate[0]), bf(ffn_w_up[0]), bf(ffn_w_down[0]), norm_final, False)

    w_in = cd_w_in[0]
    o = np.cumsum([0, C_W, C_W + 2 * C_BC, C_HEADS, D_W, D_W, D_W, D_HEADS])
    small_w = jnp.concatenate([w_in[:, o[2]:o[3]], w_in[:, o[6]:o[7]]], axis=1)
    weights = [bf(w_in[:, o[0]:o[1]]), bf(w_in[:, o[1]:o[2]]), bf(_pad_cols(small_w)),
               bf(w_in[:, o[3]:o[4]] * (LOG2_E * D_HEAD_DIM ** -0.5)), bf(w_in[:, o[4]:o[5]]), bf(w_in[:, o[5]:o[6]].T)]
    c_z, c_xbc, cd_small, d_q, d_k, d_vt = _norm_proj(
        xf, norm_mix[1], weights, [F32, F32, F32, BF16, BF16, BF16], transposed=(False,) * 5 + (True,))
    y_c = _ssd(c_xbc, cd_small, c_z, cd_conv_w[0], cd_conv_b[0], cd_dt_bias[0], cd_a_log[0], cd_d_skip[0],
               cd_norm_w[0], bsz, seq_len)
    fk = _fox_prep(cd_small, cd_f_bias[0], bsz, seq_len)
    o_d = _fox_attention(d_q, d_k, d_vt, fk, bsz, seq_len)
    w_out = cd_w_out[0]
    xf = _layer_tail(xf, y_c, o_d, bf(w_out[:C_W]), bf(w_out[C_W:]), norm_ffn[1],
                     bf(ffn_w_gate[1]), bf(ffn_w_up[1]), bf(ffn_w_down[1]), norm_final, True)
    return xf.reshape(bsz, seq_len, d)
```

```python
import functools

import jax
import jax.numpy as jnp
import numpy as np
from jax import lax
from jax.experimental import pallas as pl
from jax.experimental.pallas import tpu as pltpu

F32 = jnp.float32
BF16 = jnp.bfloat16

D_MODEL = 1024
CHUNK = 64
EPS = 1e-6
CONV_K = 4
A_HEADS, A_HEAD_DIM, A_LEFT_CHUNKS, A_MAX_REL = 8, 64, 8, 256
B_HEADS, B_HEAD_DIM = 4, 128
C_HEADS, C_HEAD_DIM, C_GROUPS, C_STATE = 8, 64, 2, 128
D_HEADS, D_HEAD_DIM = 8, 64
A_W = A_HEADS * A_HEAD_DIM
B_W = B_HEADS * B_HEAD_DIM
C_W = C_HEADS * C_HEAD_DIM
D_W = D_HEADS * D_HEAD_DIM
C_BC = C_GROUPS * C_STATE

LANES = 128
SUBLANES = 8
VMEM_LIMIT_BYTES = 56 * 1024 * 1024
NEG_BIG = -1e30
LOG2_E = float(np.log2(np.e))

ROW_TILE = 512
FFN_CHUNK = 256
BAND_TQ = 256
BAND_WIN = BAND_TQ + A_LEFT_CHUNKS * CHUNK
FOX_T = 512
FOX_ROWS = 32
SCAN_ROWS = 512
SSD_CHUNKS = 4
GDN_BUILD_CHUNKS = 4


def _cparams(*sem):
    return pltpu.CompilerParams(dimension_semantics=sem, vmem_limit_bytes=VMEM_LIMIT_BYTES)


def _resident(shape):
    nd = len(shape)
    return pl.BlockSpec(shape, lambda *_: (0,) * nd, pipeline_mode=pl.Buffered(1))


def _dot(a, b):
    return jnp.dot(a, b, preferred_element_type=F32)


def _dot_nt(a, b):
    return lax.dot_general(a, b, (((1,), (1,)), ((), ())), preferred_element_type=F32)


def _dot_tn(a, b):
    return lax.dot_general(a, b, (((0,), (0,)), ((), ())), preferred_element_type=F32)


def _rms(x, w):
    return x * lax.rsqrt(jnp.mean(x * x, axis=-1, keepdims=True) + EPS) * w


def _split3(x):
    hi = x.astype(BF16)
    r1 = x - hi.astype(F32)
    mid = r1.astype(BF16)
    lo = (r1 - mid.astype(F32)).astype(BF16)
    return hi, mid, lo


def _cumsum_rows(tril, x):
    hi, mid, lo = _split3(x)
    return _dot(tril, hi) + _dot(tril, mid) + _dot(tril, lo)


def _tril_masks(n):
    r = lax.broadcasted_iota(jnp.int32, (n, n), 0)
    c = lax.broadcasted_iota(jnp.int32, (n, n), 1)
    return r >= c, r > c


def _norm_proj_kernel(x_ref, nw_ref, *refs, transposed):
    n_out = len(refs) // 2
    h = _rms(x_ref[...], nw_ref[...]).astype(BF16)
    for w_ref, o_ref, tr in zip(refs[:n_out], refs[n_out:], transposed):
        out = _dot_nt(w_ref[...], h) if tr else _dot(h, w_ref[...])
        o_ref[...] = out.astype(o_ref.dtype)


def _norm_proj(x, norm_w, weights, out_dtypes, transposed=None):
    n, d = x.shape
    transposed = transposed or (False,) * len(weights)
    in_specs = [pl.BlockSpec((ROW_TILE, d), lambda i: (i, 0)), _resident((1, d))]
    in_specs += [_resident(w.shape) for w in weights]
    out_shape, out_specs = [], []
    for w, dt, tr in zip(weights, out_dtypes, transposed):
        if tr:
            out_shape.append(jax.ShapeDtypeStruct((n // ROW_TILE, w.shape[0], ROW_TILE), dt))
            out_specs.append(pl.BlockSpec((None, w.shape[0], ROW_TILE), lambda i: (i, 0, 0)))
        else:
            out_shape.append(jax.ShapeDtypeStruct((n, w.shape[1]), dt))
            out_specs.append(pl.BlockSpec((ROW_TILE, w.shape[1]), lambda i: (i, 0)))
    return pl.pallas_call(
        functools.partial(_norm_proj_kernel, transposed=tuple(transposed)),
        grid=(n // ROW_TILE,), in_specs=in_specs, out_specs=out_specs,
        out_shape=out_shape, compiler_params=_cparams("parallel"), name="norm_proj",
    )(x, norm_w.reshape(1, d), *weights)


def _tail_kernel(x_ref, a_ref, b_ref, woa_ref, wob_ref, nw_ref, wg_ref, wu_ref, wd_ref, fn_ref,
                 o_ref, acc_ref, *, d_ff, final_norm):
    x1 = x_ref[...] + _dot(a_ref[...], woa_ref[...]) + _dot(b_ref[...], wob_ref[...])
    h = _rms(x1, nw_ref[...]).astype(BF16)
    acc_ref[...] = x1
    ffn = None
    for c in range(d_ff // FFN_CHUNK):
        cols = slice(c * FFN_CHUNK, (c + 1) * FFN_CHUNK)
        g = _dot(h, wg_ref[:, cols])
        u = _dot(h, wu_ref[:, cols])
        part = _dot((g * jax.nn.sigmoid(g) * u).astype(BF16), wd_ref[cols, :])
        ffn = part if ffn is None else ffn + part
    y = acc_ref[...] + ffn
    if final_norm:
        y = _rms(y, fn_ref[...])
    o_ref[...] = y


def _layer_tail(x, mix_a, mix_b, w_out, norm_w, wg, wu, wd, layer, final_w, final_norm):
    n, d = x.shape
    d_ff = wg.shape[2]
    wa, wb = mix_a.shape[1], mix_b.shape[1]
    assert wa == wb and w_out.shape == (1, wa + wb, d)
    row = lambda w: pl.BlockSpec((ROW_TILE, w), lambda i: (i, 0))
    pick = lambda shape, idx: pl.BlockSpec((None,) + shape, lambda i: idx, pipeline_mode=pl.Buffered(1))
    in_specs = [row(d), row(wa), row(wb), pick((wa, d), (0, 0, 0)), pick((wb, d), (0, 1, 0)), _resident((1, d)),
                pick((d, d_ff), (layer, 0, 0)), pick((d, d_ff), (layer, 0, 0)), pick((d_ff, d), (layer, 0, 0)),
                _resident((1, d))]
    return pl.pallas_call(
        functools.partial(_tail_kernel, d_ff=d_ff, final_norm=final_norm),
        grid=(n // ROW_TILE,), in_specs=in_specs, out_specs=row(d),
        out_shape=jax.ShapeDtypeStruct((n, d), F32),
        scratch_shapes=[pltpu.VMEM((ROW_TILE, d), F32)],
        compiler_params=_cparams("parallel"), name="layer_tail",
    )(x, mix_a, mix_b, w_out, w_out, norm_w.reshape(1, d), wg, wu, wd, final_w.reshape(1, d))


def _band_kernel(q_ref, *refs):
    n_blk = BAND_WIN // BAND_TQ
    k_refs, vt_refs = refs[:n_blk], refs[n_blk:2 * n_blk]
    bias_ref, o_ref, s_ref, p_ref = refs[2 * n_blk:]
    i = pl.program_id(1)
    lane = lax.broadcasted_iota(jnp.int32, (BAND_TQ, LANES), 1)
    low_half = lane < A_HEAD_DIM
    top_half = lax.broadcasted_iota(jnp.int32, (LANES, BAND_WIN), 0) < A_HEAD_DIM
    pairs = A_W // LANES
    n_rb = BAND_WIN // FOX_ROWS
    fill = [jnp.where(i - (n_blk - 1) + j >= 0, 0.0, NEG_BIG) for j in range(n_blk)]
    one_at = lambda n: jnp.where(lane == n, 1.0, 0.0).astype(BF16)

    v_aug = []
    for pr in range(pairs):
        cols = slice(pr * LANES, (pr + 1) * LANES)
        q = q_ref[:, cols]
        k_blocks = [r[:, cols] for r in k_refs]
        k_even = jnp.concatenate([jnp.where(low_half, kb, f.astype(BF16)) for kb, f in zip(k_blocks, fill)], axis=0)
        k_odd = jnp.concatenate([jnp.where(low_half, f.astype(BF16), kb) for kb, f in zip(k_blocks, fill)], axis=0)
        s_ref[2 * pr] = _dot_nt(k_even, jnp.where(low_half, q, one_at(A_HEAD_DIM))) + bias_ref[2 * pr]
        s_ref[2 * pr + 1] = _dot_nt(k_odd, jnp.where(low_half, one_at(0), q)) + bias_ref[2 * pr + 1]
        vt = jnp.concatenate([r[cols, :] for r in vt_refs], axis=1)
        one = jnp.ones_like(vt)
        v_aug += [jnp.where(top_half, vt, one), jnp.where(top_half, one, vt)]

    rows = lambda rb: slice(rb * FOX_ROWS, (rb + 1) * FOX_ROWS)
    col_max = []
    for h in range(A_HEADS):
        m8 = None
        for rb in range(n_rb):
            s = s_ref[h, rows(rb), :]
            b8 = jnp.maximum(jnp.maximum(s[0:8], s[8:16]), jnp.maximum(s[16:24], s[24:32]))
            m8 = b8 if m8 is None else jnp.maximum(m8, b8)
        col_max.append(jnp.max(m8, axis=0, keepdims=True))
    for h in range(A_HEADS):
        for rb in range(n_rb):
            p_ref[h, rows(rb), :] = jnp.exp2(s_ref[h, rows(rb), :] - col_max[h]).astype(BF16)

    out_top = lax.broadcasted_iota(jnp.int32, (LANES, BAND_TQ), 0) < A_HEAD_DIM
    for pr in range(pairs):
        o_even = _dot(v_aug[2 * pr], p_ref[2 * pr])
        o_odd = _dot(v_aug[2 * pr + 1], p_ref[2 * pr + 1])
        out_t = jnp.where(out_top, o_even / o_even[A_HEAD_DIM:A_HEAD_DIM + 1, :], o_odd / o_odd[0:1, :])
        o_ref[:, pr * LANES:(pr + 1) * LANES] = out_t.T.astype(o_ref.dtype)


def _band_bias_tiles(rel_bias):
    r = np.arange(BAND_TQ)[:, None]
    c = np.arange(BAND_WIN)[None, :]
    offs = np.arange(-(BAND_TQ - 1), BAND_WIN)
    idx = np.clip(A_LEFT_CHUNKS * CHUNK - offs, -A_MAX_REL, A_MAX_REL) + A_MAX_REL
    period = offs.size + 1
    per_off = jnp.pad(rel_bias.astype(F32)[:, idx], ((0, 0), (0, 1)))
    skew = jnp.tile(per_off, (1, BAND_TQ))[:, :BAND_TQ * (period - 1)]
    skew = skew.reshape(-1, BAND_TQ, period - 1)[:, :, BAND_TQ - 1:BAND_TQ - 1 + BAND_WIN]
    qc, kc = r // CHUNK, c // CHUNK
    allowed = (kc >= qc) & (kc <= qc + A_LEFT_CHUNKS)
    return jnp.where(allowed[None], skew * LOG2_E, NEG_BIG).transpose(0, 2, 1)


def _band_attention(q, k, v_t, rel_bias, bsz, seq_len):
    n_blk = BAND_WIN // BAND_TQ
    nq = seq_len // BAND_TQ
    per_tile = ROW_TILE // BAND_TQ
    tiles_per_seq = seq_len // ROW_TILE
    q3, k3 = (a.reshape(bsz, seq_len, A_W) for a in (q, k))
    blk = (None, BAND_TQ, A_W)
    key_blk = lambda i, j: jnp.maximum(i - (n_blk - 1) + j, 0)
    k_specs = [pl.BlockSpec(blk, lambda b, i, j=j: (b, key_blk(i, j), 0)) for j in range(n_blk)]
    vt_specs = [pl.BlockSpec((None, A_W, BAND_TQ),
                             lambda b, i, j=j: (b * tiles_per_seq + key_blk(i, j) // per_tile, 0, key_blk(i, j) % per_tile))
                for j in range(n_blk)]
    out = pl.pallas_call(
        _band_kernel, grid=(bsz, nq),
        in_specs=[pl.BlockSpec(blk, lambda b, i: (b, i, 0))] + k_specs + vt_specs
        + [_resident((A_HEADS, BAND_WIN, BAND_TQ))],
        out_specs=pl.BlockSpec(blk, lambda b, i: (b, i, 0)),
        out_shape=jax.ShapeDtypeStruct((bsz, seq_len, A_W), BF16),
        scratch_shapes=[pltpu.VMEM((A_HEADS, BAND_WIN, BAND_TQ), F32),
                        pltpu.VMEM((A_HEADS, BAND_WIN, BAND_TQ), BF16)],
        compiler_params=_cparams("parallel", "parallel"), name="band_attention",
    )(q3, *([k3] * n_blk), *([v_t] * n_blk), _band_bias_tiles(rel_bias))
    return out.reshape(bsz * seq_len, A_W)


def _stage_conv_input(x_ref, pad_ref):
    rows = x_ref.shape[1]

    @pl.when(pl.program_id(0) == 0)
    def _():
        pad_ref[:, 0:SUBLANES, :] = jnp.zeros((pad_ref.shape[0], SUBLANES, pad_ref.shape[2]), F32)

    @pl.when(pl.program_id(0) > 0)
    def _():
        pad_ref[:, 0:SUBLANES, :] = pad_ref[:, rows:rows + SUBLANES, :]

    pad_ref[:, SUBLANES:, :] = x_ref[...]


def _conv_silu_rows(pad_ref, w_ref, b, row0, cols, bias_ref=None):
    window = pad_ref[b, pl.ds(row0, SUBLANES + CHUNK), cols]
    acc = None
    for back in range(CONV_K):
        term = w_ref[CONV_K - 1 - back:CONV_K - back, cols] * window[SUBLANES - back:SUBLANES - back + CHUNK]
        acc = term if acc is None else acc + term
    if bias_ref is not None:
        acc = acc + bias_ref[:, cols]
    return acc * jax.nn.sigmoid(acc)


def _gdn_kernel(qkv_ref, sm_ref, z_ref, cw_ref, par_ref, nw_ref, o_ref,
                pad_ref, state_ref, lhs_ref, add_ref, gl_ref):
    bsz, rows_per_step = qkv_ref.shape[0], qkv_ref.shape[1]
    n_chunks = rows_per_step // CHUNK
    heads = [(b, h) for b in range(bsz) for h in range(B_HEADS)]
    units = [(cc, b, h) for cc in range(GDN_BUILD_CHUNKS) for b, h in heads]

    @pl.when(pl.program_id(0) == 0)
    def _():
        state_ref[...] = jnp.zeros_like(state_ref)

    _stage_conv_input(qkv_ref, pad_ref)
    causal, strict = _tril_masks(CHUNK)
    tril = jnp.where(causal, 1.0, 0.0).astype(BF16)
    eye = jnp.where(causal & ~strict, 1.0, 0.0)
    a_row = -jnp.exp(par_ref[0:1, :])
    dt_bias_row = par_ref[1:2, :]

    def head_cols(part, h):
        return slice(part * B_W + h * B_HEAD_DIM, part * B_W + (h + 1) * B_HEAD_DIM)

    def build(cp, carry):
        row0 = [pl.multiple_of((cp * GDN_BUILD_CHUNKS + cc) * CHUNK, CHUNK) for cc in range(GDN_BUILD_CHUNKS)]
        sm = {(cc, b): sm_ref[b, pl.ds(row0[cc], CHUNK), :] for cc in range(GDN_BUILD_CHUNKS) for b in range(bsz)}
        beta_all = {key: jax.nn.sigmoid(x) for key, x in sm.items()}
        gc_all = {key: _cumsum_rows(tril, a_row * jax.nn.softplus(x + dt_bias_row)) for key, x in sm.items()}
        gc_all_t = {key: x.T for key, x in gc_all.items()}
        q, k, v = ([_conv_silu_rows(pad_ref, cw_ref, b, row0[cc], head_cols(part, h)) for cc, b, h in units]
                   for part in range(3))
        q = [x * lax.rsqrt(jnp.sum(x * x, axis=-1, keepdims=True) + EPS) * (B_HEAD_DIM ** -0.5) for x in q]
        k = [x * lax.rsqrt(jnp.sum(x * x, axis=-1, keepdims=True) + EPS) for x in k]
        beta = [beta_all[cc, b][:, h:h + 1] for cc, b, h in units]
        gc = [gc_all[cc, b][:, B_HEADS + h:B_HEADS + h + 1] for cc, b, h in units]
        gc_row = [gc_all_t[cc, b][B_HEADS + h:B_HEADS + h + 1, :] for cc, b, h in units]
        decay = [jnp.exp(jnp.where(causal, g - gr, NEG_BIG)) for g, gr in zip(gc, gc_row)]
        kb = [x.astype(BF16) for x in k]
        kk = [_dot_nt(x, x) for x in kb]
        qk = [_dot_nt(x.astype(BF16), y) for x, y in zip(q, kb)]
        a_mat = [jnp.where(strict, bt * x * d, 0.0) for bt, x, d in zip(beta, kk, decay)]
        inv = [eye - a for a in a_mat]
        power = a_mat
        for _ in range(5):
            power = [_dot(x, x) for x in power]
            inv = [i + _dot(i, x) for i, x in zip(inv, power)]
        exp_gc = [jnp.exp(g) for g in gc]
        rhs = [jnp.concatenate([ki * (bt * e), vi * bt], axis=1) for ki, vi, bt, e in zip(k, v, beta, exp_gc)]
        wu = [_dot(i, r).astype(BF16) for i, r in zip(inv, rhs)]
        attn = [(x * d).astype(BF16) for x, d in zip(qk, decay)]
        gc_last = [g[CHUNK - 1:CHUNK, :] for g in gc]
        k_st = [(ki * jnp.exp(gl - g)).astype(BF16) for ki, gl, g in zip(k, gc_last, gc)]
        top = [_dot_tn(x, y) for x, y in zip(k_st, wu)]
        bot = [_dot(x, y) for x, y in zip(attn, wu)]
        for i in range(len(units)):
            slot = cp * len(units) + i
            lhs_ref[slot, 0:B_HEAD_DIM, :] = (-top[i][:, :B_HEAD_DIM]).astype(BF16)
            lhs_ref[slot, B_HEAD_DIM:, :] = (q[i] * exp_gc[i] - bot[i][:, :B_HEAD_DIM]).astype(BF16)
            add_ref[slot, 0:B_HEAD_DIM, :] = top[i][:, B_HEAD_DIM:]
            add_ref[slot, B_HEAD_DIM:, :] = bot[i][:, B_HEAD_DIM:]
            gl_ref[slot] = jnp.broadcast_to(jnp.exp(gc_last[i]), (SUBLANES, LANES))
        return carry

    lax.fori_loop(0, n_chunks // GDN_BUILD_CHUNKS, build, 0)

    def scan(c, carry):
        rows = pl.ds(pl.multiple_of(c * CHUNK, CHUNK), CHUNK)
        states = [state_ref[i] for i in range(len(heads))]
        res = [_dot(lhs_ref[c * len(heads) + i], states[i].astype(BF16)) + add_ref[c * len(heads) + i]
               for i in range(len(heads))]
        for i, (b, h) in enumerate(heads):
            state_ref[i] = gl_ref[c * len(heads) + i][0:1, :] * states[i] + res[i][:B_HEAD_DIM]
            zz = z_ref[b, rows, head_cols(0, h)]
            y = _rms(res[i][B_HEAD_DIM:], nw_ref[...]) * (zz * jax.nn.sigmoid(zz))
            o_ref[b, rows, head_cols(0, h)] = y.astype(o_ref.dtype)
        return carry

    lax.fori_loop(0, n_chunks, scan, 0)


def _gated_delta(qkv_raw, small, z, conv_w, a_log, dt_bias, norm_w, bsz, seq_len):
    par = jnp.zeros((2, LANES), F32)
    par = par.at[0, B_HEADS:2 * B_HEADS].set(a_log.astype(F32))
    par = par.at[1, B_HEADS:2 * B_HEADS].set(dt_bias.astype(F32))
    blk = lambda w: pl.BlockSpec((bsz, SCAN_ROWS, w), lambda t: (0, t, 0))
    n_units = bsz * B_HEADS
    n_slots = n_units * (SCAN_ROWS // CHUNK)
    out = pl.pallas_call(
        _gdn_kernel, grid=(seq_len // SCAN_ROWS,),
        in_specs=[blk(3 * B_W), blk(LANES), blk(B_W), _resident((CONV_K, 3 * B_W)), _resident((2, LANES)),
                  _resident((1, B_HEAD_DIM))],
        out_specs=blk(B_W),
        out_shape=jax.ShapeDtypeStruct((bsz, seq_len, B_W), BF16),
        scratch_shapes=[pltpu.VMEM((bsz, SUBLANES + SCAN_ROWS, 3 * B_W), F32),
                        pltpu.VMEM((n_units, B_HEAD_DIM, B_HEAD_DIM), F32),
                        pltpu.VMEM((n_slots, B_HEAD_DIM + CHUNK, B_HEAD_DIM), BF16),
                        pltpu.VMEM((n_slots, B_HEAD_DIM + CHUNK, B_HEAD_DIM), F32),
                        pltpu.VMEM((n_slots, SUBLANES, LANES), F32)],
        compiler_params=_cparams("arbitrary"), name="gated_delta",
    )(qkv_raw.reshape(bsz, seq_len, 3 * B_W), small.reshape(bsz, seq_len, LANES),
      z.reshape(bsz, seq_len, B_W), conv_w.astype(F32), par, norm_w.reshape(1, B_HEAD_DIM).astype(F32))
    return out.reshape(bsz * seq_len, B_W)


def _ssd_kernel(xbc_ref, sm_ref, z_ref, cw_ref, cb_ref, par_ref, skip_ref, nw_ref, o_ref, pad_ref, state_ref):
    bsz, rows_per_step = xbc_ref.shape[0], xbc_ref.shape[1]
    group_w = C_W // C_GROUPS
    heads_per_group = C_HEADS // C_GROUPS
    pairs_per_group = heads_per_group // 2

    @pl.when(pl.program_id(0) == 0)
    def _():
        state_ref[...] = jnp.zeros_like(state_ref)

    _stage_conv_input(xbc_ref, pad_ref)
    causal, _ = _tril_masks(CHUNK)
    tril = jnp.where(causal, 1.0, 0.0).astype(BF16)
    a_row = -jnp.exp(par_ref[0:1, :])
    dt_bias_row = par_ref[1:2, :]
    lane = lax.broadcasted_iota(jnp.int32, (CHUNK, LANES), 1)
    low_half = lane < C_HEAD_DIM

    def per_head_lanes(cols, h0):
        return jnp.where(low_half, cols[:, h0:h0 + 1], cols[:, h0 + 1:h0 + 2])

    def pair_cols(pr):
        return slice(pr * LANES, (pr + 1) * LANES)

    def chunk_pair(cp, carry):
        units = [(cc, b) for cc in range(SSD_CHUNKS) for b in range(bsz)]
        groups = [(u, g) for u in units for g in range(C_GROUPS)]
        row0 = [pl.multiple_of((cp * SSD_CHUNKS + cc) * CHUNK, CHUNK) for cc in range(SSD_CHUNKS)]
        conv = lambda u, cols: _conv_silu_rows(pad_ref, cw_ref, u[1], row0[u[0]], cols, cb_ref)
        sm = {u: sm_ref[u[1], pl.ds(row0[u[0]], CHUNK), :] for u in units}
        dt_all = {u: jax.nn.softplus(sm[u] + dt_bias_row) for u in units}
        da_cs = {u: _cumsum_rows(tril, dt_all[u] * a_row) for u in units}
        da_cs_t = {u: da_cs[u].T for u in units}
        dt_t = {u: dt_all[u].T for u in units}
        da_last = {u: da_cs[u][CHUNK - 1:CHUNK, :] for u in units}
        exp_da = {u: jnp.exp(da_cs[u]) for u in units}
        to_end = {u: jnp.exp(da_last[u] - da_cs[u]) * dt_all[u] for u in units}
        chunk_decay = {u: jnp.exp(da_last[u]) for u in units}
        x2 = {(u, pr): conv(u, pair_cols(pr)) for u in units for pr in range(C_HEADS // 2)}
        bm = {(u, g): conv(u, slice(C_W + g * C_STATE, C_W + (g + 1) * C_STATE)).astype(BF16) for u, g in groups}
        cm = {(u, g): conv(u, slice(C_W + C_BC + g * C_STATE, C_W + C_BC + (g + 1) * C_STATE)).astype(BF16)
              for u, g in groups}
        cb = {key: _dot_nt(cm[key], bm[key]) for key in groups}
        xw = {(u, g): jnp.concatenate(
            [(x2[u, g * pairs_per_group + j] * per_head_lanes(to_end[u], g * heads_per_group + 2 * j)).astype(BF16)
             for j in range(pairs_per_group)], axis=1) for u, g in groups}
        inflow = {key: _dot_tn(bm[key], xw[key]) for key in groups}
        y_diag = {}
        for u in units:
            for h in range(C_HEADS):
                seg = da_cs[u][:, h:h + 1] - da_cs_t[u][h:h + 1, :]
                wts = cb[u, h // heads_per_group] * jnp.exp(jnp.where(causal, seg, NEG_BIG)) * dt_t[u][h:h + 1, :]
                y_diag[u, h] = _dot(wts.astype(BF16), x2[u, h // 2].astype(BF16))
        for u in units:
            cc, b = u
            rows = pl.ds(row0[cc], CHUNK)
            for g in range(C_GROUPS):
                gl = slice(g * group_w, (g + 1) * group_w)
                prev = state_ref[b, :, gl]
                y_off = _dot(cm[u, g], prev.astype(BF16))
                dec = jnp.concatenate(
                    [jnp.where(low_half[0:1], chunk_decay[u][:, h0:h0 + 1], chunk_decay[u][:, h0 + 1:h0 + 2])
                     for h0 in range(g * heads_per_group, (g + 1) * heads_per_group, 2)], axis=1)
                state_ref[b, :, gl] = prev * dec + inflow[u, g]
                ys = []
                for j in range(pairs_per_group):
                    pr = g * pairs_per_group + j
                    h0 = 2 * pr
                    y = jnp.where(low_half, y_diag[u, h0], y_diag[u, h0 + 1])
                    y = y + y_off[:, pair_cols(j)] * per_head_lanes(exp_da[u], h0) + skip_ref[:, pair_cols(pr)] * x2[u, pr]
                    zz = z_ref[b, rows, pair_cols(pr)]
                    ys.append(y * (zz * jax.nn.sigmoid(zz)))
                yg = jnp.concatenate(ys, axis=1)
                o_ref[b, rows, gl] = _rms(yg, nw_ref[:, gl]).astype(o_ref.dtype)
        return carry

    lax.fori_loop(0, rows_per_step // (CHUNK * SSD_CHUNKS), chunk_pair, 0)


def _ssd(xbc_raw, small, z, conv_w, conv_b, dt_bias, a_log, d_skip, norm_w, bsz, seq_len):
    par = jnp.zeros((2, LANES), F32)
    par = par.at[0, :C_HEADS].set(a_log.astype(F32))
    par = par.at[1, :C_HEADS].set(dt_bias.astype(F32))
    skip = jnp.repeat(d_skip.astype(F32), C_HEAD_DIM).reshape(1, C_W)
    width = C_W + 2 * C_BC
    blk = lambda w: pl.BlockSpec((bsz, SCAN_ROWS, w), lambda t: (0, t, 0))
    out = pl.pallas_call(
        _ssd_kernel, grid=(seq_len // SCAN_ROWS,),
        in_specs=[blk(width), blk(LANES), blk(C_W), _resident((CONV_K, width)), _resident((1, width)),
                  _resident((2, LANES)), _resident((1, C_W)), _resident((1, C_W))],
        out_specs=blk(C_W),
        out_shape=jax.ShapeDtypeStruct((bsz, seq_len, C_W), BF16),
        scratch_shapes=[pltpu.VMEM((bsz, SUBLANES + SCAN_ROWS, width), F32),
                        pltpu.VMEM((bsz, C_STATE, C_W), F32)],
        compiler_params=_cparams("arbitrary"), name="ssd",
    )(xbc_raw.reshape(bsz, seq_len, width), small.reshape(bsz, seq_len, LANES), z.reshape(bsz, seq_len, C_W),
      conv_w.astype(F32), conv_b.astype(F32).reshape(1, width), par, skip, norm_w.reshape(1, C_W).astype(F32))
    return out.reshape(bsz * seq_len, C_W)


FOX_F_PIECES = 3


def _fox_f_lane(h):
    return (h // 2) * LANES + (D_HEAD_DIM if h % 2 == 0 else 0)


def _fox_prep_kernel(sm_ref, fb_ref, place_ref, fk_ref, carry_ref):
    tm = sm_ref.shape[0]

    @pl.when(pl.program_id(1) == 0)
    def _():
        carry_ref[...] = jnp.zeros_like(carry_ref)

    causal, _ = _tril_masks(tm)
    tril = jnp.where(causal, 1.0, 0.0).astype(BF16)
    log_f = jax.nn.log_sigmoid(sm_ref[...] + fb_ref[...])
    f_cum = _cumsum_rows(tril, log_f) + carry_ref[...]
    carry_ref[...] = f_cum[tm - 1:tm, :]
    pieces = jnp.concatenate(_split3(-LOG2_E * f_cum), axis=1)
    fk_ref[...] = _dot(pieces, place_ref[...]).astype(fk_ref.dtype)


def _fox_prep(small, f_bias, bsz, seq_len):
    fb = jnp.zeros((1, LANES), F32).at[0, D_HEADS:2 * D_HEADS].set(f_bias.astype(F32))
    place = np.zeros((FOX_F_PIECES * LANES, D_W), np.float32)
    for h in range(D_HEADS):
        for piece in range(FOX_F_PIECES):
            place[piece * LANES + D_HEADS + h, _fox_f_lane(h) + piece] = 1.0
    nt = seq_len // FOX_T
    return pl.pallas_call(
        _fox_prep_kernel, grid=(bsz, nt),
        in_specs=[pl.BlockSpec((None, FOX_T, LANES), lambda b, t: (b, t, 0)), _resident((1, LANES)),
                  _resident(place.shape)],
        out_specs=pl.BlockSpec((None, FOX_T, D_W), lambda b, t: (b, t, 0)),
        out_shape=jax.ShapeDtypeStruct((bsz, seq_len, D_W), BF16),
        scratch_shapes=[pltpu.VMEM((1, LANES), F32)],
        compiler_params=_cparams("parallel", "arbitrary"), name="fox_prep",
    )(small.reshape(bsz, seq_len, LANES), fb, jnp.asarray(place, BF16))


def _fox_kernel(q_ref, k_ref, vt_ref, fk_ref, o_ref,
                kk_ref, vv_ref, sa_ref, sb_ref, pa_ref, pb_ref, aa_ref, ab_ref, m_ref, acc_ref):
    i = pl.program_id(2)
    t = FOX_T
    nt = k_ref.shape[0] // t
    n_rb = t // FOX_ROWS
    lane = lax.broadcasted_iota(jnp.int32, (t, LANES), 1)
    low_half = lane < D_HEAD_DIM
    top_half = lax.broadcasted_iota(jnp.int32, (LANES, t), 0) < D_HEAD_DIM

    @pl.when(i == 0)
    def _():
        for jt in range(nt):
            rows = slice(jt * t, (jt + 1) * t)
            kt, ft = k_ref[rows, :], fk_ref[rows, :]
            kk_ref[0, jt] = jnp.where(low_half, kt, ft)
            kk_ref[1, jt] = jnp.where(low_half, ft, kt)
            vt = vt_ref[jt]
            one = jnp.ones_like(vt)
            vv_ref[0, jt] = jnp.where(top_half, vt, one)
            vv_ref[1, jt] = jnp.where(top_half, one, vt)
        kk_ref[0, nt] = jnp.where(low_half, 0.0, NEG_BIG).astype(BF16)
        kk_ref[1, nt] = jnp.where(low_half, NEG_BIG, 0.0).astype(BF16)

    q = q_ref[...]
    ones_upto = lambda n: jnp.where(lane < n, 1.0, 0.0).astype(BF16)
    qs = (jnp.where(low_half, q, ones_upto(_fox_f_lane(0) + FOX_F_PIECES)),
          jnp.where(low_half, ones_upto(_fox_f_lane(1) + FOX_F_PIECES), q))
    key_in_blk = lax.broadcasted_iota(jnp.int32, (FOX_ROWS, t), 0)
    query = lax.broadcasted_iota(jnp.int32, (FOX_ROWS, t), 1)

    m_ref[...] = jnp.full(m_ref.shape, NEG_BIG, F32)
    acc_ref[...] = jnp.zeros_like(acc_ref)

    def scores(j, s_ref):
        jj = jnp.where(j >= 0, j, nt)
        for half in range(2):
            s_ref[half] = _dot_nt(kk_ref[half, jj], qs[half])

    def softmax_cols(s_ref, p_ref, a_ref, diagonal):
        def block(half, rb):
            s = s_ref[half, rb * FOX_ROWS:(rb + 1) * FOX_ROWS, :]
            if diagonal:
                s = jnp.where(key_in_blk + rb * FOX_ROWS <= query, s, NEG_BIG)
            return s

        tile_max = []
        for half in range(2):
            m8 = None
            for rb in range(n_rb):
                s = block(half, rb)
                b8 = jnp.maximum(jnp.maximum(s[0:8], s[8:16]), jnp.maximum(s[16:24], s[24:32]))
                m8 = b8 if m8 is None else jnp.maximum(m8, b8)
            tile_max.append(jnp.max(m8, axis=0, keepdims=True))
        for half in range(2):
            m_old = m_ref[half]
            m_new = jnp.maximum(m_old, tile_max[half])
            m_ref[half] = m_new
            a_ref[half] = jnp.exp2(m_old - m_new)
            for rb in range(n_rb):
                p_ref[half, rb * FOX_ROWS:(rb + 1) * FOX_ROWS, :] = jnp.exp2(block(half, rb) - m_new).astype(BF16)

    def accumulate(j, p_ref, a_ref):
        jj = jnp.maximum(j, 0)
        for half in range(2):
            acc_ref[half] = a_ref[half] * acc_ref[half] + _dot(vv_ref[half, jj], p_ref[half])

    scores(i, sb_ref)
    scores(i - 1, sa_ref)
    softmax_cols(sb_ref, pb_ref, ab_ref, True)

    def pair(n, carry):
        j = i - 1 - 2 * n
        scores(j - 1, sb_ref)
        softmax_cols(sa_ref, pa_ref, aa_ref, False)
        accumulate(j + 1, pb_ref, ab_ref)
        scores(j - 2, sa_ref)
        softmax_cols(sb_ref, pb_ref, ab_ref, False)
        accumulate(j, pa_ref, aa_ref)
        return carry

    n_pairs = (i + 1) // 2
    lax.fori_loop(0, n_pairs, pair, 0)
    accumulate(i - 2 * n_pairs, pb_ref, ab_ref)
    acc0, acc1 = acc_ref[0], acc_ref[1]
    out_t = jnp.where(top_half, acc0 / acc0[D_HEAD_DIM:D_HEAD_DIM + 1, :], acc1 / acc1[0:1, :])
    o_ref[...] = out_t.T.astype(o_ref.dtype)


def _fox_attention(q, k, v_t, fk, bsz, seq_len):
    nt = seq_len // FOX_T
    pairs = D_W // LANES
    q3, k3 = (a.reshape(bsz, seq_len, D_W) for a in (q, k))
    whole_seq = pl.BlockSpec((None, seq_len, LANES), lambda b, p, i: (b, 0, p))
    out = pl.pallas_call(
        _fox_kernel, grid=(bsz, pairs, nt),
        in_specs=[pl.BlockSpec((None, FOX_T, LANES), lambda b, p, i: (b, i, p)), whole_seq,
                  pl.BlockSpec((nt, LANES, FOX_T), lambda b, p, i: (b, p, 0)), whole_seq],
        out_specs=pl.BlockSpec((None, FOX_T, LANES), lambda b, p, i: (b, i, p)),
        out_shape=jax.ShapeDtypeStruct((bsz, seq_len, D_W), BF16),
        scratch_shapes=[pltpu.VMEM((2, nt + 1, FOX_T, LANES), BF16), pltpu.VMEM((2, nt, LANES, FOX_T), BF16)]
        + [pltpu.VMEM((2, FOX_T, FOX_T), F32)] * 2 + [pltpu.VMEM((2, FOX_T, FOX_T), BF16)] * 2
        + [pltpu.VMEM((2, 1, FOX_T), F32)] * 3 + [pltpu.VMEM((2, LANES, FOX_T), F32)],
        compiler_params=_cparams("parallel", "parallel", "arbitrary"), name="fox_attention",
    )(q3, k3, v_t, fk)
    return out.reshape(bsz * seq_len, D_W)


def _pad_cols(w, width=LANES):
    return jnp.pad(w, ((0, 0), (0, width - w.shape[1])))


def kernel(x, norm_mix, norm_ffn, norm_final, ffn_w_gate, ffn_w_up, ffn_w_down, ab_w_in, ab_rel_bias, ab_conv_w, ab_a_log, ab_dt_bias, ab_norm_w, ab_w_out, cd_w_in, cd_conv_w, cd_conv_b, cd_dt_bias, cd_a_log, cd_d_skip, cd_norm_w, cd_f_bias, cd_w_out):
    bsz, seq_len, d = x.shape
    n = bsz * seq_len
    xf = x.reshape(n, d)
    bf = lambda w: w.astype(BF16)

    w_in = ab_w_in[0]
    o = np.cumsum([0, A_W, A_W, A_W, 3 * B_W, B_HEADS, B_HEADS, B_W])
    weights = [bf(w_in[:, o[0]:o[1]] * (LOG2_E * A_HEAD_DIM ** -0.5)), bf(w_in[:, o[1]:o[2]]), bf(w_in[:, o[2]:o[3]].T),
               bf(w_in[:, o[3]:o[4]]), bf(_pad_cols(w_in[:, o[4]:o[6]])), bf(w_in[:, o[6]:o[7]])]
    a_q, a_k, a_vt, b_qkv, b_small, b_z = _norm_proj(
        xf, norm_mix[0], weights, [BF16, BF16, BF16, F32, F32, F32], transposed=(False, False, True) + (False,) * 3)
    o_a = _band_attention(a_q, a_k, a_vt, ab_rel_bias[0], bsz, seq_len)
    o_b = _gated_delta(b_qkv, b_small, b_z, ab_conv_w[0], ab_a_log[0], ab_dt_bias[0], ab_norm_w[0], bsz, seq_len)
    ffn_w = (bf(ffn_w_gate), bf(ffn_w_up), bf(ffn_w_down))
    xf = _layer_tail(xf, o_a, o_b, bf(ab_w_out), norm_ffn[0], *ffn_w, 0, norm_final, False)

    w_in = cd_w_in[0]
    o = np.cumsum([0, C_W, C_W + 2 * C_BC, C_HEADS, D_W, D_W, D_W, D_HEADS])
    small_w = jnp.concatenate([w_in[:, o[2]:o[3]], w_in[:, o[6]:o[7]]], axis=1)
    weights = [bf(w_in[:, o[0]:o[1]]), bf(w_in[:, o[1]:o[2]]), bf(_pad_cols(small_w)),
               bf(w_in[:, o[3]:o[4]] * (LOG2_E * D_HEAD_DIM ** -0.5)), bf(w_in[:, o[4]:o[5]]), bf(w_in[:, o[5]:o[6]].T)]
    c_z, c_xbc, cd_small, d_q, d_k, d_vt = _norm_proj(
        xf, norm_mix[1], weights, [F32, F32, F32, BF16, BF16, BF16], transposed=(False,) * 5 + (True,))
    y_c = _ssd(c_xbc, cd_small, c_z, cd_conv_w[0], cd_conv_b[0], cd_dt_bias[0], cd_a_log[0], cd_d_skip[0],
               cd_norm_w[0], bsz, seq_len)
    fk = _fox_prep(cd_small, cd_f_bias[0], bsz, seq_len)
    o_d = _fox_attention(d_q, d_k, d_vt, fk, bsz, seq_len)
    xf = _layer_tail(xf, y_c, o_d, bf(cd_w_out), norm_ffn[1], *ffn_w, 1, norm_final, True)
    return xf.reshape(bsz, seq_len, d)
```

```python
import functools

import jax
import jax.numpy as jnp
import numpy as np
from jax import lax
from jax.experimental import pallas as pl
from jax.experimental.pallas import tpu as pltpu

F32 = jnp.float32
BF16 = jnp.bfloat16

D_MODEL = 1024
CHUNK = 64
EPS = 1e-6
CONV_K = 4
A_HEADS, A_HEAD_DIM, A_LEFT_CHUNKS, A_MAX_REL = 8, 64, 8, 256
B_HEADS, B_HEAD_DIM = 4, 128
C_HEADS, C_HEAD_DIM, C_GROUPS, C_STATE = 8, 64, 2, 128
D_HEADS, D_HEAD_DIM = 8, 64
A_W = A_HEADS * A_HEAD_DIM
B_W = B_HEADS * B_HEAD_DIM
C_W = C_HEADS * C_HEAD_DIM
D_W = D_HEADS * D_HEAD_DIM
C_BC = C_GROUPS * C_STATE

LANES = 128
SUBLANES = 8
VMEM_LIMIT_BYTES = 56 * 1024 * 1024
NEG_BIG = -1e30
LOG2_E = float(np.log2(np.e))

ROW_TILE = 512
FFN_CHUNK = 256
BAND_TQ = 256
BAND_WIN = BAND_TQ + A_LEFT_CHUNKS * CHUNK
FOX_T = 512
FOX_ROWS = 32
SCAN_ROWS = 512
SSD_CHUNKS = 4
GDN_BUILD_CHUNKS = 4


def _cparams(*sem):
    return pltpu.CompilerParams(dimension_semantics=sem, vmem_limit_bytes=VMEM_LIMIT_BYTES)


def _resident(shape):
    nd = len(shape)
    return pl.BlockSpec(shape, lambda *_: (0,) * nd, pipeline_mode=pl.Buffered(1))


def _dot(a, b):
    return jnp.dot(a, b, preferred_element_type=F32)


def _dot_nt(a, b):
    return lax.dot_general(a, b, (((1,), (1,)), ((), ())), preferred_element_type=F32)


def _dot_tn(a, b):
    return lax.dot_general(a, b, (((0,), (0,)), ((), ())), preferred_element_type=F32)


def _rms(x, w):
    return x * lax.rsqrt(jnp.mean(x * x, axis=-1, keepdims=True) + EPS) * w


def _split3(x):
    hi = x.astype(BF16)
    r1 = x - hi.astype(F32)
    mid = r1.astype(BF16)
    lo = (r1 - mid.astype(F32)).astype(BF16)
    return hi, mid, lo


def _cumsum_rows(tril, x):
    hi, mid, lo = _split3(x)
    return _dot(tril, hi) + _dot(tril, mid) + _dot(tril, lo)


def _tril_masks(n):
    r = lax.broadcasted_iota(jnp.int32, (n, n), 0)
    c = lax.broadcasted_iota(jnp.int32, (n, n), 1)
    return r >= c, r > c


def _norm_proj_kernel(x_ref, nw_ref, *refs, transposed):
    n_out = len(refs) // 2
    h = _rms(x_ref[...], nw_ref[...]).astype(BF16)
    for w_ref, o_ref, tr in zip(refs[:n_out], refs[n_out:], transposed):
        out = _dot_nt(w_ref[...], h) if tr else _dot(h, w_ref[...])
        o_ref[...] = out.astype(o_ref.dtype)


def _norm_proj(x, norm_w, weights, out_dtypes, transposed=None):
    n, d = x.shape
    transposed = transposed or (False,) * len(weights)
    in_specs = [pl.BlockSpec((ROW_TILE, d), lambda i: (i, 0)), _resident((1, d))]
    in_specs += [_resident(w.shape) for w in weights]
    out_shape, out_specs = [], []
    for w, dt, tr in zip(weights, out_dtypes, transposed):
        if tr:
            out_shape.append(jax.ShapeDtypeStruct((n // ROW_TILE, w.shape[0], ROW_TILE), dt))
            out_specs.append(pl.BlockSpec((None, w.shape[0], ROW_TILE), lambda i: (i, 0, 0)))
        else:
            out_shape.append(jax.ShapeDtypeStruct((n, w.shape[1]), dt))
            out_specs.append(pl.BlockSpec((ROW_TILE, w.shape[1]), lambda i: (i, 0)))
    return pl.pallas_call(
        functools.partial(_norm_proj_kernel, transposed=tuple(transposed)),
        grid=(n // ROW_TILE,), in_specs=in_specs, out_specs=out_specs,
        out_shape=out_shape, compiler_params=_cparams("parallel"), name="norm_proj",
    )(x, norm_w.reshape(1, d), *weights)


def _tail_kernel(x_ref, a_ref, b_ref, woa_ref, wob_ref, nw_ref, wg_ref, wu_ref, wd_ref, fn_ref,
                 o_ref, acc_ref, *, d_ff, final_norm):
    x1 = x_ref[...] + _dot(a_ref[...], woa_ref[...]) + _dot(b_ref[...], wob_ref[...])
    h = _rms(x1, nw_ref[...]).astype(BF16)
    acc_ref[...] = x1
    ffn = None
    for c in range(d_ff // FFN_CHUNK):
        cols = slice(c * FFN_CHUNK, (c + 1) * FFN_CHUNK)
        g = _dot(h, wg_ref[:, cols])
        u = _dot(h, wu_ref[:, cols])
        part = _dot((g * jax.nn.sigmoid(g) * u).astype(BF16), wd_ref[cols, :])
        ffn = part if ffn is None else ffn + part
    y = acc_ref[...] + ffn
    if final_norm:
        y = _rms(y, fn_ref[...])
    o_ref[...] = y


def _layer_tail(x, mix_a, mix_b, w_out, norm_w, wg, wu, wd, layer, final_w, final_norm):
    n, d = x.shape
    d_ff = wg.shape[2]
    wa, wb = mix_a.shape[1], mix_b.shape[1]
    assert wa == wb and w_out.shape == (1, wa + wb, d)
    row = lambda w: pl.BlockSpec((ROW_TILE, w), lambda i: (i, 0))
    pick = lambda shape, idx: pl.BlockSpec((None,) + shape, lambda i: idx, pipeline_mode=pl.Buffered(1))
    in_specs = [row(d), row(wa), row(wb), pick((wa, d), (0, 0, 0)), pick((wb, d), (0, 1, 0)), _resident((1, d)),
                pick((d, d_ff), (layer, 0, 0)), pick((d, d_ff), (layer, 0, 0)), pick((d_ff, d), (layer, 0, 0)),
                _resident((1, d))]
    return pl.pallas_call(
        functools.partial(_tail_kernel, d_ff=d_ff, final_norm=final_norm),
        grid=(n // ROW_TILE,), in_specs=in_specs, out_specs=row(d),
        out_shape=jax.ShapeDtypeStruct((n, d), F32),
        scratch_shapes=[pltpu.VMEM((ROW_TILE, d), F32)],
        compiler_params=_cparams("parallel"), name="layer_tail",
    )(x, mix_a, mix_b, w_out, w_out, norm_w.reshape(1, d), wg, wu, wd, final_w.reshape(1, d))


def _band_kernel(q_ref, *refs):
    n_blk = BAND_WIN // BAND_TQ
    k_refs, vt_refs = refs[:n_blk], refs[n_blk:2 * n_blk]
    by_offset_ref, o_ref, bias_ref, s_ref, p_ref = refs[2 * n_blk:]
    i = pl.program_id(1)

    @pl.when((pl.program_id(0) == 0) & (i == 0))
    def _():
        _fill_band_bias(by_offset_ref, bias_ref)

    lane = lax.broadcasted_iota(jnp.int32, (BAND_TQ, LANES), 1)
    low_half = lane < A_HEAD_DIM
    top_half = lax.broadcasted_iota(jnp.int32, (LANES, BAND_WIN), 0) < A_HEAD_DIM
    pairs = A_W // LANES
    n_rb = BAND_WIN // FOX_ROWS
    fill = [jnp.where(i - (n_blk - 1) + j >= 0, 0.0, NEG_BIG) for j in range(n_blk)]
    one_at = lambda n: jnp.where(lane == n, 1.0, 0.0).astype(BF16)

    v_aug = []
    for pr in range(pairs):
        cols = slice(pr * LANES, (pr + 1) * LANES)
        q = q_ref[:, cols]
        k_blocks = [r[:, cols] for r in k_refs]
        k_even = jnp.concatenate([jnp.where(low_half, kb, f.astype(BF16)) for kb, f in zip(k_blocks, fill)], axis=0)
        k_odd = jnp.concatenate([jnp.where(low_half, f.astype(BF16), kb) for kb, f in zip(k_blocks, fill)], axis=0)
        s_ref[2 * pr] = _dot_nt(k_even, jnp.where(low_half, q, one_at(A_HEAD_DIM))) + bias_ref[2 * pr]
        s_ref[2 * pr + 1] = _dot_nt(k_odd, jnp.where(low_half, one_at(0), q)) + bias_ref[2 * pr + 1]
        vt = jnp.concatenate([r[cols, :] for r in vt_refs], axis=1)
        one = jnp.ones_like(vt)
        v_aug += [jnp.where(top_half, vt, one), jnp.where(top_half, one, vt)]

    rows = lambda rb: slice(rb * FOX_ROWS, (rb + 1) * FOX_ROWS)
    col_max = []
    for h in range(A_HEADS):
        m8 = None
        for rb in range(n_rb):
            s = s_ref[h, rows(rb), :]
            b8 = jnp.maximum(jnp.maximum(s[0:8], s[8:16]), jnp.maximum(s[16:24], s[24:32]))
            m8 = b8 if m8 is None else jnp.maximum(m8, b8)
        col_max.append(jnp.max(m8, axis=0, keepdims=True))
    for h in range(A_HEADS):
        for rb in range(n_rb):
            p_ref[h, rows(rb), :] = jnp.exp2(s_ref[h, rows(rb), :] - col_max[h]).astype(BF16)

    out_top = lax.broadcasted_iota(jnp.int32, (LANES, BAND_TQ), 0) < A_HEAD_DIM
    for pr in range(pairs):
        o_even = _dot(v_aug[2 * pr], p_ref[2 * pr])
        o_odd = _dot(v_aug[2 * pr + 1], p_ref[2 * pr + 1])
        out_t = jnp.where(out_top, o_even / o_even[A_HEAD_DIM:A_HEAD_DIM + 1, :], o_odd / o_odd[0:1, :])
        o_ref[:, pr * LANES:(pr + 1) * LANES] = out_t.T.astype(o_ref.dtype)


BAND_BIAS_PERIOD = BAND_WIN + BAND_TQ


def _band_bias_by_offset(rel_bias):
    d = np.arange(BAND_BIAS_PERIOD) - BAND_WIN
    idx = np.clip(d + A_LEFT_CHUNKS * CHUNK, -A_MAX_REL, A_MAX_REL) + A_MAX_REL
    return rel_bias.astype(F32)[:, idx] * LOG2_E


def _fill_band_bias(by_offset_ref, bias_ref):
    kc = lax.broadcasted_iota(jnp.int32, (BAND_WIN, BAND_TQ), 0) // CHUNK
    qc = lax.broadcasted_iota(jnp.int32, (BAND_WIN, BAND_TQ), 1) // CHUNK
    for h in range(A_HEADS):
        spread = jnp.broadcast_to(by_offset_ref[h:h + 1, :], (BAND_WIN, BAND_BIAS_PERIOD))
        skewed = pltpu.roll(spread, 0, axis=1, stride=1, stride_axis=0)[:, BAND_WIN:]
        bias_ref[h] = jnp.where(kc >= qc, jnp.where(kc <= qc + A_LEFT_CHUNKS, skewed, NEG_BIG), NEG_BIG)


def _band_attention(q, k, v_t, rel_bias, bsz, seq_len):
    n_blk = BAND_WIN // BAND_TQ
    nq = seq_len // BAND_TQ
    per_tile = ROW_TILE // BAND_TQ
    tiles_per_seq = seq_len // ROW_TILE
    q3, k3 = (a.reshape(bsz, seq_len, A_W) for a in (q, k))
    blk = (None, BAND_TQ, A_W)
    key_blk = lambda i, j: jnp.maximum(i - (n_blk - 1) + j, 0)
    k_specs = [pl.BlockSpec(blk, lambda b, i, j=j: (b, key_blk(i, j), 0)) for j in range(n_blk)]
    vt_specs = [pl.BlockSpec((None, A_W, BAND_TQ),
                             lambda b, i, j=j: (b * tiles_per_seq + key_blk(i, j) // per_tile, 0, key_blk(i, j) % per_tile))
                for j in range(n_blk)]
    out = pl.pallas_call(
        _band_kernel, grid=(bsz, nq),
        in_specs=[pl.BlockSpec(blk, lambda b, i: (b, i, 0))] + k_specs + vt_specs
        + [_resident((A_HEADS, BAND_BIAS_PERIOD))],
        out_specs=pl.BlockSpec(blk, lambda b, i: (b, i, 0)),
        out_shape=jax.ShapeDtypeStruct((bsz, seq_len, A_W), BF16),
        scratch_shapes=[pltpu.VMEM((A_HEADS, BAND_WIN, BAND_TQ), F32)] * 2
        + [pltpu.VMEM((A_HEADS, BAND_WIN, BAND_TQ), BF16)],
        compiler_params=_cparams("arbitrary", "arbitrary"), name="band_attention",
    )(q3, *([k3] * n_blk), *([v_t] * n_blk), _band_bias_by_offset(rel_bias))
    return out.reshape(bsz * seq_len, A_W)


def _stage_conv_input(x_ref, pad_ref):
    rows = x_ref.shape[1]

    @pl.when(pl.program_id(0) == 0)
    def _():
        pad_ref[:, 0:SUBLANES, :] = jnp.zeros((pad_ref.shape[0], SUBLANES, pad_ref.shape[2]), F32)

    @pl.when(pl.program_id(0) > 0)
    def _():
        pad_ref[:, 0:SUBLANES, :] = pad_ref[:, rows:rows + SUBLANES, :]

    pad_ref[:, SUBLANES:, :] = x_ref[...]


def _conv_silu_rows(pad_ref, w_ref, b, row0, cols, bias_ref=None):
    window = pad_ref[b, pl.ds(row0, SUBLANES + CHUNK), cols]
    acc = None
    for back in range(CONV_K):
        term = w_ref[CONV_K - 1 - back:CONV_K - back, cols] * window[SUBLANES - back:SUBLANES - back + CHUNK]
        acc = term if acc is None else acc + term
    if bias_ref is not None:
        acc = acc + bias_ref[:, cols]
    return acc * jax.nn.sigmoid(acc)


def _gdn_kernel(qkv_ref, sm_ref, z_ref, cw_ref, par_ref, nw_ref, o_ref,
                pad_ref, state_ref, lhs_ref, add_ref, gl_ref):
    bsz, rows_per_step = qkv_ref.shape[0], qkv_ref.shape[1]
    n_chunks = rows_per_step // CHUNK
    heads = [(b, h) for b in range(bsz) for h in range(B_HEADS)]
    units = [(cc, b, h) for cc in range(GDN_BUILD_CHUNKS) for b, h in heads]

    @pl.when(pl.program_id(0) == 0)
    def _():
        state_ref[...] = jnp.zeros_like(state_ref)

    _stage_conv_input(qkv_ref, pad_ref)
    causal, strict = _tril_masks(CHUNK)
    tril = jnp.where(causal, 1.0, 0.0).astype(BF16)
    eye = jnp.where(causal & ~strict, 1.0, 0.0)
    a_row = -jnp.exp(par_ref[0:1, :])
    dt_bias_row = par_ref[1:2, :]

    def head_cols(part, h):
        return slice(part * B_W + h * B_HEAD_DIM, part * B_W + (h + 1) * B_HEAD_DIM)

    def build(cp, carry):
        row0 = [pl.multiple_of((cp * GDN_BUILD_CHUNKS + cc) * CHUNK, CHUNK) for cc in range(GDN_BUILD_CHUNKS)]
        sm = {(cc, b): sm_ref[b, pl.ds(row0[cc], CHUNK), :] for cc in range(GDN_BUILD_CHUNKS) for b in range(bsz)}
        beta_all = {key: jax.nn.sigmoid(x) for key, x in sm.items()}
        gc_all = {key: _cumsum_rows(tril, a_row * jax.nn.softplus(x + dt_bias_row)) for key, x in sm.items()}
        gc_all_t = {key: x.T for key, x in gc_all.items()}
        q, k, v = ([_conv_silu_rows(pad_ref, cw_ref, b, row0[cc], head_cols(part, h)) for cc, b, h in units]
                   for part in range(3))
        q = [x * lax.rsqrt(jnp.sum(x * x, axis=-1, keepdims=True) + EPS) * (B_HEAD_DIM ** -0.5) for x in q]
        k = [x * lax.rsqrt(jnp.sum(x * x, axis=-1, keepdims=True) + EPS) for x in k]
        beta = [beta_all[cc, b][:, h:h + 1] for cc, b, h in units]
        gc = [gc_all[cc, b][:, B_HEADS + h:B_HEADS + h + 1] for cc, b, h in units]
        gc_row = [gc_all_t[cc, b][B_HEADS + h:B_HEADS + h + 1, :] for cc, b, h in units]
        decay = [jnp.exp(jnp.where(causal, g - gr, NEG_BIG)) for g, gr in zip(gc, gc_row)]
        kb = [x.astype(BF16) for x in k]
        kk = [_dot_nt(x, x) for x in kb]
        qk = [_dot_nt(x.astype(BF16), y) for x, y in zip(q, kb)]
        a_mat = [jnp.where(strict, bt * x * d, 0.0) for bt, x, d in zip(beta, kk, decay)]
        inv = [eye - a for a in a_mat]
        power = a_mat
        for _ in range(5):
            power = [_dot(x, x) for x in power]
            inv = [i + _dot(i, x) for i, x in zip(inv, power)]
        exp_gc = [jnp.exp(g) for g in gc]
        rhs = [jnp.concatenate([ki * (bt * e), vi * bt], axis=1) for ki, vi, bt, e in zip(k, v, beta, exp_gc)]
        wu = [_dot(i, r).astype(BF16) for i, r in zip(inv, rhs)]
        attn = [(x * d).astype(BF16) for x, d in zip(qk, decay)]
        gc_last = [g[CHUNK - 1:CHUNK, :] for g in gc]
        k_st = [(ki * jnp.exp(gl - g)).astype(BF16) for ki, gl, g in zip(k, gc_last, gc)]
        top = [_dot_tn(x, y) for x, y in zip(k_st, wu)]
        bot = [_dot(x, y) for x, y in zip(attn, wu)]
        for i in range(len(units)):
            slot = cp * len(units) + i
            lhs_ref[slot, 0:B_HEAD_DIM, :] = (-top[i][:, :B_HEAD_DIM]).astype(BF16)
            lhs_ref[slot, B_HEAD_DIM:, :] = (q[i] * exp_gc[i] - bot[i][:, :B_HEAD_DIM]).astype(BF16)
            add_ref[slot, 0:B_HEAD_DIM, :] = top[i][:, B_HEAD_DIM:]
            add_ref[slot, B_HEAD_DIM:, :] = bot[i][:, B_HEAD_DIM:]
            gl_ref[slot] = jnp.broadcast_to(jnp.exp(gc_last[i]), (SUBLANES, LANES))
        return carry

    lax.fori_loop(0, n_chunks // GDN_BUILD_CHUNKS, build, 0)

    def scan(c, carry):
        rows = pl.ds(pl.multiple_of(c * CHUNK, CHUNK), CHUNK)
        states = [state_ref[i] for i in range(len(heads))]
        res = [_dot(lhs_ref[c * len(heads) + i], states[i].astype(BF16)) + add_ref[c * len(heads) + i]
               for i in range(len(heads))]
        for i, (b, h) in enumerate(heads):
            state_ref[i] = gl_ref[c * len(heads) + i][0:1, :] * states[i] + res[i][:B_HEAD_DIM]
            zz = z_ref[b, rows, head_cols(0, h)]
            y = _rms(res[i][B_HEAD_DIM:], nw_ref[...]) * (zz * jax.nn.sigmoid(zz))
            o_ref[b, rows, head_cols(0, h)] = y.astype(o_ref.dtype)
        return carry

    lax.fori_loop(0, n_chunks, scan, 0)


def _gated_delta(qkv_raw, small, z, conv_w, a_log, dt_bias, norm_w, bsz, seq_len):
    par = jnp.zeros((2, LANES), F32)
    par = par.at[0, B_HEADS:2 * B_HEADS].set(a_log.astype(F32))
    par = par.at[1, B_HEADS:2 * B_HEADS].set(dt_bias.astype(F32))
    blk = lambda w: pl.BlockSpec((bsz, SCAN_ROWS, w), lambda t: (0, t, 0))
    n_units = bsz * B_HEADS
    n_slots = n_units * (SCAN_ROWS // CHUNK)
    out = pl.pallas_call(
        _gdn_kernel, grid=(seq_len // SCAN_ROWS,),
        in_specs=[blk(3 * B_W), blk(LANES), blk(B_W), _resident((CONV_K, 3 * B_W)), _resident((2, LANES)),
                  _resident((1, B_HEAD_DIM))],
        out_specs=blk(B_W),
        out_shape=jax.ShapeDtypeStruct((bsz, seq_len, B_W), BF16),
        scratch_shapes=[pltpu.VMEM((bsz, SUBLANES + SCAN_ROWS, 3 * B_W), F32),
                        pltpu.VMEM((n_units, B_HEAD_DIM, B_HEAD_DIM), F32),
                        pltpu.VMEM((n_slots, B_HEAD_DIM + CHUNK, B_HEAD_DIM), BF16),
                        pltpu.VMEM((n_slots, B_HEAD_DIM + CHUNK, B_HEAD_DIM), F32),
                        pltpu.VMEM((n_slots, SUBLANES, LANES), F32)],
        compiler_params=_cparams("arbitrary"), name="gated_delta",
    )(qkv_raw.reshape(bsz, seq_len, 3 * B_W), small.reshape(bsz, seq_len, LANES),
      z.reshape(bsz, seq_len, B_W), conv_w.astype(F32), par, norm_w.reshape(1, B_HEAD_DIM).astype(F32))
    return out.reshape(bsz * seq_len, B_W)


def _ssd_kernel(xbc_ref, sm_ref, z_ref, cw_ref, cb_ref, par_ref, skip_ref, nw_ref, o_ref, pad_ref, state_ref):
    bsz, rows_per_step = xbc_ref.shape[0], xbc_ref.shape[1]
    group_w = C_W // C_GROUPS
    heads_per_group = C_HEADS // C_GROUPS
    pairs_per_group = heads_per_group // 2

    @pl.when(pl.program_id(0) == 0)
    def _():
        state_ref[...] = jnp.zeros_like(state_ref)

    _stage_conv_input(xbc_ref, pad_ref)
    causal, _ = _tril_masks(CHUNK)
    tril = jnp.where(causal, 1.0, 0.0).astype(BF16)
    a_row = -jnp.exp(par_ref[0:1, :])
    dt_bias_row = par_ref[1:2, :]
    lane = lax.broadcasted_iota(jnp.int32, (CHUNK, LANES), 1)
    low_half = lane < C_HEAD_DIM

    def per_head_lanes(cols, h0):
        return jnp.where(low_half, cols[:, h0:h0 + 1], cols[:, h0 + 1:h0 + 2])

    def pair_cols(pr):
        return slice(pr * LANES, (pr + 1) * LANES)

    def chunk_pair(cp, carry):
        units = [(cc, b) for cc in range(SSD_CHUNKS) for b in range(bsz)]
        groups = [(u, g) for u in units for g in range(C_GROUPS)]
        row0 = [pl.multiple_of((cp * SSD_CHUNKS + cc) * CHUNK, CHUNK) for cc in range(SSD_CHUNKS)]
        conv = lambda u, cols: _conv_silu_rows(pad_ref, cw_ref, u[1], row0[u[0]], cols, cb_ref)
        sm = {u: sm_ref[u[1], pl.ds(row0[u[0]], CHUNK), :] for u in units}
        dt_all = {u: jax.nn.softplus(sm[u] + dt_bias_row) for u in units}
        da_cs = {u: _cumsum_rows(tril, dt_all[u] * a_row) for u in units}
        da_cs_t = {u: da_cs[u].T for u in units}
        dt_t = {u: dt_all[u].T for u in units}
        da_last = {u: da_cs[u][CHUNK - 1:CHUNK, :] for u in units}
        exp_da = {u: jnp.exp(da_cs[u]) for u in units}
        to_end = {u: jnp.exp(da_last[u] - da_cs[u]) * dt_all[u] for u in units}
        chunk_decay = {u: jnp.exp(da_last[u]) for u in units}
        x2 = {(u, pr): conv(u, pair_cols(pr)) for u in units for pr in range(C_HEADS // 2)}
        bm = {(u, g): conv(u, slice(C_W + g * C_STATE, C_W + (g + 1) * C_STATE)).astype(BF16) for u, g in groups}
        cm = {(u, g): conv(u, slice(C_W + C_BC + g * C_STATE, C_W + C_BC + (g + 1) * C_STATE)).astype(BF16)
              for u, g in groups}
        cb = {key: _dot_nt(cm[key], bm[key]) for key in groups}
        xw = {(u, g): jnp.concatenate(
            [(x2[u, g * pairs_per_group + j] * per_head_lanes(to_end[u], g * heads_per_group + 2 * j)).astype(BF16)
             for j in range(pairs_per_group)], axis=1) for u, g in groups}
        inflow = {key: _dot_tn(bm[key], xw[key]) for key in groups}
        y_diag = {}
        for u in units:
            for h in range(C_HEADS):
                seg = da_cs[u][:, h:h + 1] - da_cs_t[u][h:h + 1, :]
                wts = cb[u, h // heads_per_group] * jnp.exp(jnp.where(causal, seg, NEG_BIG)) * dt_t[u][h:h + 1, :]
                y_diag[u, h] = _dot(wts.astype(BF16), x2[u, h // 2].astype(BF16))
        for u in units:
            cc, b = u
            rows = pl.ds(row0[cc], CHUNK)
            for g in range(C_GROUPS):
                gl = slice(g * group_w, (g + 1) * group_w)
                prev = state_ref[b, :, gl]
                y_off = _dot(cm[u, g], prev.astype(BF16))
                dec = jnp.concatenate(
                    [jnp.where(low_half[0:1], chunk_decay[u][:, h0:h0 + 1], chunk_decay[u][:, h0 + 1:h0 + 2])
                     for h0 in range(g * heads_per_group, (g + 1) * heads_per_group, 2)], axis=1)
                state_ref[b, :, gl] = prev * dec + inflow[u, g]
                ys = []
                for j in range(pairs_per_group):
                    pr = g * pairs_per_group + j
                    h0 = 2 * pr
                    y = jnp.where(low_half, y_diag[u, h0], y_diag[u, h0 + 1])
                    y = y + y_off[:, pair_cols(j)] * per_head_lanes(exp_da[u], h0) + skip_ref[:, pair_cols(pr)] * x2[u, pr]
                    zz = z_ref[b, rows, pair_cols(pr)]
                    ys.append(y * (zz * jax.nn.sigmoid(zz)))
                yg = jnp.concatenate(ys, axis=1)
                o_ref[b, rows, gl] = _rms(yg, nw_ref[:, gl]).astype(o_ref.dtype)
        return carry

    lax.fori_loop(0, rows_per_step // (CHUNK * SSD_CHUNKS), chunk_pair, 0)


def _ssd(xbc_raw, small, z, conv_w, conv_b, dt_bias, a_log, d_skip, norm_w, bsz, seq_len):
    par = jnp.zeros((2, LANES), F32)
    par = par.at[0, :C_HEADS].set(a_log.astype(F32))
    par = par.at[1, :C_HEADS].set(dt_bias.astype(F32))
    skip = jnp.repeat(d_skip.astype(F32), C_HEAD_DIM).reshape(1, C_W)
    width = C_W + 2 * C_BC
    blk = lambda w: pl.BlockSpec((bsz, SCAN_ROWS, w), lambda t: (0, t, 0))
    out = pl.pallas_call(
        _ssd_kernel, grid=(seq_len // SCAN_ROWS,),
        in_specs=[blk(width), blk(LANES), blk(C_W), _resident((CONV_K, width)), _resident((1, width)),
                  _resident((2, LANES)), _resident((1, C_W)), _resident((1, C_W))],
        out_specs=blk(C_W),
        out_shape=jax.ShapeDtypeStruct((bsz, seq_len, C_W), BF16),
        scratch_shapes=[pltpu.VMEM((bsz, SUBLANES + SCAN_ROWS, width), F32),
                        pltpu.VMEM((bsz, C_STATE, C_W), F32)],
        compiler_params=_cparams("arbitrary"), name="ssd",
    )(xbc_raw.reshape(bsz, seq_len, width), small.reshape(bsz, seq_len, LANES), z.reshape(bsz, seq_len, C_W),
      conv_w.astype(F32), conv_b.astype(F32).reshape(1, width), par, skip, norm_w.reshape(1, C_W).astype(F32))
    return out.reshape(bsz * seq_len, C_W)


FOX_F_PIECES = 3


def _fox_f_lane(h):
    return (h // 2) * LANES + (D_HEAD_DIM if h % 2 == 0 else 0)


def _fox_prep_kernel(sm_ref, fb_ref, place_ref, fk_ref, carry_ref):
    tm = sm_ref.shape[0]

    @pl.when(pl.program_id(1) == 0)
    def _():
        carry_ref[...] = jnp.zeros_like(carry_ref)

    causal, _ = _tril_masks(tm)
    tril = jnp.where(causal, 1.0, 0.0).astype(BF16)
    log_f = jax.nn.log_sigmoid(sm_ref[...] + fb_ref[...])
    f_cum = _cumsum_rows(tril, log_f) + carry_ref[...]
    carry_ref[...] = f_cum[tm - 1:tm, :]
    pieces = jnp.concatenate(_split3(-LOG2_E * f_cum), axis=1)
    fk_ref[...] = _dot(pieces, place_ref[...]).astype(fk_ref.dtype)


def _fox_prep(small, f_bias, bsz, seq_len):
    fb = jnp.zeros((1, LANES), F32).at[0, D_HEADS:2 * D_HEADS].set(f_bias.astype(F32))
    place = np.zeros((FOX_F_PIECES * LANES, D_W), np.float32)
    for h in range(D_HEADS):
        for piece in range(FOX_F_PIECES):
            place[piece * LANES + D_HEADS + h, _fox_f_lane(h) + piece] = 1.0
    nt = seq_len // FOX_T
    return pl.pallas_call(
        _fox_prep_kernel, grid=(bsz, nt),
        in_specs=[pl.BlockSpec((None, FOX_T, LANES), lambda b, t: (b, t, 0)), _resident((1, LANES)),
                  _resident(place.shape)],
        out_specs=pl.BlockSpec((None, FOX_T, D_W), lambda b, t: (b, t, 0)),
        out_shape=jax.ShapeDtypeStruct((bsz, seq_len, D_W), BF16),
        scratch_shapes=[pltpu.VMEM((1, LANES), F32)],
        compiler_params=_cparams("parallel", "arbitrary"), name="fox_prep",
    )(small.reshape(bsz, seq_len, LANES), fb, jnp.asarray(place, BF16))


def _fox_kernel(q_ref, k_ref, vt_ref, fk_ref, o_ref,
                kk_ref, vv_ref, sa_ref, sb_ref, pa_ref, pb_ref, aa_ref, ab_ref, m_ref, acc_ref):
    i = pl.program_id(2)
    t = FOX_T
    nt = k_ref.shape[0] // t
    n_rb = t // FOX_ROWS
    lane = lax.broadcasted_iota(jnp.int32, (t, LANES), 1)
    low_half = lane < D_HEAD_DIM
    top_half = lax.broadcasted_iota(jnp.int32, (LANES, t), 0) < D_HEAD_DIM

    @pl.when(i == 0)
    def _():
        for jt in range(nt):
            rows = slice(jt * t, (jt + 1) * t)
            kt, ft = k_ref[rows, :], fk_ref[rows, :]
            kk_ref[0, jt] = jnp.where(low_half, kt, ft)
            kk_ref[1, jt] = jnp.where(low_half, ft, kt)
            vt = vt_ref[jt]
            one = jnp.ones_like(vt)
            vv_ref[0, jt] = jnp.where(top_half, vt, one)
            vv_ref[1, jt] = jnp.where(top_half, one, vt)
        kk_ref[0, nt] = jnp.where(low_half, 0.0, NEG_BIG).astype(BF16)
        kk_ref[1, nt] = jnp.where(low_half, NEG_BIG, 0.0).astype(BF16)

    q = q_ref[...]
    ones_upto = lambda n: jnp.where(lane < n, 1.0, 0.0).astype(BF16)
    qs = (jnp.where(low_half, q, ones_upto(_fox_f_lane(0) + FOX_F_PIECES)),
          jnp.where(low_half, ones_upto(_fox_f_lane(1) + FOX_F_PIECES), q))
    key_in_blk = lax.broadcasted_iota(jnp.int32, (FOX_ROWS, t), 0)
    query = lax.broadcasted_iota(jnp.int32, (FOX_ROWS, t), 1)

    m_ref[...] = jnp.full(m_ref.shape, NEG_BIG, F32)
    acc_ref[...] = jnp.zeros_like(acc_ref)

    def scores(j, s_ref):
        jj = jnp.where(j >= 0, j, nt)
        for half in range(2):
            s_ref[half] = _dot_nt(kk_ref[half, jj], qs[half])

    def softmax_cols(s_ref, p_ref, a_ref, diagonal):
        def block(half, rb):
            s = s_ref[half, rb * FOX_ROWS:(rb + 1) * FOX_ROWS, :]
            if diagonal:
                s = jnp.where(key_in_blk + rb * FOX_ROWS <= query, s, NEG_BIG)
            return s

        tile_max = []
        for half in range(2):
            m8 = None
            for rb in range(n_rb):
                s = block(half, rb)
                b8 = jnp.maximum(jnp.maximum(s[0:8], s[8:16]), jnp.maximum(s[16:24], s[24:32]))
                m8 = b8 if m8 is None else jnp.maximum(m8, b8)
            tile_max.append(jnp.max(m8, axis=0, keepdims=True))
        for half in range(2):
            m_old = m_ref[half]
            m_new = jnp.maximum(m_old, tile_max[half])
            m_ref[half] = m_new
            a_ref[half] = jnp.exp2(m_old - m_new)
            for rb in range(n_rb):
                p_ref[half, rb * FOX_ROWS:(rb + 1) * FOX_ROWS, :] = jnp.exp2(block(half, rb) - m_new).astype(BF16)

    def accumulate(j, p_ref, a_ref):
        jj = jnp.maximum(j, 0)
        for half in range(2):
            acc_ref[half] = a_ref[half] * acc_ref[half] + _dot(vv_ref[half, jj], p_ref[half])

    scores(i, sb_ref)
    scores(i - 1, sa_ref)
    softmax_cols(sb_ref, pb_ref, ab_ref, True)

    def pair(n, carry):
        j = i - 1 - 2 * n
        scores(j - 1, sb_ref)
        softmax_cols(sa_ref, pa_ref, aa_ref, False)
        accumulate(j + 1, pb_ref, ab_ref)
        scores(j - 2, sa_ref)
        softmax_cols(sb_ref, pb_ref, ab_ref, False)
        accumulate(j, pa_ref, aa_ref)
        return carry

    n_pairs = (i + 1) // 2
    lax.fori_loop(0, n_pairs, pair, 0)
    accumulate(i - 2 * n_pairs, pb_ref, ab_ref)
    acc0, acc1 = acc_ref[0], acc_ref[1]
    out_t = jnp.where(top_half, acc0 / acc0[D_HEAD_DIM:D_HEAD_DIM + 1, :], acc1 / acc1[0:1, :])
    o_ref[...] = out_t.T.astype(o_ref.dtype)


def _fox_attention(q, k, v_t, fk, bsz, seq_len):
    nt = seq_len // FOX_T
    pairs = D_W // LANES
    q3, k3 = (a.reshape(bsz, seq_len, D_W) for a in (q, k))
    whole_seq = pl.BlockSpec((None, seq_len, LANES), lambda b, p, i: (b, 0, p))
    out = pl.pallas_call(
        _fox_kernel, grid=(bsz, pairs, nt),
        in_specs=[pl.BlockSpec((None, FOX_T, LANES), lambda b, p, i: (b, i, p)), whole_seq,
                  pl.BlockSpec((nt, LANES, FOX_T), lambda b, p, i: (b, p, 0)), whole_seq],
        out_specs=pl.BlockSpec((None, FOX_T, LANES), lambda b, p, i: (b, i, p)),
        out_shape=jax.ShapeDtypeStruct((bsz, seq_len, D_W), BF16),
        scratch_shapes=[pltpu.VMEM((2, nt + 1, FOX_T, LANES), BF16), pltpu.VMEM((2, nt, LANES, FOX_T), BF16)]
        + [pltpu.VMEM((2, FOX_T, FOX_T), F32)] * 2 + [pltpu.VMEM((2, FOX_T, FOX_T), BF16)] * 2
        + [pltpu.VMEM((2, 1, FOX_T), F32)] * 3 + [pltpu.VMEM((2, LANES, FOX_T), F32)],
        compiler_params=_cparams("parallel", "parallel", "arbitrary"), name="fox_attention",
    )(q3, k3, v_t, fk)
    return out.reshape(bsz * seq_len, D_W)


def _pad_cols(w, width=LANES):
    return jnp.pad(w, ((0, 0), (0, width - w.shape[1])))


def kernel(x, norm_mix, norm_ffn, norm_final, ffn_w_gate, ffn_w_up, ffn_w_down, ab_w_in, ab_rel_bias, ab_conv_w, ab_a_log, ab_dt_bias, ab_norm_w, ab_w_out, cd_w_in, cd_conv_w, cd_conv_b, cd_dt_bias, cd_a_log, cd_d_skip, cd_norm_w, cd_f_bias, cd_w_out):
    bsz, seq_len, d = x.shape
    n = bsz * seq_len
    xf = x.reshape(n, d)
    bf = lambda w: w.astype(BF16)

    w_in = ab_w_in[0]
    o = np.cumsum([0, A_W, A_W, A_W, 3 * B_W, B_HEADS, B_HEADS, B_W])
    weights = [bf(w_in[:, o[0]:o[1]] * (LOG2_E * A_HEAD_DIM ** -0.5)), bf(w_in[:, o[1]:o[2]]), bf(w_in[:, o[2]:o[3]].T),
               bf(w_in[:, o[3]:o[4]]), bf(_pad_cols(w_in[:, o[4]:o[6]])), bf(w_in[:, o[6]:o[7]])]
    a_q, a_k, a_vt, b_qkv, b_small, b_z = _norm_proj(
        xf, norm_mix[0], weights, [BF16, BF16, BF16, F32, F32, F32], transposed=(False, False, True) + (False,) * 3)
    o_a = _band_attention(a_q, a_k, a_vt, ab_rel_bias[0], bsz, seq_len)
    o_b = _gated_delta(b_qkv, b_small, b_z, ab_conv_w[0], ab_a_log[0], ab_dt_bias[0], ab_norm_w[0], bsz, seq_len)
    ffn_w = (bf(ffn_w_gate), bf(ffn_w_up), bf(ffn_w_down))
    xf = _layer_tail(xf, o_a, o_b, bf(ab_w_out), norm_ffn[0], *ffn_w, 0, norm_final, False)

    w_in = cd_w_in[0]
    o = np.cumsum([0, C_W, C_W + 2 * C_BC, C_HEADS, D_W, D_W, D_W, D_HEADS])
    small_w = jnp.concatenate([w_in[:, o[2]:o[3]], w_in[:, o[6]:o[7]]], axis=1)
    weights = [bf(w_in[:, o[0]:o[1]]), bf(w_in[:, o[1]:o[2]]), bf(_pad_cols(small_w)),
               bf(w_in[:, o[3]:o[4]] * (LOG2_E * D_HEAD_DIM ** -0.5)), bf(w_in[:, o[4]:o[5]]), bf(w_in[:, o[5]:o[6]].T)]
    c_z, c_xbc, cd_small, d_q, d_k, d_vt = _norm_proj(
        xf, norm_mix[1], weights, [F32, F32, F32, BF16, BF16, BF16], transposed=(False,) * 5 + (True,))
    y_c = _ssd(c_xbc, cd_small, c_z, cd_conv_w[0], cd_conv_b[0], cd_dt_bias[0], cd_a_log[0], cd_d_skip[0],
               cd_norm_w[0], bsz, seq_len)
    fk = _fox_prep(cd_small, cd_f_bias[0], bsz, seq_len)
    o_d = _fox_attention(d_q, d_k, d_vt, fk, bsz, seq_len)
    xf = _layer_tail(xf, y_c, o_d, bf(cd_w_out), norm_ffn[1], *ffn_w, 1, norm_final, True)
    return xf.reshape(bsz, seq_len, d)
```

```python
import functools

import jax
import jax.numpy as jnp
import numpy as np
from jax import lax
from jax.experimental import pallas as pl
from jax.experimental.pallas import tpu as pltpu

F32 = jnp.float32
BF16 = jnp.bfloat16

D_MODEL = 1024
CHUNK = 64
EPS = 1e-6
CONV_K = 4
A_HEADS, A_HEAD_DIM, A_LEFT_CHUNKS, A_MAX_REL = 8, 64, 8, 256
B_HEADS, B_HEAD_DIM = 4, 128
C_HEADS, C_HEAD_DIM, C_GROUPS, C_STATE = 8, 64, 2, 128
D_HEADS, D_HEAD_DIM = 8, 64
A_W = A_HEADS * A_HEAD_DIM
B_W = B_HEADS * B_HEAD_DIM
C_W = C_HEADS * C_HEAD_DIM
D_W = D_HEADS * D_HEAD_DIM
C_BC = C_GROUPS * C_STATE

LANES = 128
SUBLANES = 8
VMEM_LIMIT_BYTES = 56 * 1024 * 1024
NEG_BIG = -1e30
LOG2_E = float(np.log2(np.e))

ROW_TILE = 512
TAIL_ROWS = 1024
FFN_CHUNK = 256
BAND_TQ = 256
BAND_WIN = BAND_TQ + A_LEFT_CHUNKS * CHUNK
FOX_T = 512
FOX_ROWS = 32
SCAN_ROWS = 512
SSD_CHUNKS = 4
GDN_BUILD_CHUNKS = 4


def _cparams(*sem):
    return pltpu.CompilerParams(dimension_semantics=sem, vmem_limit_bytes=VMEM_LIMIT_BYTES)


def _resident(shape):
    nd = len(shape)
    return pl.BlockSpec(shape, lambda *_: (0,) * nd, pipeline_mode=pl.Buffered(1))


def _dot(a, b):
    return jnp.dot(a, b, preferred_element_type=F32)


def _dot_nt(a, b):
    return lax.dot_general(a, b, (((1,), (1,)), ((), ())), preferred_element_type=F32)


def _dot_tn(a, b):
    return lax.dot_general(a, b, (((0,), (0,)), ((), ())), preferred_element_type=F32)


def _rms(x, w):
    return x * lax.rsqrt(jnp.mean(x * x, axis=-1, keepdims=True) + EPS) * w


def _split3(x):
    hi = x.astype(BF16)
    r1 = x - hi.astype(F32)
    mid = r1.astype(BF16)
    lo = (r1 - mid.astype(F32)).astype(BF16)
    return hi, mid, lo


def _cumsum_rows(tril, x):
    hi, mid, lo = _split3(x)
    return _dot(tril, hi) + _dot(tril, mid) + _dot(tril, lo)


def _tril_masks(n):
    r = lax.broadcasted_iota(jnp.int32, (n, n), 0)
    c = lax.broadcasted_iota(jnp.int32, (n, n), 1)
    return r >= c, r > c


def _norm_proj_kernel(x_ref, nw_ref, *refs, transposed):
    n_out = len(refs) // 2
    h = _rms(x_ref[...], nw_ref[...]).astype(BF16)
    for w_ref, o_ref, tr in zip(refs[:n_out], refs[n_out:], transposed):
        out = _dot_nt(w_ref[...], h) if tr else _dot(h, w_ref[...])
        o_ref[...] = out.astype(o_ref.dtype)


def _norm_proj(x, norm_w, weights, out_dtypes, transposed=None):
    n, d = x.shape
    transposed = transposed or (False,) * len(weights)
    in_specs = [pl.BlockSpec((ROW_TILE, d), lambda i: (i, 0)), _resident((1, d))]
    in_specs += [_resident(w.shape) for w in weights]
    out_shape, out_specs = [], []
    for w, dt, tr in zip(weights, out_dtypes, transposed):
        if tr:
            out_shape.append(jax.ShapeDtypeStruct((n // ROW_TILE, w.shape[0], ROW_TILE), dt))
            out_specs.append(pl.BlockSpec((None, w.shape[0], ROW_TILE), lambda i: (i, 0, 0)))
        else:
            out_shape.append(jax.ShapeDtypeStruct((n, w.shape[1]), dt))
            out_specs.append(pl.BlockSpec((ROW_TILE, w.shape[1]), lambda i: (i, 0)))
    return pl.pallas_call(
        functools.partial(_norm_proj_kernel, transposed=tuple(transposed)),
        grid=(n // ROW_TILE,), in_specs=in_specs, out_specs=out_specs,
        out_shape=out_shape, compiler_params=_cparams("parallel"), name="norm_proj",
    )(x, norm_w.reshape(1, d), *weights)


def _tail_kernel(x_ref, a_ref, b_ref, woa_ref, wob_ref, nw_ref, wg_ref, wu_ref, wd_ref, fn_ref,
                 o_ref, acc_ref, *, d_ff, final_norm):
    x1 = x_ref[...] + _dot(a_ref[...], woa_ref[...]) + _dot(b_ref[...], wob_ref[...])
    h = _rms(x1, nw_ref[...]).astype(BF16)
    acc_ref[...] = x1
    ffn = None
    for c in range(d_ff // FFN_CHUNK):
        cols = slice(c * FFN_CHUNK, (c + 1) * FFN_CHUNK)
        g = _dot(h, wg_ref[:, cols])
        u = _dot(h, wu_ref[:, cols])
        part = _dot((g * jax.nn.sigmoid(g) * u).astype(BF16), wd_ref[cols, :])
        ffn = part if ffn is None else ffn + part
    y = acc_ref[...] + ffn
    if final_norm:
        y = _rms(y, fn_ref[...])
    o_ref[...] = y


def _layer_tail(x, mix_a, mix_b, w_out, norm_w, wg, wu, wd, layer, final_w, final_norm):
    n, d = x.shape
    d_ff = wg.shape[2]
    wa, wb = mix_a.shape[1], mix_b.shape[1]
    assert wa == wb and w_out.shape == (1, wa + wb, d)
    row = lambda w: pl.BlockSpec((TAIL_ROWS, w), lambda i: (i, 0))
    pick = lambda shape, idx: pl.BlockSpec((None,) + shape, lambda i: idx, pipeline_mode=pl.Buffered(1))
    in_specs = [row(d), row(wa), row(wb), pick((wa, d), (0, 0, 0)), pick((wb, d), (0, 1, 0)), _resident((1, d)),
                pick((d, d_ff), (layer, 0, 0)), pick((d, d_ff), (layer, 0, 0)), pick((d_ff, d), (layer, 0, 0)),
                _resident((1, d))]
    return pl.pallas_call(
        functools.partial(_tail_kernel, d_ff=d_ff, final_norm=final_norm),
        grid=(n // TAIL_ROWS,), in_specs=in_specs, out_specs=row(d),
        out_shape=jax.ShapeDtypeStruct((n, d), F32),
        scratch_shapes=[pltpu.VMEM((TAIL_ROWS, d), F32)],
        compiler_params=_cparams("parallel"), name="layer_tail",
    )(x, mix_a, mix_b, w_out, w_out, norm_w.reshape(1, d), wg, wu, wd, final_w.reshape(1, d))


def _band_kernel(q_ref, *refs):
    n_blk = BAND_WIN // BAND_TQ
    k_refs, vt_refs = refs[:n_blk], refs[n_blk:2 * n_blk]
    by_offset_ref, o_ref, bias_ref, s_ref, p_ref = refs[2 * n_blk:]
    i = pl.program_id(1)

    @pl.when((pl.program_id(0) == 0) & (i == 0))
    def _():
        _fill_band_bias(by_offset_ref, bias_ref)

    lane = lax.broadcasted_iota(jnp.int32, (BAND_TQ, LANES), 1)
    low_half = lane < A_HEAD_DIM
    top_half = lax.broadcasted_iota(jnp.int32, (LANES, BAND_WIN), 0) < A_HEAD_DIM
    pairs = A_W // LANES
    n_rb = BAND_WIN // FOX_ROWS
    fill = [jnp.where(i - (n_blk - 1) + j >= 0, 0.0, NEG_BIG) for j in range(n_blk)]
    one_at = lambda n: jnp.where(lane == n, 1.0, 0.0).astype(BF16)

    v_aug = []
    for pr in range(pairs):
        cols = slice(pr * LANES, (pr + 1) * LANES)
        q = q_ref[:, cols]
        k_blocks = [r[:, cols] for r in k_refs]
        k_even = jnp.concatenate([jnp.where(low_half, kb, f.astype(BF16)) for kb, f in zip(k_blocks, fill)], axis=0)
        k_odd = jnp.concatenate([jnp.where(low_half, f.astype(BF16), kb) for kb, f in zip(k_blocks, fill)], axis=0)
        s_ref[2 * pr] = _dot_nt(k_even, jnp.where(low_half, q, one_at(A_HEAD_DIM))) + bias_ref[2 * pr]
        s_ref[2 * pr + 1] = _dot_nt(k_odd, jnp.where(low_half, one_at(0), q)) + bias_ref[2 * pr + 1]
        vt = jnp.concatenate([r[cols, :] for r in vt_refs], axis=1)
        one = jnp.ones_like(vt)
        v_aug += [jnp.where(top_half, vt, one), jnp.where(top_half, one, vt)]

    rows = lambda rb: slice(rb * FOX_ROWS, (rb + 1) * FOX_ROWS)
    col_max = []
    for h in range(A_HEADS):
        m8 = None
        for rb in range(n_rb):
            s = s_ref[h, rows(rb), :]
            b8 = jnp.maximum(jnp.maximum(s[0:8], s[8:16]), jnp.maximum(s[16:24], s[24:32]))
            m8 = b8 if m8 is None else jnp.maximum(m8, b8)
        col_max.append(jnp.max(m8, axis=0, keepdims=True))
    for h in range(A_HEADS):
        for rb in range(n_rb):
            p_ref[h, rows(rb), :] = jnp.exp2(s_ref[h, rows(rb), :] - col_max[h]).astype(BF16)

    out_top = lax.broadcasted_iota(jnp.int32, (LANES, BAND_TQ), 0) < A_HEAD_DIM
    for pr in range(pairs):
        o_even = _dot(v_aug[2 * pr], p_ref[2 * pr])
        o_odd = _dot(v_aug[2 * pr + 1], p_ref[2 * pr + 1])
        out_t = jnp.where(out_top, o_even / o_even[A_HEAD_DIM:A_HEAD_DIM + 1, :], o_odd / o_odd[0:1, :])
        o_ref[:, pr * LANES:(pr + 1) * LANES] = out_t.T.astype(o_ref.dtype)


BAND_BIAS_PERIOD = BAND_WIN + BAND_TQ


def _band_bias_by_offset(rel_bias):
    d = np.arange(BAND_BIAS_PERIOD) - BAND_WIN
    idx = np.clip(d + A_LEFT_CHUNKS * CHUNK, -A_MAX_REL, A_MAX_REL) + A_MAX_REL
    return rel_bias.astype(F32)[:, idx] * LOG2_E


def _fill_band_bias(by_offset_ref, bias_ref):
    kc = lax.broadcasted_iota(jnp.int32, (BAND_WIN, BAND_TQ), 0) // CHUNK
    qc = lax.broadcasted_iota(jnp.int32, (BAND_WIN, BAND_TQ), 1) // CHUNK
    for h in range(A_HEADS):
        spread = jnp.broadcast_to(by_offset_ref[h:h + 1, :], (BAND_WIN, BAND_BIAS_PERIOD))
        skewed = pltpu.roll(spread, 0, axis=1, stride=1, stride_axis=0)[:, BAND_WIN:]
        bias_ref[h] = jnp.where(kc >= qc, jnp.where(kc <= qc + A_LEFT_CHUNKS, skewed, NEG_BIG), NEG_BIG)


def _band_attention(q, k, v_t, rel_bias, bsz, seq_len):
    n_blk = BAND_WIN // BAND_TQ
    nq = seq_len // BAND_TQ
    per_tile = ROW_TILE // BAND_TQ
    tiles_per_seq = seq_len // ROW_TILE
    q3, k3 = (a.reshape(bsz, seq_len, A_W) for a in (q, k))
    blk = (None, BAND_TQ, A_W)
    key_blk = lambda i, j: jnp.maximum(i - (n_blk - 1) + j, 0)
    k_specs = [pl.BlockSpec(blk, lambda b, i, j=j: (b, key_blk(i, j), 0)) for j in range(n_blk)]
    vt_specs = [pl.BlockSpec((None, A_W, BAND_TQ),
                             lambda b, i, j=j: (b * tiles_per_seq + key_blk(i, j) // per_tile, 0, key_blk(i, j) % per_tile))
                for j in range(n_blk)]
    out = pl.pallas_call(
        _band_kernel, grid=(bsz, nq),
        in_specs=[pl.BlockSpec(blk, lambda b, i: (b, i, 0))] + k_specs + vt_specs
        + [_resident((A_HEADS, BAND_BIAS_PERIOD))],
        out_specs=pl.BlockSpec(blk, lambda b, i: (b, i, 0)),
        out_shape=jax.ShapeDtypeStruct((bsz, seq_len, A_W), BF16),
        scratch_shapes=[pltpu.VMEM((A_HEADS, BAND_WIN, BAND_TQ), F32)] * 2
        + [pltpu.VMEM((A_HEADS, BAND_WIN, BAND_TQ), BF16)],
        compiler_params=_cparams("arbitrary", "arbitrary"), name="band_attention",
    )(q3, *([k3] * n_blk), *([v_t] * n_blk), _band_bias_by_offset(rel_bias))
    return out.reshape(bsz * seq_len, A_W)


def _stage_conv_input(x_ref, pad_ref):
    rows = x_ref.shape[1]

    @pl.when(pl.program_id(0) == 0)
    def _():
        pad_ref[:, 0:SUBLANES, :] = jnp.zeros((pad_ref.shape[0], SUBLANES, pad_ref.shape[2]), F32)

    @pl.when(pl.program_id(0) > 0)
    def _():
        pad_ref[:, 0:SUBLANES, :] = pad_ref[:, rows:rows + SUBLANES, :]

    pad_ref[:, SUBLANES:, :] = x_ref[...]


def _conv_silu_rows(pad_ref, w_ref, b, row0, cols, bias_ref=None):
    window = pad_ref[b, pl.ds(row0, SUBLANES + CHUNK), cols]
    acc = None
    for back in range(CONV_K):
        term = w_ref[CONV_K - 1 - back:CONV_K - back, cols] * window[SUBLANES - back:SUBLANES - back + CHUNK]
        acc = term if acc is None else acc + term
    if bias_ref is not None:
        acc = acc + bias_ref[:, cols]
    return acc * jax.nn.sigmoid(acc)


def _gdn_kernel(qkv_ref, sm_ref, z_ref, cw_ref, par_ref, nw_ref, o_ref,
                pad_ref, state_ref, lhs_ref, add_ref, gl_ref):
    bsz, rows_per_step = qkv_ref.shape[0], qkv_ref.shape[1]
    n_chunks = rows_per_step // CHUNK
    heads = [(b, h) for b in range(bsz) for h in range(B_HEADS)]
    units = [(cc, b, h) for cc in range(GDN_BUILD_CHUNKS) for b, h in heads]

    @pl.when(pl.program_id(0) == 0)
    def _():
        state_ref[...] = jnp.zeros_like(state_ref)

    _stage_conv_input(qkv_ref, pad_ref)
    causal, strict = _tril_masks(CHUNK)
    tril = jnp.where(causal, 1.0, 0.0).astype(BF16)
    eye = jnp.where(causal & ~strict, 1.0, 0.0)
    a_row = -jnp.exp(par_ref[0:1, :])
    dt_bias_row = par_ref[1:2, :]

    def head_cols(part, h):
        return slice(part * B_W + h * B_HEAD_DIM, part * B_W + (h + 1) * B_HEAD_DIM)

    def build(cp, carry):
        row0 = [pl.multiple_of((cp * GDN_BUILD_CHUNKS + cc) * CHUNK, CHUNK) for cc in range(GDN_BUILD_CHUNKS)]
        sm = {(cc, b): sm_ref[b, pl.ds(row0[cc], CHUNK), :] for cc in range(GDN_BUILD_CHUNKS) for b in range(bsz)}
        beta_all = {key: jax.nn.sigmoid(x) for key, x in sm.items()}
        gc_all = {key: _cumsum_rows(tril, a_row * jax.nn.softplus(x + dt_bias_row)) for key, x in sm.items()}
        gc_all_t = {key: x.T for key, x in gc_all.items()}
        q, k, v = ([_conv_silu_rows(pad_ref, cw_ref, b, row0[cc], head_cols(part, h)) for cc, b, h in units]
                   for part in range(3))
        q = [x * lax.rsqrt(jnp.sum(x * x, axis=-1, keepdims=True) + EPS) * (B_HEAD_DIM ** -0.5) for x in q]
        k = [x * lax.rsqrt(jnp.sum(x * x, axis=-1, keepdims=True) + EPS) for x in k]
        beta = [beta_all[cc, b][:, h:h + 1] for cc, b, h in units]
        gc = [gc_all[cc, b][:, B_HEADS + h:B_HEADS + h + 1] for cc, b, h in units]
        gc_row = [gc_all_t[cc, b][B_HEADS + h:B_HEADS + h + 1, :] for cc, b, h in units]
        decay = [jnp.exp(jnp.where(causal, g - gr, NEG_BIG)) for g, gr in zip(gc, gc_row)]
        kb = [x.astype(BF16) for x in k]
        kk = [_dot_nt(x, x) for x in kb]
        qk = [_dot_nt(x.astype(BF16), y) for x, y in zip(q, kb)]
        a_mat = [jnp.where(strict, bt * x * d, 0.0) for bt, x, d in zip(beta, kk, decay)]
        inv = [eye - a for a in a_mat]
        power = a_mat
        for _ in range(5):
            power = [_dot(x, x) for x in power]
            inv = [i + _dot(i, x) for i, x in zip(inv, power)]
        exp_gc = [jnp.exp(g) for g in gc]
        rhs = [jnp.concatenate([ki * (bt * e), vi * bt], axis=1) for ki, vi, bt, e in zip(k, v, beta, exp_gc)]
        wu = [_dot(i, r).astype(BF16) for i, r in zip(inv, rhs)]
        attn = [(x * d).astype(BF16) for x, d in zip(qk, decay)]
        gc_last = [g[CHUNK - 1:CHUNK, :] for g in gc]
        k_st = [(ki * jnp.exp(gl - g)).astype(BF16) for ki, gl, g in zip(k, gc_last, gc)]
        top = [_dot_tn(x, y) for x, y in zip(k_st, wu)]
        bot = [_dot(x, y) for x, y in zip(attn, wu)]
        for i in range(len(units)):
            slot = cp * len(units) + i
            lhs_ref[slot, 0:B_HEAD_DIM, :] = (-top[i][:, :B_HEAD_DIM]).astype(BF16)
            lhs_ref[slot, B_HEAD_DIM:, :] = (q[i] * exp_gc[i] - bot[i][:, :B_HEAD_DIM]).astype(BF16)
            add_ref[slot, 0:B_HEAD_DIM, :] = top[i][:, B_HEAD_DIM:]
            add_ref[slot, B_HEAD_DIM:, :] = bot[i][:, B_HEAD_DIM:]
            gl_ref[slot] = jnp.broadcast_to(jnp.exp(gc_last[i]), (SUBLANES, LANES))
        return carry

    lax.fori_loop(0, n_chunks // GDN_BUILD_CHUNKS, build, 0)

    def scan(c, carry):
        rows = pl.ds(pl.multiple_of(c * CHUNK, CHUNK), CHUNK)
        states = [state_ref[i] for i in range(len(heads))]
        res = [_dot(lhs_ref[c * len(heads) + i], states[i].astype(BF16)) + add_ref[c * len(heads) + i]
               for i in range(len(heads))]
        for i, (b, h) in enumerate(heads):
            state_ref[i] = gl_ref[c * len(heads) + i][0:1, :] * states[i] + res[i][:B_HEAD_DIM]
            zz = z_ref[b, rows, head_cols(0, h)]
            y = _rms(res[i][B_HEAD_DIM:], nw_ref[...]) * (zz * jax.nn.sigmoid(zz))
            o_ref[b, rows, head_cols(0, h)] = y.astype(o_ref.dtype)
        return carry

    lax.fori_loop(0, n_chunks, scan, 0)


def _gated_delta(qkv_raw, small, z, conv_w, a_log, dt_bias, norm_w, bsz, seq_len):
    par = jnp.zeros((2, LANES), F32)
    par = par.at[0, B_HEADS:2 * B_HEADS].set(a_log.astype(F32))
    par = par.at[1, B_HEADS:2 * B_HEADS].set(dt_bias.astype(F32))
    blk = lambda w: pl.BlockSpec((bsz, SCAN_ROWS, w), lambda t: (0, t, 0))
    n_units = bsz * B_HEADS
    n_slots = n_units * (SCAN_ROWS // CHUNK)
    out = pl.pallas_call(
        _gdn_kernel, grid=(seq_len // SCAN_ROWS,),
        in_specs=[blk(3 * B_W), blk(LANES), blk(B_W), _resident((CONV_K, 3 * B_W)), _resident((2, LANES)),
                  _resident((1, B_HEAD_DIM))],
        out_specs=blk(B_W),
        out_shape=jax.ShapeDtypeStruct((bsz, seq_len, B_W), BF16),
        scratch_shapes=[pltpu.VMEM((bsz, SUBLANES + SCAN_ROWS, 3 * B_W), F32),
                        pltpu.VMEM((n_units, B_HEAD_DIM, B_HEAD_DIM), F32),
                        pltpu.VMEM((n_slots, B_HEAD_DIM + CHUNK, B_HEAD_DIM), BF16),
                        pltpu.VMEM((n_slots, B_HEAD_DIM + CHUNK, B_HEAD_DIM), F32),
                        pltpu.VMEM((n_slots, SUBLANES, LANES), F32)],
        compiler_params=_cparams("arbitrary"), name="gated_delta",
    )(qkv_raw.reshape(bsz, seq_len, 3 * B_W), small.reshape(bsz, seq_len, LANES),
      z.reshape(bsz, seq_len, B_W), conv_w.astype(F32), par, norm_w.reshape(1, B_HEAD_DIM).astype(F32))
    return out.reshape(bsz * seq_len, B_W)


def _ssd_kernel(xbc_ref, sm_ref, z_ref, cw_ref, cb_ref, par_ref, skip_ref, nw_ref, o_ref, pad_ref, state_ref):
    bsz, rows_per_step = xbc_ref.shape[0], xbc_ref.shape[1]
    group_w = C_W // C_GROUPS
    heads_per_group = C_HEADS // C_GROUPS
    pairs_per_group = heads_per_group // 2

    @pl.when(pl.program_id(0) == 0)
    def _():
        state_ref[...] = jnp.zeros_like(state_ref)

    _stage_conv_input(xbc_ref, pad_ref)
    causal, _ = _tril_masks(CHUNK)
    tril = jnp.where(causal, 1.0, 0.0).astype(BF16)
    a_row = -jnp.exp(par_ref[0:1, :])
    dt_bias_row = par_ref[1:2, :]
    lane = lax.broadcasted_iota(jnp.int32, (CHUNK, LANES), 1)
    low_half = lane < C_HEAD_DIM

    def per_head_lanes(cols, h0):
        return jnp.where(low_half, cols[:, h0:h0 + 1], cols[:, h0 + 1:h0 + 2])

    def pair_cols(pr):
        return slice(pr * LANES, (pr + 1) * LANES)

    def chunk_pair(cp, carry):
        units = [(cc, b) for cc in range(SSD_CHUNKS) for b in range(bsz)]
        groups = [(u, g) for u in units for g in range(C_GROUPS)]
        row0 = [pl.multiple_of((cp * SSD_CHUNKS + cc) * CHUNK, CHUNK) for cc in range(SSD_CHUNKS)]
        conv = lambda u, cols: _conv_silu_rows(pad_ref, cw_ref, u[1], row0[u[0]], cols, cb_ref)
        sm = {u: sm_ref[u[1], pl.ds(row0[u[0]], CHUNK), :] for u in units}
        dt_all = {u: jax.nn.softplus(sm[u] + dt_bias_row) for u in units}
        da_cs = {u: _cumsum_rows(tril, dt_all[u] * a_row) for u in units}
        da_cs_t = {u: da_cs[u].T for u in units}
        dt_t = {u: dt_all[u].T for u in units}
        da_last = {u: da_cs[u][CHUNK - 1:CHUNK, :] for u in units}
        exp_da = {u: jnp.exp(da_cs[u]) for u in units}
        to_end = {u: jnp.exp(da_last[u] - da_cs[u]) * dt_all[u] for u in units}
        chunk_decay = {u: jnp.exp(da_last[u]) for u in units}
        x2 = {(u, pr): conv(u, pair_cols(pr)) for u in units for pr in range(C_HEADS // 2)}
        bm = {(u, g): conv(u, slice(C_W + g * C_STATE, C_W + (g + 1) * C_STATE)).astype(BF16) for u, g in groups}
        cm = {(u, g): conv(u, slice(C_W + C_BC + g * C_STATE, C_W + C_BC + (g + 1) * C_STATE)).astype(BF16)
              for u, g in groups}
        cb = {key: _dot_nt(cm[key], bm[key]) for key in groups}
        xw = {(u, g): jnp.concatenate(
            [(x2[u, g * pairs_per_group + j] * per_head_lanes(to_end[u], g * heads_per_group + 2 * j)).astype(BF16)
             for j in range(pairs_per_group)], axis=1) for u, g in groups}
        inflow = {key: _dot_tn(bm[key], xw[key]) for key in groups}
        y_diag = {}
        for u in units:
            for h in range(C_HEADS):
                seg = da_cs[u][:, h:h + 1] - da_cs_t[u][h:h + 1, :]
                wts = cb[u, h // heads_per_group] * jnp.exp(jnp.where(causal, seg, NEG_BIG)) * dt_t[u][h:h + 1, :]
                y_diag[u, h] = _dot(wts.astype(BF16), x2[u, h // 2].astype(BF16))
        for u in units:
            cc, b = u
            rows = pl.ds(row0[cc], CHUNK)
            for g in range(C_GROUPS):
                gl = slice(g * group_w, (g + 1) * group_w)
                prev = state_ref[b, :, gl]
                y_off = _dot(cm[u, g], prev.astype(BF16))
                dec = jnp.concatenate(
                    [jnp.where(low_half[0:1], chunk_decay[u][:, h0:h0 + 1], chunk_decay[u][:, h0 + 1:h0 + 2])
                     for h0 in range(g * heads_per_group, (g + 1) * heads_per_group, 2)], axis=1)
                state_ref[b, :, gl] = prev * dec + inflow[u, g]
                ys = []
                for j in range(pairs_per_group):
                    pr = g * pairs_per_group + j
                    h0 = 2 * pr
                    y = jnp.where(low_half, y_diag[u, h0], y_diag[u, h0 + 1])
                    y = y + y_off[:, pair_cols(j)] * per_head_lanes(exp_da[u], h0) + skip_ref[:, pair_cols(pr)] * x2[u, pr]
                    zz = z_ref[b, rows, pair_cols(pr)]
                    ys.append(y * (zz * jax.nn.sigmoid(zz)))
                yg = jnp.concatenate(ys, axis=1)
                o_ref[b, rows, gl] = _rms(yg, nw_ref[:, gl]).astype(o_ref.dtype)
        return carry

    lax.fori_loop(0, rows_per_step // (CHUNK * SSD_CHUNKS), chunk_pair, 0)


def _ssd(xbc_raw, small, z, conv_w, conv_b, dt_bias, a_log, d_skip, norm_w, bsz, seq_len):
    par = jnp.zeros((2, LANES), F32)
    par = par.at[0, :C_HEADS].set(a_log.astype(F32))
    par = par.at[1, :C_HEADS].set(dt_bias.astype(F32))
    skip = jnp.repeat(d_skip.astype(F32), C_HEAD_DIM).reshape(1, C_W)
    width = C_W + 2 * C_BC
    blk = lambda w: pl.BlockSpec((bsz, SCAN_ROWS, w), lambda t: (0, t, 0))
    out = pl.pallas_call(
        _ssd_kernel, grid=(seq_len // SCAN_ROWS,),
        in_specs=[blk(width), blk(LANES), blk(C_W), _resident((CONV_K, width)), _resident((1, width)),
                  _resident((2, LANES)), _resident((1, C_W)), _resident((1, C_W))],
        out_specs=blk(C_W),
        out_shape=jax.ShapeDtypeStruct((bsz, seq_len, C_W), BF16),
        scratch_shapes=[pltpu.VMEM((bsz, SUBLANES + SCAN_ROWS, width), F32),
                        pltpu.VMEM((bsz, C_STATE, C_W), F32)],
        compiler_params=_cparams("arbitrary"), name="ssd",
    )(xbc_raw.reshape(bsz, seq_len, width), small.reshape(bsz, seq_len, LANES), z.reshape(bsz, seq_len, C_W),
      conv_w.astype(F32), conv_b.astype(F32).reshape(1, width), par, skip, norm_w.reshape(1, C_W).astype(F32))
    return out.reshape(bsz * seq_len, C_W)


FOX_F_PIECES = 3


def _fox_f_lane(h):
    return (h // 2) * LANES + (D_HEAD_DIM if h % 2 == 0 else 0)


def _fox_prep_kernel(sm_ref, fb_ref, place_ref, fk_ref, carry_ref):
    tm = sm_ref.shape[0]

    @pl.when(pl.program_id(1) == 0)
    def _():
        carry_ref[...] = jnp.zeros_like(carry_ref)

    causal, _ = _tril_masks(tm)
    tril = jnp.where(causal, 1.0, 0.0).astype(BF16)
    log_f = jax.nn.log_sigmoid(sm_ref[...] + fb_ref[...])
    f_cum = _cumsum_rows(tril, log_f) + carry_ref[...]
    carry_ref[...] = f_cum[tm - 1:tm, :]
    pieces = jnp.concatenate(_split3(-LOG2_E * f_cum), axis=1)
    fk_ref[...] = _dot(pieces, place_ref[...]).astype(fk_ref.dtype)


def _fox_prep(small, f_bias, bsz, seq_len):
    fb = jnp.zeros((1, LANES), F32).at[0, D_HEADS:2 * D_HEADS].set(f_bias.astype(F32))
    place = np.zeros((FOX_F_PIECES * LANES, D_W), np.float32)
    for h in range(D_HEADS):
        for piece in range(FOX_F_PIECES):
            place[piece * LANES + D_HEADS + h, _fox_f_lane(h) + piece] = 1.0
    nt = seq_len // FOX_T
    return pl.pallas_call(
        _fox_prep_kernel, grid=(bsz, nt),
        in_specs=[pl.BlockSpec((None, FOX_T, LANES), lambda b, t: (b, t, 0)), _resident((1, LANES)),
                  _resident(place.shape)],
        out_specs=pl.BlockSpec((None, FOX_T, D_W), lambda b, t: (b, t, 0)),
        out_shape=jax.ShapeDtypeStruct((bsz, seq_len, D_W), BF16),
        scratch_shapes=[pltpu.VMEM((1, LANES), F32)],
        compiler_params=_cparams("parallel", "arbitrary"), name="fox_prep",
    )(small.reshape(bsz, seq_len, LANES), fb, jnp.asarray(place, BF16))


def _fox_kernel(q_ref, k_ref, vt_ref, fk_ref, o_ref,
                kk_ref, vv_ref, sa_ref, sb_ref, pa_ref, pb_ref, aa_ref, ab_ref, m_ref, acc_ref):
    i = pl.program_id(2)
    t = FOX_T
    nt = k_ref.shape[0] // t
    n_rb = t // FOX_ROWS
    lane = lax.broadcasted_iota(jnp.int32, (t, LANES), 1)
    low_half = lane < D_HEAD_DIM
    top_half = lax.broadcasted_iota(jnp.int32, (LANES, t), 0) < D_HEAD_DIM

    @pl.when(i == 0)
    def _():
        for jt in range(nt):
            rows = slice(jt * t, (jt + 1) * t)
            kt, ft = k_ref[rows, :], fk_ref[rows, :]
            kk_ref[0, jt] = jnp.where(low_half, kt, ft)
            kk_ref[1, jt] = jnp.where(low_half, ft, kt)
            vt = vt_ref[jt]
            one = jnp.ones_like(vt)
            vv_ref[0, jt] = jnp.where(top_half, vt, one)
            vv_ref[1, jt] = jnp.where(top_half, one, vt)

    q = q_ref[...]
    ones_upto = lambda n: jnp.where(lane < n, 1.0, 0.0).astype(BF16)
    qs = (jnp.where(low_half, q, ones_upto(_fox_f_lane(0) + FOX_F_PIECES)),
          jnp.where(low_half, ones_upto(_fox_f_lane(1) + FOX_F_PIECES), q))
    key_in_blk = lax.broadcasted_iota(jnp.int32, (FOX_ROWS, t), 0)
    query = lax.broadcasted_iota(jnp.int32, (FOX_ROWS, t), 1)

    m_ref[...] = jnp.full(m_ref.shape, NEG_BIG, F32)
    acc_ref[...] = jnp.zeros_like(acc_ref)

    def scores(j, s_ref):
        jj = jnp.maximum(j, 0)
        for half in range(2):
            s_ref[half] = _dot_nt(kk_ref[half, jj], qs[half])

    def softmax_cols(s_ref, p_ref, a_ref, diagonal):
        def block(half, rb):
            s = s_ref[half, rb * FOX_ROWS:(rb + 1) * FOX_ROWS, :]
            if diagonal:
                s = jnp.where(key_in_blk + rb * FOX_ROWS <= query, s, NEG_BIG)
            return s

        tile_max = []
        for half in range(2):
            m8 = None
            for rb in range(n_rb):
                s = block(half, rb)
                b8 = jnp.maximum(jnp.maximum(s[0:8], s[8:16]), jnp.maximum(s[16:24], s[24:32]))
                m8 = b8 if m8 is None else jnp.maximum(m8, b8)
            tile_max.append(jnp.max(m8, axis=0, keepdims=True))
        for half in range(2):
            m_old = m_ref[half]
            m_new = jnp.maximum(m_old, tile_max[half])
            m_ref[half] = m_new
            a_ref[half] = jnp.exp2(m_old - m_new)
            for rb in range(n_rb):
                p_ref[half, rb * FOX_ROWS:(rb + 1) * FOX_ROWS, :] = jnp.exp2(block(half, rb) - m_new).astype(BF16)

    def accumulate(j, p_ref, a_ref):
        jj = jnp.maximum(j, 0)
        for half in range(2):
            acc_ref[half] = a_ref[half] * acc_ref[half] + _dot(vv_ref[half, jj], p_ref[half])

    scores(i, sb_ref)
    scores(i - 1, sa_ref)
    softmax_cols(sb_ref, pb_ref, ab_ref, True)

    def pair(n, carry):
        j = i - 1 - 2 * n
        scores(j - 1, sb_ref)
        softmax_cols(sa_ref, pa_ref, aa_ref, False)
        accumulate(j + 1, pb_ref, ab_ref)
        scores(j - 2, sa_ref)
        softmax_cols(sb_ref, pb_ref, ab_ref, False)
        accumulate(j, pa_ref, aa_ref)
        return carry

    n_pairs = i // 2
    lax.fori_loop(0, n_pairs, pair, 0)
    pending = i - 2 * n_pairs

    @pl.when(pending == 1)
    def _():
        softmax_cols(sa_ref, pa_ref, aa_ref, False)
        accumulate(1, pb_ref, ab_ref)
        accumulate(0, pa_ref, aa_ref)

    @pl.when(pending == 0)
    def _():
        accumulate(0, pb_ref, ab_ref)

    acc0, acc1 = acc_ref[0], acc_ref[1]
    out_t = jnp.where(top_half, acc0 / acc0[D_HEAD_DIM:D_HEAD_DIM + 1, :], acc1 / acc1[0:1, :])
    o_ref[...] = out_t.T.astype(o_ref.dtype)


def _fox_attention(q, k, v_t, fk, bsz, seq_len):
    nt = seq_len // FOX_T
    pairs = D_W // LANES
    q3, k3 = (a.reshape(bsz, seq_len, D_W) for a in (q, k))
    whole_seq = pl.BlockSpec((None, seq_len, LANES), lambda b, p, i: (b, 0, p))
    out = pl.pallas_call(
        _fox_kernel, grid=(bsz, pairs, nt),
        in_specs=[pl.BlockSpec((None, FOX_T, LANES), lambda b, p, i: (b, i, p)), whole_seq,
                  pl.BlockSpec((nt, LANES, FOX_T), lambda b, p, i: (b, p, 0)), whole_seq],
        out_specs=pl.BlockSpec((None, FOX_T, LANES), lambda b, p, i: (b, i, p)),
        out_shape=jax.ShapeDtypeStruct((bsz, seq_len, D_W), BF16),
        scratch_shapes=[pltpu.VMEM((2, nt, FOX_T, LANES), BF16), pltpu.VMEM((2, nt, LANES, FOX_T), BF16)]
        + [pltpu.VMEM((2, FOX_T, FOX_T), F32)] * 2 + [pltpu.VMEM((2, FOX_T, FOX_T), BF16)] * 2
        + [pltpu.VMEM((2, 1, FOX_T), F32)] * 3 + [pltpu.VMEM((2, LANES, FOX_T), F32)],
        compiler_params=_cparams("parallel", "parallel", "arbitrary"), name="fox_attention",
    )(q3, k3, v_t, fk)
    return out.reshape(bsz * seq_len, D_W)


def _pad_cols(w, width=LANES):
    return jnp.pad(w, ((0, 0), (0, width - w.shape[1])))


def kernel(x, norm_mix, norm_ffn, norm_final, ffn_w_gate, ffn_w_up, ffn_w_down, ab_w_in, ab_rel_bias, ab_conv_w, ab_a_log, ab_dt_bias, ab_norm_w, ab_w_out, cd_w_in, cd_conv_w, cd_conv_b, cd_dt_bias, cd_a_log, cd_d_skip, cd_norm_w, cd_f_bias, cd_w_out):
    bsz, seq_len, d = x.shape
    n = bsz * seq_len
    xf = x.reshape(n, d)
    bf = lambda w: w.astype(BF16)

    w_in = ab_w_in[0]
    o = np.cumsum([0, A_W, A_W, A_W, 3 * B_W, B_HEADS, B_HEADS, B_W])
    weights = [bf(w_in[:, o[0]:o[1]] * (LOG2_E * A_HEAD_DIM ** -0.5)), bf(w_in[:, o[1]:o[2]]), bf(w_in[:, o[2]:o[3]].T),
               bf(w_in[:, o[3]:o[4]]), bf(_pad_cols(w_in[:, o[4]:o[6]])), bf(w_in[:, o[6]:o[7]])]
    a_q, a_k, a_vt, b_qkv, b_small, b_z = _norm_proj(
        xf, norm_mix[0], weights, [BF16, BF16, BF16, F32, F32, F32], transposed=(False, False, True) + (False,) * 3)
    o_a = _band_attention(a_q, a_k, a_vt, ab_rel_bias[0], bsz, seq_len)
    o_b = _gated_delta(b_qkv, b_small, b_z, ab_conv_w[0], ab_a_log[0], ab_dt_bias[0], ab_norm_w[0], bsz, seq_len)
    ffn_w = (bf(ffn_w_gate), bf(ffn_w_up), bf(ffn_w_down))
    xf = _layer_tail(xf, o_a, o_b, bf(ab_w_out), norm_ffn[0], *ffn_w, 0, norm_final, False)

    w_in = cd_w_in[0]
    o = np.cumsum([0, C_W, C_W + 2 * C_BC, C_HEADS, D_W, D_W, D_W, D_HEADS])
    small_w = jnp.concatenate([w_in[:, o[2]:o[3]], w_in[:, o[6]:o[7]]], axis=1)
    weights = [bf(w_in[:, o[0]:o[1]]), bf(w_in[:, o[1]:o[2]]), bf(_pad_cols(small_w)),
               bf(w_in[:, o[3]:o[4]] * (LOG2_E * D_HEAD_DIM ** -0.5)), bf(w_in[:, o[4]:o[5]]), bf(w_in[:, o[5]:o[6]].T)]
    c_z, c_xbc, cd_small, d_q, d_k, d_vt = _norm_proj(
        xf, norm_mix[1], weights, [F32, F32, F32, BF16, BF16, BF16], transposed=(False,) * 5 + (True,))
    y_c = _ssd(c_xbc, cd_small, c_z, cd_conv_w[0], cd_conv_b[0], cd_dt_bias[0], cd_a_log[0], cd_d_skip[0],
               cd_norm_w[0], bsz, seq_len)
    fk = _fox_prep(cd_small, cd_f_bias[0], bsz, seq_len)
    o_d = _fox_attention(d_q, d_k, d_vt, fk, bsz, seq_len)
    xf = _layer_tail(xf, y_c, o_d, bf(cd_w_out), norm_ffn[1], *ffn_w, 1, norm_final, True)
    return xf.reshape(bsz, seq_len, d)
```

```python
import functools

import jax
import jax.numpy as jnp
import numpy as np
from jax import lax
from jax.experimental import pallas as pl
from jax.experimental.pallas import tpu as pltpu

F32 = jnp.float32
BF16 = jnp.bfloat16

D_MODEL = 1024
CHUNK = 64
EPS = 1e-6
CONV_K = 4
A_HEADS, A_HEAD_DIM, A_LEFT_CHUNKS, A_MAX_REL = 8, 64, 8, 256
B_HEADS, B_HEAD_DIM = 4, 128
C_HEADS, C_HEAD_DIM, C_GROUPS, C_STATE = 8, 64, 2, 128
D_HEADS, D_HEAD_DIM = 8, 64
A_W = A_HEADS * A_HEAD_DIM
B_W = B_HEADS * B_HEAD_DIM
C_W = C_HEADS * C_HEAD_DIM
D_W = D_HEADS * D_HEAD_DIM
C_BC = C_GROUPS * C_STATE

LANES = 128
SUBLANES = 8
VMEM_LIMIT_BYTES = 56 * 1024 * 1024
NEG_BIG = -1e30
LOG2_E = float(np.log2(np.e))

ROW_TILE = 512
TAIL_ROWS = 1024
FFN_CHUNK = 256
BAND_TQ = 256
BAND_WIN = BAND_TQ + A_LEFT_CHUNKS * CHUNK
FOX_T = 512
FOX_ROWS = 32
SCAN_ROWS = 512
SSD_CHUNKS = 4
GDN_BUILD_CHUNKS = 4


def _cparams(*sem):
    return pltpu.CompilerParams(dimension_semantics=sem, vmem_limit_bytes=VMEM_LIMIT_BYTES)


def _resident(shape):
    nd = len(shape)
    return pl.BlockSpec(shape, lambda *_: (0,) * nd, pipeline_mode=pl.Buffered(1))


def _dot(a, b):
    return jnp.dot(a, b, preferred_element_type=F32)


def _dot_nt(a, b):
    return lax.dot_general(a, b, (((1,), (1,)), ((), ())), preferred_element_type=F32)


def _dot_tn(a, b):
    return lax.dot_general(a, b, (((0,), (0,)), ((), ())), preferred_element_type=F32)


def _rms(x, w):
    return x * lax.rsqrt(jnp.mean(x * x, axis=-1, keepdims=True) + EPS) * w


def _split3(x):
    hi = x.astype(BF16)
    r1 = x - hi.astype(F32)
    mid = r1.astype(BF16)
    lo = (r1 - mid.astype(F32)).astype(BF16)
    return hi, mid, lo


def _cumsum_rows(tril, x):
    hi, mid, lo = _split3(x)
    return _dot(tril, hi) + _dot(tril, mid) + _dot(tril, lo)


def _tril_masks(n):
    r = lax.broadcasted_iota(jnp.int32, (n, n), 0)
    c = lax.broadcasted_iota(jnp.int32, (n, n), 1)
    return r >= c, r > c


def _norm_proj_kernel(x_ref, nw_ref, *refs, transposed):
    n_out = len(refs) // 2
    h = _rms(x_ref[...], nw_ref[...]).astype(BF16)
    for w_ref, o_ref, tr in zip(refs[:n_out], refs[n_out:], transposed):
        out = _dot_nt(w_ref[...], h) if tr else _dot(h, w_ref[...])
        o_ref[...] = out.astype(o_ref.dtype)


def _norm_proj(x, norm_w, weights, out_dtypes, transposed=None):
    n, d = x.shape
    transposed = transposed or (False,) * len(weights)
    in_specs = [pl.BlockSpec((ROW_TILE, d), lambda i: (i, 0)), _resident((1, d))]
    in_specs += [_resident(w.shape) for w in weights]
    out_shape, out_specs = [], []
    for w, dt, tr in zip(weights, out_dtypes, transposed):
        if tr:
            out_shape.append(jax.ShapeDtypeStruct((n // ROW_TILE, w.shape[0], ROW_TILE), dt))
            out_specs.append(pl.BlockSpec((None, w.shape[0], ROW_TILE), lambda i: (i, 0, 0)))
        else:
            out_shape.append(jax.ShapeDtypeStruct((n, w.shape[1]), dt))
            out_specs.append(pl.BlockSpec((ROW_TILE, w.shape[1]), lambda i: (i, 0)))
    return pl.pallas_call(
        functools.partial(_norm_proj_kernel, transposed=tuple(transposed)),
        grid=(n // ROW_TILE,), in_specs=in_specs, out_specs=out_specs,
        out_shape=out_shape, compiler_params=_cparams("parallel"), name="norm_proj",
    )(x, norm_w.reshape(1, d), *weights)


def _tail_kernel(x_ref, a_ref, b_ref, woa_ref, wob_ref, nw_ref, wg_ref, wu_ref, wd_ref, fn_ref,
                 o_ref, acc_ref, *, d_ff, final_norm):
    x1 = x_ref[...] + _dot(a_ref[...], woa_ref[...]) + _dot(b_ref[...], wob_ref[...])
    h = _rms(x1, nw_ref[...]).astype(BF16)
    acc_ref[...] = x1
    ffn = None
    for c in range(d_ff // FFN_CHUNK):
        cols = slice(c * FFN_CHUNK, (c + 1) * FFN_CHUNK)
        g = _dot(h, wg_ref[:, cols])
        u = _dot(h, wu_ref[:, cols])
        part = _dot((g * jax.nn.sigmoid(g) * u).astype(BF16), wd_ref[cols, :])
        ffn = part if ffn is None else ffn + part
    y = acc_ref[...] + ffn
    if final_norm:
        y = _rms(y, fn_ref[...])
    o_ref[...] = y


def _layer_tail(x, mix_a, mix_b, w_out, norm_w, wg, wu, wd, layer, final_w, final_norm):
    n, d = x.shape
    d_ff = wg.shape[2]
    wa, wb = mix_a.shape[1], mix_b.shape[1]
    assert wa == wb and w_out.shape == (1, wa + wb, d)
    row = lambda w: pl.BlockSpec((TAIL_ROWS, w), lambda i: (i, 0))
    pick = lambda shape, idx: pl.BlockSpec((None,) + shape, lambda i: idx, pipeline_mode=pl.Buffered(1))
    in_specs = [row(d), row(wa), row(wb), pick((wa, d), (0, 0, 0)), pick((wb, d), (0, 1, 0)), _resident((1, d)),
                pick((d, d_ff), (layer, 0, 0)), pick((d, d_ff), (layer, 0, 0)), pick((d_ff, d), (layer, 0, 0)),
                _resident((1, d))]
    return pl.pallas_call(
        functools.partial(_tail_kernel, d_ff=d_ff, final_norm=final_norm),
        grid=(n // TAIL_ROWS,), in_specs=in_specs, out_specs=row(d),
        out_shape=jax.ShapeDtypeStruct((n, d), F32),
        scratch_shapes=[pltpu.VMEM((TAIL_ROWS, d), F32)],
        compiler_params=_cparams("parallel"), name="layer_tail",
    )(x, mix_a, mix_b, w_out, w_out, norm_w.reshape(1, d), wg, wu, wd, final_w.reshape(1, d))


def _band_kernel(q_ref, *refs):
    n_blk = BAND_WIN // BAND_TQ
    k_refs, vt_refs = refs[:n_blk], refs[n_blk:2 * n_blk]
    by_offset_ref, o_ref, bias_ref, s_ref, p_ref = refs[2 * n_blk:]
    i = pl.program_id(1)

    @pl.when((pl.program_id(0) == 0) & (i == 0))
    def _():
        _fill_band_bias(by_offset_ref, bias_ref)

    lane = lax.broadcasted_iota(jnp.int32, (BAND_TQ, LANES), 1)
    low_half = lane < A_HEAD_DIM
    top_half = lax.broadcasted_iota(jnp.int32, (LANES, BAND_WIN), 0) < A_HEAD_DIM
    pairs = A_W // LANES
    n_rb = BAND_WIN // FOX_ROWS
    fill = [jnp.where(i - (n_blk - 1) + j >= 0, 0.0, NEG_BIG) for j in range(n_blk)]
    one_at = lambda n: jnp.where(lane == n, 1.0, 0.0).astype(BF16)

    v_aug = []
    for pr in range(pairs):
        cols = slice(pr * LANES, (pr + 1) * LANES)
        q = q_ref[:, cols]
        k_blocks = [r[:, cols] for r in k_refs]
        k_even = jnp.concatenate([jnp.where(low_half, kb, f.astype(BF16)) for kb, f in zip(k_blocks, fill)], axis=0)
        k_odd = jnp.concatenate([jnp.where(low_half, f.astype(BF16), kb) for kb, f in zip(k_blocks, fill)], axis=0)
        s_ref[2 * pr] = _dot_nt(k_even, jnp.where(low_half, q, one_at(A_HEAD_DIM))) + bias_ref[2 * pr]
        s_ref[2 * pr + 1] = _dot_nt(k_odd, jnp.where(low_half, one_at(0), q)) + bias_ref[2 * pr + 1]
        vt = jnp.concatenate([r[cols, :] for r in vt_refs], axis=1)
        one = jnp.ones_like(vt)
        v_aug += [jnp.where(top_half, vt, one), jnp.where(top_half, one, vt)]

    rows = lambda rb: slice(rb * FOX_ROWS, (rb + 1) * FOX_ROWS)
    col_max = []
    for h in range(A_HEADS):
        m8 = None
        for rb in range(n_rb):
            s = s_ref[h, rows(rb), :]
            b8 = jnp.maximum(jnp.maximum(s[0:8], s[8:16]), jnp.maximum(s[16:24], s[24:32]))
            m8 = b8 if m8 is None else jnp.maximum(m8, b8)
        col_max.append(jnp.max(m8, axis=0, keepdims=True))
    for h in range(A_HEADS):
        for rb in range(n_rb):
            p_ref[h, rows(rb), :] = jnp.exp2(s_ref[h, rows(rb), :] - col_max[h]).astype(BF16)

    out_top = lax.broadcasted_iota(jnp.int32, (LANES, BAND_TQ), 0) < A_HEAD_DIM
    for pr in range(pairs):
        o_even = _dot(v_aug[2 * pr], p_ref[2 * pr])
        o_odd = _dot(v_aug[2 * pr + 1], p_ref[2 * pr + 1])
        out_t = jnp.where(out_top, o_even / o_even[A_HEAD_DIM:A_HEAD_DIM + 1, :], o_odd / o_odd[0:1, :])
        o_ref[:, pr * LANES:(pr + 1) * LANES] = out_t.T.astype(o_ref.dtype)


BAND_BIAS_PERIOD = BAND_WIN + BAND_TQ


def _band_bias_by_offset(rel_bias):
    d = np.arange(BAND_BIAS_PERIOD) - BAND_WIN
    idx = np.clip(d + A_LEFT_CHUNKS * CHUNK, -A_MAX_REL, A_MAX_REL) + A_MAX_REL
    return rel_bias.astype(F32)[:, idx] * LOG2_E


def _fill_band_bias(by_offset_ref, bias_ref):
    kc = lax.broadcasted_iota(jnp.int32, (BAND_WIN, BAND_TQ), 0) // CHUNK
    qc = lax.broadcasted_iota(jnp.int32, (BAND_WIN, BAND_TQ), 1) // CHUNK
    for h in range(A_HEADS):
        spread = jnp.broadcast_to(by_offset_ref[h:h + 1, :], (BAND_WIN, BAND_BIAS_PERIOD))
        skewed = pltpu.roll(spread, 0, axis=1, stride=1, stride_axis=0)[:, BAND_WIN:]
        bias_ref[h] = jnp.where(kc >= qc, jnp.where(kc <= qc + A_LEFT_CHUNKS, skewed, NEG_BIG), NEG_BIG)


def _band_attention(q, k, v_t, rel_bias, bsz, seq_len):
    n_blk = BAND_WIN // BAND_TQ
    nq = seq_len // BAND_TQ
    per_tile = ROW_TILE // BAND_TQ
    tiles_per_seq = seq_len // ROW_TILE
    q3, k3 = (a.reshape(bsz, seq_len, A_W) for a in (q, k))
    blk = (None, BAND_TQ, A_W)
    key_blk = lambda i, j: jnp.maximum(i - (n_blk - 1) + j, 0)
    k_specs = [pl.BlockSpec(blk, lambda b, i, j=j: (b, key_blk(i, j), 0)) for j in range(n_blk)]
    vt_specs = [pl.BlockSpec((None, A_W, BAND_TQ),
                             lambda b, i, j=j: (b * tiles_per_seq + key_blk(i, j) // per_tile, 0, key_blk(i, j) % per_tile))
                for j in range(n_blk)]
    out = pl.pallas_call(
        _band_kernel, grid=(bsz, nq),
        in_specs=[pl.BlockSpec(blk, lambda b, i: (b, i, 0))] + k_specs + vt_specs
        + [_resident((A_HEADS, BAND_BIAS_PERIOD))],
        out_specs=pl.BlockSpec(blk, lambda b, i: (b, i, 0)),
        out_shape=jax.ShapeDtypeStruct((bsz, seq_len, A_W), BF16),
        scratch_shapes=[pltpu.VMEM((A_HEADS, BAND_WIN, BAND_TQ), F32)] * 2
        + [pltpu.VMEM((A_HEADS, BAND_WIN, BAND_TQ), BF16)],
        compiler_params=_cparams("arbitrary", "arbitrary"), name="band_attention",
    )(q3, *([k3] * n_blk), *([v_t] * n_blk), _band_bias_by_offset(rel_bias))
    return out.reshape(bsz * seq_len, A_W)


def _stage_conv_input(x_ref, pad_ref):
    rows = x_ref.shape[1]

    @pl.when(pl.program_id(0) == 0)
    def _():
        pad_ref[:, 0:SUBLANES, :] = jnp.zeros((pad_ref.shape[0], SUBLANES, pad_ref.shape[2]), F32)

    @pl.when(pl.program_id(0) > 0)
    def _():
        pad_ref[:, 0:SUBLANES, :] = pad_ref[:, rows:rows + SUBLANES, :]

    pad_ref[:, SUBLANES:, :] = x_ref[...]


def _conv_silu_rows(pad_ref, w_ref, b, row0, cols, bias_ref=None):
    window = pad_ref[b, pl.ds(row0, SUBLANES + CHUNK), cols]
    acc = None
    for back in range(CONV_K):
        term = w_ref[CONV_K - 1 - back:CONV_K - back, cols] * window[SUBLANES - back:SUBLANES - back + CHUNK]
        acc = term if acc is None else acc + term
    if bias_ref is not None:
        acc = acc + bias_ref[:, cols]
    return acc * jax.nn.sigmoid(acc)


def _gdn_kernel(qkv_ref, sm_ref, z_ref, cw_ref, par_ref, nw_ref, o_ref,
                pad_ref, state_ref, lhs_ref, add_ref, gl_ref):
    bsz, rows_per_step = qkv_ref.shape[0], qkv_ref.shape[1]
    n_chunks = rows_per_step // CHUNK
    heads = [(b, h) for b in range(bsz) for h in range(B_HEADS)]
    units = [(cc, b, h) for cc in range(GDN_BUILD_CHUNKS) for b, h in heads]

    @pl.when(pl.program_id(0) == 0)
    def _():
        state_ref[...] = jnp.zeros_like(state_ref)

    _stage_conv_input(qkv_ref, pad_ref)
    causal, strict = _tril_masks(CHUNK)
    tril = jnp.where(causal, 1.0, 0.0).astype(BF16)
    eye = jnp.where(causal & ~strict, 1.0, 0.0)
    a_row = -jnp.exp(par_ref[0:1, :])
    dt_bias_row = par_ref[1:2, :]

    def head_cols(part, h):
        return slice(part * B_W + h * B_HEAD_DIM, part * B_W + (h + 1) * B_HEAD_DIM)

    def build(cp, carry):
        row0 = [pl.multiple_of((cp * GDN_BUILD_CHUNKS + cc) * CHUNK, CHUNK) for cc in range(GDN_BUILD_CHUNKS)]
        sm = {(cc, b): sm_ref[b, pl.ds(row0[cc], CHUNK), :] for cc in range(GDN_BUILD_CHUNKS) for b in range(bsz)}
        beta_all = {key: jax.nn.sigmoid(x) for key, x in sm.items()}
        gc_all = {key: _cumsum_rows(tril, a_row * jax.nn.softplus(x + dt_bias_row)) for key, x in sm.items()}
        gc_all_t = {key: x.T for key, x in gc_all.items()}
        q, k, v = ([_conv_silu_rows(pad_ref, cw_ref, b, row0[cc], head_cols(part, h)) for cc, b, h in units]
                   for part in range(3))
        q = [x * lax.rsqrt(jnp.sum(x * x, axis=-1, keepdims=True) + EPS) * (B_HEAD_DIM ** -0.5) for x in q]
        k = [x * lax.rsqrt(jnp.sum(x * x, axis=-1, keepdims=True) + EPS) for x in k]
        beta = [beta_all[cc, b][:, h:h + 1] for cc, b, h in units]
        gc = [gc_all[cc, b][:, B_HEADS + h:B_HEADS + h + 1] for cc, b, h in units]
        gc_row = [gc_all_t[cc, b][B_HEADS + h:B_HEADS + h + 1, :] for cc, b, h in units]
        decay = [jnp.exp(jnp.where(causal, g - gr, NEG_BIG)) for g, gr in zip(gc, gc_row)]
        kb = [x.astype(BF16) for x in k]
        kk = [_dot_nt(x, x) for x in kb]
        qk = [_dot_nt(x.astype(BF16), y) for x, y in zip(q, kb)]
        a_mat = [jnp.where(strict, bt * x * d, 0.0) for bt, x, d in zip(beta, kk, decay)]
        inv = [eye - a for a in a_mat]
        power = a_mat
        for _ in range(5):
            power = [_dot(x, x) for x in power]
            inv = [i + _dot(i, x) for i, x in zip(inv, power)]
        exp_gc = [jnp.exp(g) for g in gc]
        rhs = [jnp.concatenate([ki * (bt * e), vi * bt], axis=1) for ki, vi, bt, e in zip(k, v, beta, exp_gc)]
        wu = [_dot(i, r).astype(BF16) for i, r in zip(inv, rhs)]
        attn = [(x * d).astype(BF16) for x, d in zip(qk, decay)]
        gc_last = [g[CHUNK - 1:CHUNK, :] for g in gc]
        k_st = [(ki * jnp.exp(gl - g)).astype(BF16) for ki, gl, g in zip(k, gc_last, gc)]
        top = [_dot_tn(x, y) for x, y in zip(k_st, wu)]
        bot = [_dot(x, y) for x, y in zip(attn, wu)]
        for i in range(len(units)):
            slot = cp * len(units) + i
            lhs_ref[slot, 0:B_HEAD_DIM, :] = (-top[i][:, :B_HEAD_DIM]).astype(BF16)
            lhs_ref[slot, B_HEAD_DIM:, :] = (q[i] * exp_gc[i] - bot[i][:, :B_HEAD_DIM]).astype(BF16)
            add_ref[slot, 0:B_HEAD_DIM, :] = top[i][:, B_HEAD_DIM:]
            add_ref[slot, B_HEAD_DIM:, :] = bot[i][:, B_HEAD_DIM:]
            gl_ref[slot] = jnp.broadcast_to(jnp.exp(gc_last[i]), (SUBLANES, LANES))
        return carry

    lax.fori_loop(0, n_chunks // GDN_BUILD_CHUNKS, build, 0)

    def scan(c, carry):
        rows = pl.ds(pl.multiple_of(c * CHUNK, CHUNK), CHUNK)
        states = [state_ref[i] for i in range(len(heads))]
        res = [_dot(lhs_ref[c * len(heads) + i], states[i].astype(BF16)) + add_ref[c * len(heads) + i]
               for i in range(len(heads))]
        for i, (b, h) in enumerate(heads):
            state_ref[i] = gl_ref[c * len(heads) + i][0:1, :] * states[i] + res[i][:B_HEAD_DIM]
            zz = z_ref[b, rows, head_cols(0, h)]
            y = _rms(res[i][B_HEAD_DIM:], nw_ref[...]) * (zz * jax.nn.sigmoid(zz))
            o_ref[b, rows, head_cols(0, h)] = y.astype(o_ref.dtype)
        return carry

    lax.fori_loop(0, n_chunks, scan, 0)


def _gated_delta(qkv_raw, small, z, conv_w, a_log, dt_bias, norm_w, bsz, seq_len):
    par = jnp.zeros((2, LANES), F32)
    par = par.at[0, B_HEADS:2 * B_HEADS].set(a_log.astype(F32))
    par = par.at[1, B_HEADS:2 * B_HEADS].set(dt_bias.astype(F32))
    blk = lambda w: pl.BlockSpec((bsz, SCAN_ROWS, w), lambda t: (0, t, 0))
    n_units = bsz * B_HEADS
    n_slots = n_units * (SCAN_ROWS // CHUNK)
    out = pl.pallas_call(
        _gdn_kernel, grid=(seq_len // SCAN_ROWS,),
        in_specs=[blk(3 * B_W), blk(LANES), blk(B_W), _resident((CONV_K, 3 * B_W)), _resident((2, LANES)),
                  _resident((1, B_HEAD_DIM))],
        out_specs=blk(B_W),
        out_shape=jax.ShapeDtypeStruct((bsz, seq_len, B_W), BF16),
        scratch_shapes=[pltpu.VMEM((bsz, SUBLANES + SCAN_ROWS, 3 * B_W), F32),
                        pltpu.VMEM((n_units, B_HEAD_DIM, B_HEAD_DIM), F32),
                        pltpu.VMEM((n_slots, B_HEAD_DIM + CHUNK, B_HEAD_DIM), BF16),
                        pltpu.VMEM((n_slots, B_HEAD_DIM + CHUNK, B_HEAD_DIM), F32),
                        pltpu.VMEM((n_slots, SUBLANES, LANES), F32)],
        compiler_params=_cparams("arbitrary"), name="gated_delta",
    )(qkv_raw.reshape(bsz, seq_len, 3 * B_W), small.reshape(bsz, seq_len, LANES),
      z.reshape(bsz, seq_len, B_W), conv_w.astype(F32), par, norm_w.reshape(1, B_HEAD_DIM).astype(F32))
    return out.reshape(bsz * seq_len, B_W)


def _ssd_kernel(xbc_ref, sm_ref, z_ref, cw_ref, cb_ref, par_ref, skip_ref, nw_ref, o_ref, pad_ref, state_ref):
    bsz, rows_per_step = xbc_ref.shape[0], xbc_ref.shape[1]
    group_w = C_W // C_GROUPS
    heads_per_group = C_HEADS // C_GROUPS
    pairs_per_group = heads_per_group // 2

    @pl.when(pl.program_id(0) == 0)
    def _():
        state_ref[...] = jnp.zeros_like(state_ref)

    _stage_conv_input(xbc_ref, pad_ref)
    causal, _ = _tril_masks(CHUNK)
    tril = jnp.where(causal, 1.0, 0.0).astype(BF16)
    a_row = -jnp.exp(par_ref[0:1, :])
    dt_bias_row = par_ref[1:2, :]
    lane = lax.broadcasted_iota(jnp.int32, (CHUNK, LANES), 1)
    low_half = lane < C_HEAD_DIM

    def per_head_lanes(cols, h0):
        return jnp.where(low_half, cols[:, h0:h0 + 1], cols[:, h0 + 1:h0 + 2])

    def pair_cols(pr):
        return slice(pr * LANES, (pr + 1) * LANES)

    def chunk_pair(cp, carry):
        units = [(cc, b) for cc in range(SSD_CHUNKS) for b in range(bsz)]
        groups = [(u, g) for u in units for g in range(C_GROUPS)]
        row0 = [pl.multiple_of((cp * SSD_CHUNKS + cc) * CHUNK, CHUNK) for cc in range(SSD_CHUNKS)]
        conv = lambda u, cols: _conv_silu_rows(pad_ref, cw_ref, u[1], row0[u[0]], cols, cb_ref)
        sm = {u: sm_ref[u[1], pl.ds(row0[u[0]], CHUNK), :] for u in units}
        dt_all = {u: jax.nn.softplus(sm[u] + dt_bias_row) for u in units}
        da_cs = {u: _cumsum_rows(tril, dt_all[u] * a_row) for u in units}
        da_cs_t = {u: da_cs[u].T for u in units}
        dt_t = {u: dt_all[u].T for u in units}
        da_last = {u: da_cs[u][CHUNK - 1:CHUNK, :] for u in units}
        exp_da = {u: jnp.exp(da_cs[u]) for u in units}
        to_end = {u: jnp.exp(da_last[u] - da_cs[u]) * dt_all[u] for u in units}
        chunk_decay = {u: jnp.exp(da_last[u]) for u in units}
        x2 = {(u, pr): conv(u, pair_cols(pr)) for u in units for pr in range(C_HEADS // 2)}
        bm = {(u, g): conv(u, slice(C_W + g * C_STATE, C_W + (g + 1) * C_STATE)).astype(BF16) for u, g in groups}
        cm = {(u, g): conv(u, slice(C_W + C_BC + g * C_STATE, C_W + C_BC + (g + 1) * C_STATE)).astype(BF16)
              for u, g in groups}
        cb = {key: _dot_nt(cm[key], bm[key]) for key in groups}
        xw = {(u, g): jnp.concatenate(
            [(x2[u, g * pairs_per_group + j] * per_head_lanes(to_end[u], g * heads_per_group + 2 * j)).astype(BF16)
             for j in range(pairs_per_group)], axis=1) for u, g in groups}
        inflow = {key: _dot_tn(bm[key], xw[key]) for key in groups}
        y_diag = {}
        for u in units:
            for h in range(C_HEADS):
                seg = da_cs[u][:, h:h + 1] - da_cs_t[u][h:h + 1, :]
                wts = cb[u, h // heads_per_group] * jnp.exp(jnp.where(causal, seg, NEG_BIG)) * dt_t[u][h:h + 1, :]
                y_diag[u, h] = _dot(wts.astype(BF16), x2[u, h // 2].astype(BF16))
        for u in units:
            cc, b = u
            rows = pl.ds(row0[cc], CHUNK)
            for g in range(C_GROUPS):
                gl = slice(g * group_w, (g + 1) * group_w)
                prev = state_ref[b, :, gl]
                y_off = _dot(cm[u, g], prev.astype(BF16))
                dec = jnp.concatenate(
                    [jnp.where(low_half[0:1], chunk_decay[u][:, h0:h0 + 1], chunk_decay[u][:, h0 + 1:h0 + 2])
                     for h0 in range(g * heads_per_group, (g + 1) * heads_per_group, 2)], axis=1)
                state_ref[b, :, gl] = prev * dec + inflow[u, g]
                ys = []
                for j in range(pairs_per_group):
                    pr = g * pairs_per_group + j
                    h0 = 2 * pr
                    y = jnp.where(low_half, y_diag[u, h0], y_diag[u, h0 + 1])
                    y = y + y_off[:, pair_cols(j)] * per_head_lanes(exp_da[u], h0) + skip_ref[:, pair_cols(pr)] * x2[u, pr]
                    zz = z_ref[b, rows, pair_cols(pr)]
                    ys.append(y * (zz * jax.nn.sigmoid(zz)))
                yg = jnp.concatenate(ys, axis=1)
                o_ref[b, rows, gl] = _rms(yg, nw_ref[:, gl]).astype(o_ref.dtype)
        return carry

    lax.fori_loop(0, rows_per_step // (CHUNK * SSD_CHUNKS), chunk_pair, 0)


def _ssd(xbc_raw, small, z, conv_w, conv_b, dt_bias, a_log, d_skip, norm_w, bsz, seq_len):
    par = jnp.zeros((2, LANES), F32)
    par = par.at[0, :C_HEADS].set(a_log.astype(F32))
    par = par.at[1, :C_HEADS].set(dt_bias.astype(F32))
    skip = jnp.repeat(d_skip.astype(F32), C_HEAD_DIM).reshape(1, C_W)
    width = C_W + 2 * C_BC
    blk = lambda w: pl.BlockSpec((bsz, SCAN_ROWS, w), lambda t: (0, t, 0))
    out = pl.pallas_call(
        _ssd_kernel, grid=(seq_len // SCAN_ROWS,),
        in_specs=[blk(width), blk(LANES), blk(C_W), _resident((CONV_K, width)), _resident((1, width)),
                  _resident((2, LANES)), _resident((1, C_W)), _resident((1, C_W))],
        out_specs=blk(C_W),
        out_shape=jax.ShapeDtypeStruct((bsz, seq_len, C_W), BF16),
        scratch_shapes=[pltpu.VMEM((bsz, SUBLANES + SCAN_ROWS, width), F32),
                        pltpu.VMEM((bsz, C_STATE, C_W), F32)],
        compiler_params=_cparams("arbitrary"), name="ssd",
    )(xbc_raw.reshape(bsz, seq_len, width), small.reshape(bsz, seq_len, LANES), z.reshape(bsz, seq_len, C_W),
      conv_w.astype(F32), conv_b.astype(F32).reshape(1, width), par, skip, norm_w.reshape(1, C_W).astype(F32))
    return out.reshape(bsz * seq_len, C_W)


FOX_F_PIECES = 3


def _fox_f_lane(h):
    return (h // 2) * LANES + (D_HEAD_DIM if h % 2 == 0 else 0)


def _fox_prep_kernel(sm_ref, fb_ref, place_ref, fk_ref, carry_ref):
    tm = sm_ref.shape[0]

    @pl.when(pl.program_id(1) == 0)
    def _():
        carry_ref[...] = jnp.zeros_like(carry_ref)

    causal, _ = _tril_masks(tm)
    tril = jnp.where(causal, 1.0, 0.0).astype(BF16)
    log_f = jax.nn.log_sigmoid(sm_ref[...] + fb_ref[...])
    f_cum = _cumsum_rows(tril, log_f) + carry_ref[...]
    carry_ref[...] = f_cum[tm - 1:tm, :]
    pieces = jnp.concatenate(_split3(-LOG2_E * f_cum), axis=1)
    fk_ref[...] = _dot(pieces, place_ref[...]).astype(fk_ref.dtype)


def _fox_prep(small, f_bias, bsz, seq_len):
    fb = jnp.zeros((1, LANES), F32).at[0, D_HEADS:2 * D_HEADS].set(f_bias.astype(F32))
    place = np.zeros((FOX_F_PIECES * LANES, D_W), np.float32)
    for h in range(D_HEADS):
        for piece in range(FOX_F_PIECES):
            place[piece * LANES + D_HEADS + h, _fox_f_lane(h) + piece] = 1.0
    nt = seq_len // FOX_T
    return pl.pallas_call(
        _fox_prep_kernel, grid=(bsz, nt),
        in_specs=[pl.BlockSpec((None, FOX_T, LANES), lambda b, t: (b, t, 0)), _resident((1, LANES)),
                  _resident(place.shape)],
        out_specs=pl.BlockSpec((None, FOX_T, D_W), lambda b, t: (b, t, 0)),
        out_shape=jax.ShapeDtypeStruct((bsz, seq_len, D_W), BF16),
        scratch_shapes=[pltpu.VMEM((1, LANES), F32)],
        compiler_params=_cparams("parallel", "arbitrary"), name="fox_prep",
    )(small.reshape(bsz, seq_len, LANES), fb, jnp.asarray(place, BF16))


def _fox_kernel(q_ref, k_ref, vt_ref, fk_ref, o_ref,
                kk_ref, vv_ref, sa_ref, sb_ref, pa_ref, pb_ref, aa_ref, ab_ref, m_ref, acc_ref):
    i = pl.program_id(2)
    t = FOX_T
    nt = k_ref.shape[0] // t
    n_rb = t // FOX_ROWS
    lane = lax.broadcasted_iota(jnp.int32, (t, LANES), 1)
    low_half = lane < D_HEAD_DIM
    top_half = lax.broadcasted_iota(jnp.int32, (LANES, t), 0) < D_HEAD_DIM

    @pl.when(i == 0)
    def _():
        for jt in range(nt):
            rows = slice(jt * t, (jt + 1) * t)
            kt, ft = k_ref[rows, :], fk_ref[rows, :]
            kk_ref[0, jt] = jnp.where(low_half, kt, ft)
            kk_ref[1, jt] = jnp.where(low_half, ft, kt)
            vt = vt_ref[jt]
            one = jnp.ones_like(vt)
            vv_ref[0, jt] = jnp.where(top_half, vt, one)
            vv_ref[1, jt] = jnp.where(top_half, one, vt)

    q = q_ref[...]
    ones_upto = lambda n: jnp.where(lane < n, 1.0, 0.0).astype(BF16)
    qs = (jnp.where(low_half, q, ones_upto(_fox_f_lane(0) + FOX_F_PIECES)),
          jnp.where(low_half, ones_upto(_fox_f_lane(1) + FOX_F_PIECES), q))

    m_ref[...] = jnp.full(m_ref.shape, NEG_BIG, F32)
    acc_ref[...] = jnp.zeros_like(acc_ref)

    def scores(j, s_ref):
        jj = jnp.maximum(j, 0)
        for half in range(2):
            s_ref[half] = _dot_nt(kk_ref[half, jj], qs[half])

    def softmax_cols(s_ref, p_ref, a_ref):
        def block(half, rb):
            return s_ref[half, rb * FOX_ROWS:(rb + 1) * FOX_ROWS, :]

        tile_max = []
        for half in range(2):
            m8 = None
            for rb in range(n_rb):
                s = block(half, rb)
                b8 = jnp.maximum(jnp.maximum(s[0:8], s[8:16]), jnp.maximum(s[16:24], s[24:32]))
                m8 = b8 if m8 is None else jnp.maximum(m8, b8)
            tile_max.append(jnp.max(m8, axis=0, keepdims=True))
        for half in range(2):
            m_old = m_ref[half]
            m_new = jnp.maximum(m_old, tile_max[half])
            m_ref[half] = m_new
            a_ref[half] = jnp.exp2(m_old - m_new)
            for rb in range(n_rb):
                p_ref[half, rb * FOX_ROWS:(rb + 1) * FOX_ROWS, :] = jnp.exp2(block(half, rb) - m_new).astype(BF16)

    def softmax_diag(s_ref, p_ref, a_ref):
        n_lt = t // LANES
        rb_per_lt = LANES // FOX_ROWS
        tri_key = lax.broadcasted_iota(jnp.int32, (FOX_ROWS, LANES), 0)
        tri_query = lax.broadcasted_iota(jnp.int32, (FOX_ROWS, LANES), 1)

        def piece(half, rb, lt):
            s = s_ref[half, rb * FOX_ROWS:(rb + 1) * FOX_ROWS, lt * LANES:(lt + 1) * LANES]
            if lt == rb // rb_per_lt:
                s = jnp.where(tri_key + (rb % rb_per_lt) * FOX_ROWS <= tri_query, s, NEG_BIG)
            return s

        tile_max = []
        for half in range(2):
            per_lt = []
            for lt in range(n_lt):
                m8 = None
                for rb in range((lt + 1) * rb_per_lt):
                    s = piece(half, rb, lt)
                    b8 = jnp.maximum(jnp.maximum(s[0:8], s[8:16]), jnp.maximum(s[16:24], s[24:32]))
                    m8 = b8 if m8 is None else jnp.maximum(m8, b8)
                per_lt.append(jnp.max(m8, axis=0, keepdims=True))
            tile_max.append(jnp.concatenate(per_lt, axis=1))
        for half in range(2):
            m_old = m_ref[half]
            m_new = jnp.maximum(m_old, tile_max[half])
            m_ref[half] = m_new
            a_ref[half] = jnp.exp2(m_old - m_new)
            for rb in range(n_rb):
                for lt in range(n_lt):
                    lanes = slice(lt * LANES, (lt + 1) * LANES)
                    if lt < rb // rb_per_lt:
                        p = jnp.zeros((FOX_ROWS, LANES), BF16)
                    else:
                        p = jnp.exp2(piece(half, rb, lt) - m_new[:, lanes]).astype(BF16)
                    p_ref[half, rb * FOX_ROWS:(rb + 1) * FOX_ROWS, lanes] = p

    def accumulate(j, p_ref, a_ref):
        jj = jnp.maximum(j, 0)
        for half in range(2):
            acc_ref[half] = a_ref[half] * acc_ref[half] + _dot(vv_ref[half, jj], p_ref[half])

    scores(i, sb_ref)
    scores(i - 1, sa_ref)
    softmax_diag(sb_ref, pb_ref, ab_ref)

    def pair(n, carry):
        j = i - 1 - 2 * n
        scores(j - 1, sb_ref)
        softmax_cols(sa_ref, pa_ref, aa_ref)
        accumulate(j + 1, pb_ref, ab_ref)
        scores(j - 2, sa_ref)
        softmax_cols(sb_ref, pb_ref, ab_ref)
        accumulate(j, pa_ref, aa_ref)
        return carry

    n_pairs = i // 2
    lax.fori_loop(0, n_pairs, pair, 0)
    pending = i - 2 * n_pairs

    @pl.when(pending == 1)
    def _():
        softmax_cols(sa_ref, pa_ref, aa_ref)
        accumulate(1, pb_ref, ab_ref)
        accumulate(0, pa_ref, aa_ref)

    @pl.when(pending == 0)
    def _():
        accumulate(0, pb_ref, ab_ref)

    acc0, acc1 = acc_ref[0], acc_ref[1]
    out_t = jnp.where(top_half, acc0 / acc0[D_HEAD_DIM:D_HEAD_DIM + 1, :], acc1 / acc1[0:1, :])
    o_ref[...] = out_t.T.astype(o_ref.dtype)


def _fox_attention(q, k, v_t, fk, bsz, seq_len):
    nt = seq_len // FOX_T
    pairs = D_W // LANES
    q3, k3 = (a.reshape(bsz, seq_len, D_W) for a in (q, k))
    whole_seq = pl.BlockSpec((None, seq_len, LANES), lambda b, p, i: (b, 0, p))
    out = pl.pallas_call(
        _fox_kernel, grid=(bsz, pairs, nt),
        in_specs=[pl.BlockSpec((None, FOX_T, LANES), lambda b, p, i: (b, i, p)), whole_seq,
                  pl.BlockSpec((nt, LANES, FOX_T), lambda b, p, i: (b, p, 0)), whole_seq],
        out_specs=pl.BlockSpec((None, FOX_T, LANES), lambda b, p, i: (b, i, p)),
        out_shape=jax.ShapeDtypeStruct((bsz, seq_len, D_W), BF16),
        scratch_shapes=[pltpu.VMEM((2, nt, FOX_T, LANES), BF16), pltpu.VMEM((2, nt, LANES, FOX_T), BF16)]
        + [pltpu.VMEM((2, FOX_T, FOX_T), F32)] * 2 + [pltpu.VMEM((2, FOX_T, FOX_T), BF16)] * 2
        + [pltpu.VMEM((2, 1, FOX_T), F32)] * 3 + [pltpu.VMEM((2, LANES, FOX_T), F32)],
        compiler_params=_cparams("parallel", "parallel", "arbitrary"), name="fox_attention",
    )(q3, k3, v_t, fk)
    return out.reshape(bsz * seq_len, D_W)


def _pad_cols(w, width=LANES):
    return jnp.pad(w, ((0, 0), (0, width - w.shape[1])))


def kernel(x, norm_mix, norm_ffn, norm_final, ffn_w_gate, ffn_w_up, ffn_w_down, ab_w_in, ab_rel_bias, ab_conv_w, ab_a_log, ab_dt_bias, ab_norm_w, ab_w_out, cd_w_in, cd_conv_w, cd_conv_b, cd_dt_bias, cd_a_log, cd_d_skip, cd_norm_w, cd_f_bias, cd_w_out):
    bsz, seq_len, d = x.shape
    n = bsz * seq_len
    xf = x.reshape(n, d)
    bf = lambda w: w.astype(BF16)

    w_in = ab_w_in[0]
    o = np.cumsum([0, A_W, A_W, A_W, 3 * B_W, B_HEADS, B_HEADS, B_W])
    weights = [bf(w_in[:, o[0]:o[1]] * (LOG2_E * A_HEAD_DIM ** -0.5)), bf(w_in[:, o[1]:o[2]]), bf(w_in[:, o[2]:o[3]].T),
               bf(w_in[:, o[3]:o[4]]), bf(_pad_cols(w_in[:, o[4]:o[6]])), bf(w_in[:, o[6]:o[7]])]
    a_q, a_k, a_vt, b_qkv, b_small, b_z = _norm_proj(
        xf, norm_mix[0], weights, [BF16, BF16, BF16, F32, F32, F32], transposed=(False, False, True) + (False,) * 3)
    o_a = _band_attention(a_q, a_k, a_vt, ab_rel_bias[0], bsz, seq_len)
    o_b = _gated_delta(b_qkv, b_small, b_z, ab_conv_w[0], ab_a_log[0], ab_dt_bias[0], ab_norm_w[0], bsz, seq_len)
    ffn_w = (bf(ffn_w_gate), bf(ffn_w_up), bf(ffn_w_down))
    xf = _layer_tail(xf, o_a, o_b, bf(ab_w_out), norm_ffn[0], *ffn_w, 0, norm_final, False)

    w_in = cd_w_in[0]
    o = np.cumsum([0, C_W, C_W + 2 * C_BC, C_HEADS, D_W, D_W, D_W, D_HEADS])
    small_w = jnp.concatenate([w_in[:, o[2]:o[3]], w_in[:, o[6]:o[7]]], axis=1)
    weights = [bf(w_in[:, o[0]:o[1]]), bf(w_in[:, o[1]:o[2]]), bf(_pad_cols(small_w)),
               bf(w_in[:, o[3]:o[4]] * (LOG2_E * D_HEAD_DIM ** -0.5)), bf(w_in[:, o[4]:o[5]]), bf(w_in[:, o[5]:o[6]].T)]
    c_z, c_xbc, cd_small, d_q, d_k, d_vt = _norm_proj(
        xf, norm_mix[1], weights, [F32, F32, F32, BF16, BF16, BF16], transposed=(False,) * 5 + (True,))
    y_c = _ssd(c_xbc, cd_small, c_z, cd_conv_w[0], cd_conv_b[0], cd_dt_bias[0], cd_a_log[0], cd_d_skip[0],
               cd_norm_w[0], bsz, seq_len)
    fk = _fox_prep(cd_small, cd_f_bias[0], bsz, seq_len)
    o_d = _fox_attention(d_q, d_k, d_vt, fk, bsz, seq_len)
    xf = _layer_tail(xf, y_c, o_d, bf(cd_w_out), norm_ffn[1], *ffn_w, 1, norm_final, True)
    return xf.reshape(bsz, seq_len, d)
```

```python
import functools

import jax
import jax.numpy as jnp
import numpy as np
from jax import lax
from jax.experimental import pallas as pl
from jax.experimental.pallas import tpu as pltpu

F32 = jnp.float32
BF16 = jnp.bfloat16

D_MODEL = 1024
CHUNK = 64
EPS = 1e-6
CONV_K = 4
A_HEADS, A_HEAD_DIM, A_LEFT_CHUNKS, A_MAX_REL = 8, 64, 8, 256
B_HEADS, B_HEAD_DIM = 4, 128
C_HEADS, C_HEAD_DIM, C_GROUPS, C_STATE = 8, 64, 2, 128
D_HEADS, D_HEAD_DIM = 8, 64
A_W = A_HEADS * A_HEAD_DIM
B_W = B_HEADS * B_HEAD_DIM
C_W = C_HEADS * C_HEAD_DIM
D_W = D_HEADS * D_HEAD_DIM
C_BC = C_GROUPS * C_STATE

LANES = 128
SUBLANES = 8
VMEM_LIMIT_BYTES = 56 * 1024 * 1024
NEG_BIG = -1e30
LOG2_E = float(np.log2(np.e))

ROW_TILE = 512
TAIL_ROWS = 1024
FFN_CHUNK = 256
BAND_TQ = 256
BAND_WIN = BAND_TQ + A_LEFT_CHUNKS * CHUNK
FOX_T = 512
FOX_ROWS = 32
FOX_PAIRS = 2
SCAN_ROWS = 512
SSD_CHUNKS = 4
GDN_BUILD_CHUNKS = 4


def _cparams(*sem):
    return pltpu.CompilerParams(dimension_semantics=sem, vmem_limit_bytes=VMEM_LIMIT_BYTES)


def _resident(shape):
    nd = len(shape)
    return pl.BlockSpec(shape, lambda *_: (0,) * nd, pipeline_mode=pl.Buffered(1))


def _dot(a, b):
    return jnp.dot(a, b, preferred_element_type=F32)


def _dot_nt(a, b):
    return lax.dot_general(a, b, (((1,), (1,)), ((), ())), preferred_element_type=F32)


def _dot_tn(a, b):
    return lax.dot_general(a, b, (((0,), (0,)), ((), ())), preferred_element_type=F32)


def _rms(x, w):
    return x * lax.rsqrt(jnp.mean(x * x, axis=-1, keepdims=True) + EPS) * w


def _split3(x):
    hi = x.astype(BF16)
    r1 = x - hi.astype(F32)
    mid = r1.astype(BF16)
    lo = (r1 - mid.astype(F32)).astype(BF16)
    return hi, mid, lo


def _cumsum_rows(tril, x):
    hi, mid, lo = _split3(x)
    return _dot(tril, hi) + _dot(tril, mid) + _dot(tril, lo)


def _tril_masks(n):
    r = lax.broadcasted_iota(jnp.int32, (n, n), 0)
    c = lax.broadcasted_iota(jnp.int32, (n, n), 1)
    return r >= c, r > c


def _norm_proj_kernel(x_ref, nw_ref, *refs, transposed):
    n_out = len(refs) // 2
    h = _rms(x_ref[...], nw_ref[...]).astype(BF16)
    for w_ref, o_ref, tr in zip(refs[:n_out], refs[n_out:], transposed):
        out = _dot_nt(w_ref[...], h) if tr else _dot(h, w_ref[...])
        o_ref[...] = out.astype(o_ref.dtype)


def _norm_proj(x, norm_w, weights, out_dtypes, transposed=None):
    n, d = x.shape
    transposed = transposed or (False,) * len(weights)
    in_specs = [pl.BlockSpec((ROW_TILE, d), lambda i: (i, 0)), _resident((1, d))]
    in_specs += [_resident(w.shape) for w in weights]
    out_shape, out_specs = [], []
    for w, dt, tr in zip(weights, out_dtypes, transposed):
        if tr:
            out_shape.append(jax.ShapeDtypeStruct((n // ROW_TILE, w.shape[0], ROW_TILE), dt))
            out_specs.append(pl.BlockSpec((None, w.shape[0], ROW_TILE), lambda i: (i, 0, 0)))
        else:
            out_shape.append(jax.ShapeDtypeStruct((n, w.shape[1]), dt))
            out_specs.append(pl.BlockSpec((ROW_TILE, w.shape[1]), lambda i: (i, 0)))
    return pl.pallas_call(
        functools.partial(_norm_proj_kernel, transposed=tuple(transposed)),
        grid=(n // ROW_TILE,), in_specs=in_specs, out_specs=out_specs,
        out_shape=out_shape, compiler_params=_cparams("parallel"), name="norm_proj",
    )(x, norm_w.reshape(1, d), *weights)


def _tail_kernel(x_ref, a_ref, b_ref, woa_ref, wob_ref, nw_ref, wg_ref, wu_ref, wd_ref, fn_ref,
                 o_ref, acc_ref, *, d_ff, final_norm):
    x1 = x_ref[...] + _dot(a_ref[...], woa_ref[...]) + _dot(b_ref[...], wob_ref[...])
    h = _rms(x1, nw_ref[...]).astype(BF16)
    acc_ref[...] = x1
    ffn = None
    for c in range(d_ff // FFN_CHUNK):
        cols = slice(c * FFN_CHUNK, (c + 1) * FFN_CHUNK)
        g = _dot(h, wg_ref[:, cols])
        u = _dot(h, wu_ref[:, cols])
        part = _dot((g * jax.nn.sigmoid(g) * u).astype(BF16), wd_ref[cols, :])
        ffn = part if ffn is None else ffn + part
    y = acc_ref[...] + ffn
    if final_norm:
        y = _rms(y, fn_ref[...])
    o_ref[...] = y


def _layer_tail(x, mix_a, mix_b, w_out, norm_w, wg, wu, wd, layer, final_w, final_norm):
    n, d = x.shape
    d_ff = wg.shape[2]
    wa, wb = mix_a.shape[1], mix_b.shape[1]
    assert wa == wb and w_out.shape == (1, wa + wb, d)
    row = lambda w: pl.BlockSpec((TAIL_ROWS, w), lambda i: (i, 0))
    pick = lambda shape, idx: pl.BlockSpec((None,) + shape, lambda i: idx, pipeline_mode=pl.Buffered(1))
    in_specs = [row(d), row(wa), row(wb), pick((wa, d), (0, 0, 0)), pick((wb, d), (0, 1, 0)), _resident((1, d)),
                pick((d, d_ff), (layer, 0, 0)), pick((d, d_ff), (layer, 0, 0)), pick((d_ff, d), (layer, 0, 0)),
                _resident((1, d))]
    return pl.pallas_call(
        functools.partial(_tail_kernel, d_ff=d_ff, final_norm=final_norm),
        grid=(n // TAIL_ROWS,), in_specs=in_specs, out_specs=row(d),
        out_shape=jax.ShapeDtypeStruct((n, d), F32),
        scratch_shapes=[pltpu.VMEM((TAIL_ROWS, d), F32)],
        compiler_params=_cparams("parallel"), name="layer_tail",
    )(x, mix_a, mix_b, w_out, w_out, norm_w.reshape(1, d), wg, wu, wd, final_w.reshape(1, d))


def _band_kernel(q_ref, *refs):
    n_blk = BAND_WIN // BAND_TQ
    k_refs, vt_refs = refs[:n_blk], refs[n_blk:2 * n_blk]
    by_offset_ref, o_ref, bias_ref, s_ref, p_ref = refs[2 * n_blk:]
    i = pl.program_id(1)

    @pl.when((pl.program_id(0) == 0) & (i == 0))
    def _():
        _fill_band_bias(by_offset_ref, bias_ref)

    lane = lax.broadcasted_iota(jnp.int32, (BAND_TQ, LANES), 1)
    low_half = lane < A_HEAD_DIM
    top_half = lax.broadcasted_iota(jnp.int32, (LANES, BAND_WIN), 0) < A_HEAD_DIM
    pairs = A_W // LANES
    n_rb = BAND_WIN // FOX_ROWS
    fill = [jnp.where(i - (n_blk - 1) + j >= 0, 0.0, NEG_BIG) for j in range(n_blk)]
    one_at = lambda n: jnp.where(lane == n, 1.0, 0.0).astype(BF16)

    v_aug = []
    for pr in range(pairs):
        cols = slice(pr * LANES, (pr + 1) * LANES)
        q = q_ref[:, cols]
        k_blocks = [r[:, cols] for r in k_refs]
        k_even = jnp.concatenate([jnp.where(low_half, kb, f.astype(BF16)) for kb, f in zip(k_blocks, fill)], axis=0)
        k_odd = jnp.concatenate([jnp.where(low_half, f.astype(BF16), kb) for kb, f in zip(k_blocks, fill)], axis=0)
        s_ref[2 * pr] = _dot_nt(k_even, jnp.where(low_half, q, one_at(A_HEAD_DIM))) + bias_ref[2 * pr]
        s_ref[2 * pr + 1] = _dot_nt(k_odd, jnp.where(low_half, one_at(0), q)) + bias_ref[2 * pr + 1]
        vt = jnp.concatenate([r[cols, :] for r in vt_refs], axis=1)
        one = jnp.ones_like(vt)
        v_aug += [jnp.where(top_half, vt, one), jnp.where(top_half, one, vt)]

    rows = lambda rb: slice(rb * FOX_ROWS, (rb + 1) * FOX_ROWS)
    col_max = []
    for h in range(A_HEADS):
        m8 = None
        for rb in range(n_rb):
            s = s_ref[h, rows(rb), :]
            b8 = jnp.maximum(jnp.maximum(s[0:8], s[8:16]), jnp.maximum(s[16:24], s[24:32]))
            m8 = b8 if m8 is None else jnp.maximum(m8, b8)
        col_max.append(jnp.max(m8, axis=0, keepdims=True))
    for h in range(A_HEADS):
        for rb in range(n_rb):
            p_ref[h, rows(rb), :] = jnp.exp2(s_ref[h, rows(rb), :] - col_max[h]).astype(BF16)

    out_top = lax.broadcasted_iota(jnp.int32, (LANES, BAND_TQ), 0) < A_HEAD_DIM
    for pr in range(pairs):
        o_even = _dot(v_aug[2 * pr], p_ref[2 * pr])
        o_odd = _dot(v_aug[2 * pr + 1], p_ref[2 * pr + 1])
        out_t = jnp.where(out_top, o_even / o_even[A_HEAD_DIM:A_HEAD_DIM + 1, :], o_odd / o_odd[0:1, :])
        o_ref[:, pr * LANES:(pr + 1) * LANES] = out_t.T.astype(o_ref.dtype)


BAND_BIAS_PERIOD = BAND_WIN + BAND_TQ


def _band_bias_by_offset(rel_bias):
    d = np.arange(BAND_BIAS_PERIOD) - BAND_WIN
    idx = np.clip(d + A_LEFT_CHUNKS * CHUNK, -A_MAX_REL, A_MAX_REL) + A_MAX_REL
    return rel_bias.astype(F32)[:, idx] * LOG2_E


def _fill_band_bias(by_offset_ref, bias_ref):
    kc = lax.broadcasted_iota(jnp.int32, (BAND_WIN, BAND_TQ), 0) // CHUNK
    qc = lax.broadcasted_iota(jnp.int32, (BAND_WIN, BAND_TQ), 1) // CHUNK
    for h in range(A_HEADS):
        spread = jnp.broadcast_to(by_offset_ref[h:h + 1, :], (BAND_WIN, BAND_BIAS_PERIOD))
        skewed = pltpu.roll(spread, 0, axis=1, stride=1, stride_axis=0)[:, BAND_WIN:]
        bias_ref[h] = jnp.where(kc >= qc, jnp.where(kc <= qc + A_LEFT_CHUNKS, skewed, NEG_BIG), NEG_BIG)


def _band_attention(q, k, v_t, rel_bias, bsz, seq_len):
    n_blk = BAND_WIN // BAND_TQ
    nq = seq_len // BAND_TQ
    per_tile = ROW_TILE // BAND_TQ
    tiles_per_seq = seq_len // ROW_TILE
    q3, k3 = (a.reshape(bsz, seq_len, A_W) for a in (q, k))
    blk = (None, BAND_TQ, A_W)
    key_blk = lambda i, j: jnp.maximum(i - (n_blk - 1) + j, 0)
    k_specs = [pl.BlockSpec(blk, lambda b, i, j=j: (b, key_blk(i, j), 0)) for j in range(n_blk)]
    vt_specs = [pl.BlockSpec((None, A_W, BAND_TQ),
                             lambda b, i, j=j: (b * tiles_per_seq + key_blk(i, j) // per_tile, 0, key_blk(i, j) % per_tile))
                for j in range(n_blk)]
    out = pl.pallas_call(
        _band_kernel, grid=(bsz, nq),
        in_specs=[pl.BlockSpec(blk, lambda b, i: (b, i, 0))] + k_specs + vt_specs
        + [_resident((A_HEADS, BAND_BIAS_PERIOD))],
        out_specs=pl.BlockSpec(blk, lambda b, i: (b, i, 0)),
        out_shape=jax.ShapeDtypeStruct((bsz, seq_len, A_W), BF16),
        scratch_shapes=[pltpu.VMEM((A_HEADS, BAND_WIN, BAND_TQ), F32)] * 2
        + [pltpu.VMEM((A_HEADS, BAND_WIN, BAND_TQ), BF16)],
        compiler_params=_cparams("arbitrary", "arbitrary"), name="band_attention",
    )(q3, *([k3] * n_blk), *([v_t] * n_blk), _band_bias_by_offset(rel_bias))
    return out.reshape(bsz * seq_len, A_W)


def _stage_conv_input(x_ref, pad_ref):
    rows = x_ref.shape[1]

    @pl.when(pl.program_id(0) == 0)
    def _():
        pad_ref[:, 0:SUBLANES, :] = jnp.zeros((pad_ref.shape[0], SUBLANES, pad_ref.shape[2]), F32)

    @pl.when(pl.program_id(0) > 0)
    def _():
        pad_ref[:, 0:SUBLANES, :] = pad_ref[:, rows:rows + SUBLANES, :]

    pad_ref[:, SUBLANES:, :] = x_ref[...]


def _conv_silu_rows(pad_ref, w_ref, b, row0, cols, bias_ref=None):
    window = pad_ref[b, pl.ds(row0, SUBLANES + CHUNK), cols]
    acc = None
    for back in range(CONV_K):
        term = w_ref[CONV_K - 1 - back:CONV_K - back, cols] * window[SUBLANES - back:SUBLANES - back + CHUNK]
        acc = term if acc is None else acc + term
    if bias_ref is not None:
        acc = acc + bias_ref[:, cols]
    return acc * jax.nn.sigmoid(acc)


def _gdn_kernel(qkv_ref, sm_ref, z_ref, cw_ref, par_ref, nw_ref, o_ref,
                pad_ref, state_ref, lhs_ref, add_ref, gl_ref):
    bsz, rows_per_step = qkv_ref.shape[0], qkv_ref.shape[1]
    n_chunks = rows_per_step // CHUNK
    heads = [(b, h) for b in range(bsz) for h in range(B_HEADS)]
    units = [(cc, b, h) for cc in range(GDN_BUILD_CHUNKS) for b, h in heads]

    @pl.when(pl.program_id(0) == 0)
    def _():
        state_ref[...] = jnp.zeros_like(state_ref)

    _stage_conv_input(qkv_ref, pad_ref)
    causal, strict = _tril_masks(CHUNK)
    tril = jnp.where(causal, 1.0, 0.0).astype(BF16)
    eye = jnp.where(causal & ~strict, 1.0, 0.0)
    a_row = -jnp.exp(par_ref[0:1, :])
    dt_bias_row = par_ref[1:2, :]

    def head_cols(part, h):
        return slice(part * B_W + h * B_HEAD_DIM, part * B_W + (h + 1) * B_HEAD_DIM)

    def build(cp, carry):
        row0 = [pl.multiple_of((cp * GDN_BUILD_CHUNKS + cc) * CHUNK, CHUNK) for cc in range(GDN_BUILD_CHUNKS)]
        sm = {(cc, b): sm_ref[b, pl.ds(row0[cc], CHUNK), :] for cc in range(GDN_BUILD_CHUNKS) for b in range(bsz)}
        beta_all = {key: jax.nn.sigmoid(x) for key, x in sm.items()}
        gc_all = {key: _cumsum_rows(tril, a_row * jax.nn.softplus(x + dt_bias_row)) for key, x in sm.items()}
        gc_all_t = {key: x.T for key, x in gc_all.items()}
        q, k, v = ([_conv_silu_rows(pad_ref, cw_ref, b, row0[cc], head_cols(part, h)) for cc, b, h in units]
                   for part in range(3))
        q = [x * lax.rsqrt(jnp.sum(x * x, axis=-1, keepdims=True) + EPS) * (B_HEAD_DIM ** -0.5) for x in q]
        k = [x * lax.rsqrt(jnp.sum(x * x, axis=-1, keepdims=True) + EPS) for x in k]
        beta = [beta_all[cc, b][:, h:h + 1] for cc, b, h in units]
        gc = [gc_all[cc, b][:, B_HEADS + h:B_HEADS + h + 1] for cc, b, h in units]
        gc_row = [gc_all_t[cc, b][B_HEADS + h:B_HEADS + h + 1, :] for cc, b, h in units]
        decay = [jnp.exp(jnp.where(causal, g - gr, NEG_BIG)) for g, gr in zip(gc, gc_row)]
        kb = [x.astype(BF16) for x in k]
        kk = [_dot_nt(x, x) for x in kb]
        qk = [_dot_nt(x.astype(BF16), y) for x, y in zip(q, kb)]
        a_mat = [jnp.where(strict, bt * x * d, 0.0) for bt, x, d in zip(beta, kk, decay)]
        inv = [eye - a for a in a_mat]
        power = a_mat
        for _ in range(5):
            power = [_dot(x, x) for x in power]
            inv = [i + _dot(i, x) for i, x in zip(inv, power)]
        exp_gc = [jnp.exp(g) for g in gc]
        rhs = [jnp.concatenate([ki * (bt * e), vi * bt], axis=1) for ki, vi, bt, e in zip(k, v, beta, exp_gc)]
        wu = [_dot(i, r).astype(BF16) for i, r in zip(inv, rhs)]
        attn = [(x * d).astype(BF16) for x, d in zip(qk, decay)]
        gc_last = [g[CHUNK - 1:CHUNK, :] for g in gc]
        k_st = [(ki * jnp.exp(gl - g)).astype(BF16) for ki, gl, g in zip(k, gc_last, gc)]
        top = [_dot_tn(x, y) for x, y in zip(k_st, wu)]
        bot = [_dot(x, y) for x, y in zip(attn, wu)]
        for i in range(len(units)):
            slot = cp * len(units) + i
            lhs_ref[slot, 0:B_HEAD_DIM, :] = (-top[i][:, :B_HEAD_DIM]).astype(BF16)
            lhs_ref[slot, B_HEAD_DIM:, :] = (q[i] * exp_gc[i] - bot[i][:, :B_HEAD_DIM]).astype(BF16)
            add_ref[slot, 0:B_HEAD_DIM, :] = top[i][:, B_HEAD_DIM:]
            add_ref[slot, B_HEAD_DIM:, :] = bot[i][:, B_HEAD_DIM:]
            gl_ref[slot] = jnp.broadcast_to(jnp.exp(gc_last[i]), (SUBLANES, LANES))
        return carry

    lax.fori_loop(0, n_chunks // GDN_BUILD_CHUNKS, build, 0)

    def scan(c, carry):
        rows = pl.ds(pl.multiple_of(c * CHUNK, CHUNK), CHUNK)
        states = [state_ref[i] for i in range(len(heads))]
        res = [_dot(lhs_ref[c * len(heads) + i], states[i].astype(BF16)) + add_ref[c * len(heads) + i]
               for i in range(len(heads))]
        for i, (b, h) in enumerate(heads):
            state_ref[i] = gl_ref[c * len(heads) + i][0:1, :] * states[i] + res[i][:B_HEAD_DIM]
            zz = z_ref[b, rows, head_cols(0, h)]
            y = _rms(res[i][B_HEAD_DIM:], nw_ref[...]) * (zz * jax.nn.sigmoid(zz))
            o_ref[b, rows, head_cols(0, h)] = y.astype(o_ref.dtype)
        return carry

    lax.fori_loop(0, n_chunks, scan, 0)


def _gated_delta(qkv_raw, small, z, conv_w, a_log, dt_bias, norm_w, bsz, seq_len):
    par = jnp.zeros((2, LANES), F32)
    par = par.at[0, B_HEADS:2 * B_HEADS].set(a_log.astype(F32))
    par = par.at[1, B_HEADS:2 * B_HEADS].set(dt_bias.astype(F32))
    blk = lambda w: pl.BlockSpec((bsz, SCAN_ROWS, w), lambda t: (0, t, 0))
    n_units = bsz * B_HEADS
    n_slots = n_units * (SCAN_ROWS // CHUNK)
    out = pl.pallas_call(
        _gdn_kernel, grid=(seq_len // SCAN_ROWS,),
        in_specs=[blk(3 * B_W), blk(LANES), blk(B_W), _resident((CONV_K, 3 * B_W)), _resident((2, LANES)),
                  _resident((1, B_HEAD_DIM))],
        out_specs=blk(B_W),
        out_shape=jax.ShapeDtypeStruct((bsz, seq_len, B_W), BF16),
        scratch_shapes=[pltpu.VMEM((bsz, SUBLANES + SCAN_ROWS, 3 * B_W), F32),
                        pltpu.VMEM((n_units, B_HEAD_DIM, B_HEAD_DIM), F32),
                        pltpu.VMEM((n_slots, B_HEAD_DIM + CHUNK, B_HEAD_DIM), BF16),
                        pltpu.VMEM((n_slots, B_HEAD_DIM + CHUNK, B_HEAD_DIM), F32),
                        pltpu.VMEM((n_slots, SUBLANES, LANES), F32)],
        compiler_params=_cparams("arbitrary"), name="gated_delta",
    )(qkv_raw.reshape(bsz, seq_len, 3 * B_W), small.reshape(bsz, seq_len, LANES),
      z.reshape(bsz, seq_len, B_W), conv_w.astype(F32), par, norm_w.reshape(1, B_HEAD_DIM).astype(F32))
    return out.reshape(bsz * seq_len, B_W)


def _ssd_kernel(xbc_ref, sm_ref, z_ref, cw_ref, cb_ref, par_ref, skip_ref, nw_ref, o_ref, pad_ref, state_ref):
    bsz, rows_per_step = xbc_ref.shape[0], xbc_ref.shape[1]
    group_w = C_W // C_GROUPS
    heads_per_group = C_HEADS // C_GROUPS
    pairs_per_group = heads_per_group // 2

    @pl.when(pl.program_id(0) == 0)
    def _():
        state_ref[...] = jnp.zeros_like(state_ref)

    _stage_conv_input(xbc_ref, pad_ref)
    causal, _ = _tril_masks(CHUNK)
    tril = jnp.where(causal, 1.0, 0.0).astype(BF16)
    a_row = -jnp.exp(par_ref[0:1, :])
    dt_bias_row = par_ref[1:2, :]
    lane = lax.broadcasted_iota(jnp.int32, (CHUNK, LANES), 1)
    low_half = lane < C_HEAD_DIM

    def per_head_lanes(cols, h0):
        return jnp.where(low_half, cols[:, h0:h0 + 1], cols[:, h0 + 1:h0 + 2])

    def pair_cols(pr):
        return slice(pr * LANES, (pr + 1) * LANES)

    def chunk_pair(cp, carry):
        units = [(cc, b) for cc in range(SSD_CHUNKS) for b in range(bsz)]
        groups = [(u, g) for u in units for g in range(C_GROUPS)]
        row0 = [pl.multiple_of((cp * SSD_CHUNKS + cc) * CHUNK, CHUNK) for cc in range(SSD_CHUNKS)]
        conv = lambda u, cols: _conv_silu_rows(pad_ref, cw_ref, u[1], row0[u[0]], cols, cb_ref)
        sm = {u: sm_ref[u[1], pl.ds(row0[u[0]], CHUNK), :] for u in units}
        dt_all = {u: jax.nn.softplus(sm[u] + dt_bias_row) for u in units}
        da_cs = {u: _cumsum_rows(tril, dt_all[u] * a_row) for u in units}
        da_cs_t = {u: da_cs[u].T for u in units}
        dt_t = {u: dt_all[u].T for u in units}
        da_last = {u: da_cs[u][CHUNK - 1:CHUNK, :] for u in units}
        exp_da = {u: jnp.exp(da_cs[u]) for u in units}
        to_end = {u: jnp.exp(da_last[u] - da_cs[u]) * dt_all[u] for u in units}
        chunk_decay = {u: jnp.exp(da_last[u]) for u in units}
        x2 = {(u, pr): conv(u, pair_cols(pr)) for u in units for pr in range(C_HEADS // 2)}
        bm = {(u, g): conv(u, slice(C_W + g * C_STATE, C_W + (g + 1) * C_STATE)).astype(BF16) for u, g in groups}
        cm = {(u, g): conv(u, slice(C_W + C_BC + g * C_STATE, C_W + C_BC + (g + 1) * C_STATE)).astype(BF16)
              for u, g in groups}
        cb = {key: _dot_nt(cm[key], bm[key]) for key in groups}
        xw = {(u, g): jnp.concatenate(
            [(x2[u, g * pairs_per_group + j] * per_head_lanes(to_end[u], g * heads_per_group + 2 * j)).astype(BF16)
             for j in range(pairs_per_group)], axis=1) for u, g in groups}
        inflow = {key: _dot_tn(bm[key], xw[key]) for key in groups}
        y_diag = {}
        for u in units:
            for h in range(C_HEADS):
                seg = da_cs[u][:, h:h + 1] - da_cs_t[u][h:h + 1, :]
                wts = cb[u, h // heads_per_group] * jnp.exp(jnp.where(causal, seg, NEG_BIG)) * dt_t[u][h:h + 1, :]
                y_diag[u, h] = _dot(wts.astype(BF16), x2[u, h // 2].astype(BF16))
        for u in units:
            cc, b = u
            rows = pl.ds(row0[cc], CHUNK)
            for g in range(C_GROUPS):
                gl = slice(g * group_w, (g + 1) * group_w)
                prev = state_ref[b, :, gl]
                y_off = _dot(cm[u, g], prev.astype(BF16))
                dec = jnp.concatenate(
                    [jnp.where(low_half[0:1], chunk_decay[u][:, h0:h0 + 1], chunk_decay[u][:, h0 + 1:h0 + 2])
                     for h0 in range(g * heads_per_group, (g + 1) * heads_per_group, 2)], axis=1)
                state_ref[b, :, gl] = prev * dec + inflow[u, g]
                ys = []
                for j in range(pairs_per_group):
                    pr = g * pairs_per_group + j
                    h0 = 2 * pr
                    y = jnp.where(low_half, y_diag[u, h0], y_diag[u, h0 + 1])
                    y = y + y_off[:, pair_cols(j)] * per_head_lanes(exp_da[u], h0) + skip_ref[:, pair_cols(pr)] * x2[u, pr]
                    zz = z_ref[b, rows, pair_cols(pr)]
                    ys.append(y * (zz * jax.nn.sigmoid(zz)))
                yg = jnp.concatenate(ys, axis=1)
                o_ref[b, rows, gl] = _rms(yg, nw_ref[:, gl]).astype(o_ref.dtype)
        return carry

    lax.fori_loop(0, rows_per_step // (CHUNK * SSD_CHUNKS), chunk_pair, 0)


def _ssd(xbc_raw, small, z, conv_w, conv_b, dt_bias, a_log, d_skip, norm_w, bsz, seq_len):
    par = jnp.zeros((2, LANES), F32)
    par = par.at[0, :C_HEADS].set(a_log.astype(F32))
    par = par.at[1, :C_HEADS].set(dt_bias.astype(F32))
    skip = jnp.repeat(d_skip.astype(F32), C_HEAD_DIM).reshape(1, C_W)
    width = C_W + 2 * C_BC
    blk = lambda w: pl.BlockSpec((bsz, SCAN_ROWS, w), lambda t: (0, t, 0))
    out = pl.pallas_call(
        _ssd_kernel, grid=(seq_len // SCAN_ROWS,),
        in_specs=[blk(width), blk(LANES), blk(C_W), _resident((CONV_K, width)), _resident((1, width)),
                  _resident((2, LANES)), _resident((1, C_W)), _resident((1, C_W))],
        out_specs=blk(C_W),
        out_shape=jax.ShapeDtypeStruct((bsz, seq_len, C_W), BF16),
        scratch_shapes=[pltpu.VMEM((bsz, SUBLANES + SCAN_ROWS, width), F32),
                        pltpu.VMEM((bsz, C_STATE, C_W), F32)],
        compiler_params=_cparams("arbitrary"), name="ssd",
    )(xbc_raw.reshape(bsz, seq_len, width), small.reshape(bsz, seq_len, LANES), z.reshape(bsz, seq_len, C_W),
      conv_w.astype(F32), conv_b.astype(F32).reshape(1, width), par, skip, norm_w.reshape(1, C_W).astype(F32))
    return out.reshape(bsz * seq_len, C_W)


FOX_F_PIECES = 3


def _fox_f_lane(h):
    return (h // 2) * LANES + (D_HEAD_DIM if h % 2 == 0 else 0)


def _fox_prep_kernel(sm_ref, fb_ref, place_ref, fk_ref, carry_ref):
    tm = sm_ref.shape[0]

    @pl.when(pl.program_id(1) == 0)
    def _():
        carry_ref[...] = jnp.zeros_like(carry_ref)

    causal, _ = _tril_masks(tm)
    tril = jnp.where(causal, 1.0, 0.0).astype(BF16)
    log_f = jax.nn.log_sigmoid(sm_ref[...] + fb_ref[...])
    f_cum = _cumsum_rows(tril, log_f) + carry_ref[...]
    carry_ref[...] = f_cum[tm - 1:tm, :]
    pieces = jnp.concatenate(_split3(-LOG2_E * f_cum), axis=1)
    fk_ref[...] = _dot(pieces, place_ref[...]).astype(fk_ref.dtype)


def _fox_prep(small, f_bias, bsz, seq_len):
    fb = jnp.zeros((1, LANES), F32).at[0, D_HEADS:2 * D_HEADS].set(f_bias.astype(F32))
    place = np.zeros((FOX_F_PIECES * LANES, D_W), np.float32)
    for h in range(D_HEADS):
        for piece in range(FOX_F_PIECES):
            place[piece * LANES + D_HEADS + h, _fox_f_lane(h) + piece] = 1.0
    nt = seq_len // FOX_T
    return pl.pallas_call(
        _fox_prep_kernel, grid=(bsz, nt),
        in_specs=[pl.BlockSpec((None, FOX_T, LANES), lambda b, t: (b, t, 0)), _resident((1, LANES)),
                  _resident(place.shape)],
        out_specs=pl.BlockSpec((None, FOX_T, D_W), lambda b, t: (b, t, 0)),
        out_shape=jax.ShapeDtypeStruct((bsz, seq_len, D_W), BF16),
        scratch_shapes=[pltpu.VMEM((1, LANES), F32)],
        compiler_params=_cparams("parallel", "arbitrary"), name="fox_prep",
    )(small.reshape(bsz, seq_len, LANES), fb, jnp.asarray(place, BF16))


def _fox_kernel(q_ref, k_ref, vt_ref, fk_ref, o_ref,
                kk_ref, vv_ref, sa_ref, sb_ref, pa_ref, pb_ref, aa_ref, ab_ref, m_ref, acc_ref):
    i = pl.program_id(2)
    t = FOX_T
    nt = k_ref.shape[0] // t
    n_rb = t // FOX_ROWS
    lane = lax.broadcasted_iota(jnp.int32, (t, LANES), 1)
    low_half = lane < D_HEAD_DIM
    top_half = lax.broadcasted_iota(jnp.int32, (LANES, t), 0) < D_HEAD_DIM
    pair_lanes = lambda pr: slice(pr * LANES, (pr + 1) * LANES)

    @pl.when(i == 0)
    def _():
        for jt in range(nt):
            rows = slice(jt * t, (jt + 1) * t)
            for pr in range(FOX_PAIRS):
                kt, ft = k_ref[rows, pair_lanes(pr)], fk_ref[rows, pair_lanes(pr)]
                kk_ref[2 * pr, jt] = jnp.where(low_half, kt, ft)
                kk_ref[2 * pr + 1, jt] = jnp.where(low_half, ft, kt)
                vt = vt_ref[jt, pair_lanes(pr), :]
                one = jnp.ones_like(vt)
                vv_ref[2 * pr, jt] = jnp.where(top_half, vt, one)
                vv_ref[2 * pr + 1, jt] = jnp.where(top_half, one, vt)

    ones_upto = lambda n: jnp.where(lane < n, 1.0, 0.0).astype(BF16)
    qs = []
    for pr in range(FOX_PAIRS):
        q = q_ref[:, pair_lanes(pr)]
        qs += [jnp.where(low_half, q, ones_upto(_fox_f_lane(0) + FOX_F_PIECES)),
               jnp.where(low_half, ones_upto(_fox_f_lane(1) + FOX_F_PIECES), q)]
    key_in_blk = lax.broadcasted_iota(jnp.int32, (FOX_ROWS, t), 0)
    query = lax.broadcasted_iota(jnp.int32, (FOX_ROWS, t), 1)

    m_ref[...] = jnp.full(m_ref.shape, NEG_BIG, F32)
    acc_ref[...] = jnp.zeros_like(acc_ref)

    def scores(j, s_ref, hs):
        jj = jnp.maximum(j, 0)
        for h in hs:
            s_ref[h] = _dot_nt(kk_ref[h, jj], qs[h])

    def softmax_cols(s_ref, p_ref, a_ref, diagonal, hs):
        def block(h, rb):
            s = s_ref[h, rb * FOX_ROWS:(rb + 1) * FOX_ROWS, :]
            if diagonal:
                s = jnp.where(key_in_blk + rb * FOX_ROWS <= query, s, NEG_BIG)
            return s

        tile_max = {}
        for h in hs:
            m8 = None
            for rb in range(n_rb):
                s = block(h, rb)
                b8 = jnp.maximum(jnp.maximum(s[0:8], s[8:16]), jnp.maximum(s[16:24], s[24:32]))
                m8 = b8 if m8 is None else jnp.maximum(m8, b8)
            tile_max[h] = jnp.max(m8, axis=0, keepdims=True)
        for h in hs:
            m_old = m_ref[h]
            m_new = jnp.maximum(m_old, tile_max[h])
            m_ref[h] = m_new
            a_ref[h] = jnp.exp2(m_old - m_new)
            for rb in range(n_rb):
                p_ref[h, rb * FOX_ROWS:(rb + 1) * FOX_ROWS, :] = jnp.exp2(block(h, rb) - m_new).astype(BF16)

    def accumulate(j, p_ref, a_ref, hs):
        jj = jnp.maximum(j, 0)
        for h in hs:
            acc_ref[h] = a_ref[h] * acc_ref[h] + _dot(vv_ref[h, jj], p_ref[h])

    pair_heads = [(2 * pr, 2 * pr + 1) for pr in range(FOX_PAIRS)]
    for hs in pair_heads:
        scores(i, sb_ref, hs)
        scores(i - 1, sa_ref, hs)
    for hs in pair_heads:
        softmax_cols(sb_ref, pb_ref, ab_ref, True, hs)

    def pair(n, carry):
        j = i - 1 - 2 * n
        for hs in pair_heads:
            scores(j - 1, sb_ref, hs)
            softmax_cols(sa_ref, pa_ref, aa_ref, False, hs)
            accumulate(j + 1, pb_ref, ab_ref, hs)
        for hs in pair_heads:
            scores(j - 2, sa_ref, hs)
            softmax_cols(sb_ref, pb_ref, ab_ref, False, hs)
            accumulate(j, pa_ref, aa_ref, hs)
        return carry

    n_pairs = i // 2
    lax.fori_loop(0, n_pairs, pair, 0)
    pending = i - 2 * n_pairs

    @pl.when(pending == 1)
    def _():
        for hs in pair_heads:
            softmax_cols(sa_ref, pa_ref, aa_ref, False, hs)
            accumulate(1, pb_ref, ab_ref, hs)
            accumulate(0, pa_ref, aa_ref, hs)

    @pl.when(pending == 0)
    def _():
        for hs in pair_heads:
            accumulate(0, pb_ref, ab_ref, hs)

    for pr in range(FOX_PAIRS):
        acc0, acc1 = acc_ref[2 * pr], acc_ref[2 * pr + 1]
        out_t = jnp.where(top_half, acc0 / acc0[D_HEAD_DIM:D_HEAD_DIM + 1, :], acc1 / acc1[0:1, :])
        o_ref[:, pair_lanes(pr)] = out_t.T.astype(o_ref.dtype)


def _fox_attention(q, k, v_t, fk, bsz, seq_len):
    nt = seq_len // FOX_T
    width = FOX_PAIRS * LANES
    groups = D_W // width
    n_heads = 2 * FOX_PAIRS
    q3, k3 = (a.reshape(bsz, seq_len, D_W) for a in (q, k))
    whole_seq = pl.BlockSpec((None, seq_len, width), lambda b, g, i: (b, 0, g))
    out = pl.pallas_call(
        _fox_kernel, grid=(bsz, groups, nt),
        in_specs=[pl.BlockSpec((None, FOX_T, width), lambda b, g, i: (b, i, g)), whole_seq,
                  pl.BlockSpec((nt, width, FOX_T), lambda b, g, i: (b, g, 0)), whole_seq],
        out_specs=pl.BlockSpec((None, FOX_T, width), lambda b, g, i: (b, i, g)),
        out_shape=jax.ShapeDtypeStruct((bsz, seq_len, D_W), BF16),
        scratch_shapes=[pltpu.VMEM((n_heads, nt, FOX_T, LANES), BF16), pltpu.VMEM((n_heads, nt, LANES, FOX_T), BF16)]
        + [pltpu.VMEM((n_heads, FOX_T, FOX_T), F32)] * 2 + [pltpu.VMEM((n_heads, FOX_T, FOX_T), BF16)] * 2
        + [pltpu.VMEM((n_heads, 1, FOX_T), F32)] * 3 + [pltpu.VMEM((n_heads, LANES, FOX_T), F32)],
        compiler_params=_cparams("parallel", "parallel", "arbitrary"), name="fox_attention",
    )(q3, k3, v_t, fk)
    return out.reshape(bsz * seq_len, D_W)


def _pad_cols(w, width=LANES):
    return jnp.pad(w, ((0, 0), (0, width - w.shape[1])))


def kernel(x, norm_mix, norm_ffn, norm_final, ffn_w_gate, ffn_w_up, ffn_w_down, ab_w_in, ab_rel_bias, ab_conv_w, ab_a_log, ab_dt_bias, ab_norm_w, ab_w_out, cd_w_in, cd_conv_w, cd_conv_b, cd_dt_bias, cd_a_log, cd_d_skip, cd_norm_w, cd_f_bias, cd_w_out):
    bsz, seq_len, d = x.shape
    n = bsz * seq_len
    xf = x.reshape(n, d)
    bf = lambda w: w.astype(BF16)

    w_in = ab_w_in[0]
    o = np.cumsum([0, A_W, A_W, A_W, 3 * B_W, B_HEADS, B_HEADS, B_W])
    weights = [bf(w_in[:, o[0]:o[1]] * (LOG2_E * A_HEAD_DIM ** -0.5)), bf(w_in[:, o[1]:o[2]]), bf(w_in[:, o[2]:o[3]].T),
               bf(w_in[:, o[3]:o[4]]), bf(_pad_cols(w_in[:, o[4]:o[6]])), bf(w_in[:, o[6]:o[7]])]
    a_q, a_k, a_vt, b_qkv, b_small, b_z = _norm_proj(
        xf, norm_mix[0], weights, [BF16, BF16, BF16, F32, F32, F32], transposed=(False, False, True) + (False,) * 3)
    o_a = _band_attention(a_q, a_k, a_vt, ab_rel_bias[0], bsz, seq_len)
    o_b = _gated_delta(b_qkv, b_small, b_z, ab_conv_w[0], ab_a_log[0], ab_dt_bias[0], ab_norm_w[0], bsz, seq_len)
    ffn_w = (bf(ffn_w_gate), bf(ffn_w_up), bf(ffn_w_down))
    xf = _layer_tail(xf, o_a, o_b, bf(ab_w_out), norm_ffn[0], *ffn_w, 0, norm_final, False)

    w_in = cd_w_in[0]
    o = np.cumsum([0, C_W, C_W + 2 * C_BC, C_HEADS, D_W, D_W, D_W, D_HEADS])
    small_w = jnp.concatenate([w_in[:, o[2]:o[3]], w_in[:, o[6]:o[7]]], axis=1)
    weights = [bf(w_in[:, o[0]:o[1]]), bf(w_in[:, o[1]:o[2]]), bf(_pad_cols(small_w)),
               bf(w_in[:, o[3]:o[4]] * (LOG2_E * D_HEAD_DIM ** -0.5)), bf(w_in[:, o[4]:o[5]]), bf(w_in[:, o[5]:o[6]].T)]
    c_z, c_xbc, cd_small, d_q, d_k, d_vt = _norm_proj(
        xf, norm_mix[1], weights, [F32, F32, F32, BF16, BF16, BF16], transposed=(False,) * 5 + (True,))
    y_c = _ssd(c_xbc, cd_small, c_z, cd_conv_w[0], cd_conv_b[0], cd_dt_bias[0], cd_a_log[0], cd_d_skip[0],
               cd_norm_w[0], bsz, seq_len)
    fk = _fox_prep(cd_small, cd_f_bias[0], bsz, seq_len)
    o_d = _fox_attention(d_q, d_k, d_vt, fk, bsz, seq_len)
    xf = _layer_tail(xf, y_c, o_d, bf(cd_w_out), norm_ffn[1], *ffn_w, 1, norm_final, True)
    return xf.reshape(bsz, seq_len, d)
```

```python
import functools

import jax
import jax.numpy as jnp
import numpy as np
from jax import lax
from jax.experimental import pallas as pl
from jax.experimental.pallas import tpu as pltpu

F32 = jnp.float32
BF16 = jnp.bfloat16

D_MODEL = 1024
CHUNK = 64
EPS = 1e-6
CONV_K = 4
A_HEADS, A_HEAD_DIM, A_LEFT_CHUNKS, A_MAX_REL = 8, 64, 8, 256
B_HEADS, B_HEAD_DIM = 4, 128
C_HEADS, C_HEAD_DIM, C_GROUPS, C_STATE = 8, 64, 2, 128
D_HEADS, D_HEAD_DIM = 8, 64
A_W = A_HEADS * A_HEAD_DIM
B_W = B_HEADS * B_HEAD_DIM
C_W = C_HEADS * C_HEAD_DIM
D_W = D_HEADS * D_HEAD_DIM
C_BC = C_GROUPS * C_STATE

LANES = 128
SUBLANES = 8
VMEM_LIMIT_BYTES = 56 * 1024 * 1024
NEG_BIG = -1e30
LOG2_E = float(np.log2(np.e))

ROW_TILE = 512
TAIL_ROWS = 1024
FFN_CHUNK = 256
BAND_TQ = 256
BAND_WIN = BAND_TQ + A_LEFT_CHUNKS * CHUNK
FOX_T = 512
FOX_ROWS = 32
FOX_PAIRS = 2
SCAN_ROWS = 512
SSD_CHUNKS = 4
GDN_BUILD_CHUNKS = 4


def _cparams(*sem):
    return pltpu.CompilerParams(dimension_semantics=sem, vmem_limit_bytes=VMEM_LIMIT_BYTES)


def _resident(shape):
    nd = len(shape)
    return pl.BlockSpec(shape, lambda *_: (0,) * nd, pipeline_mode=pl.Buffered(1))


def _dot(a, b):
    return jnp.dot(a, b, preferred_element_type=F32)


def _dot_nt(a, b):
    return lax.dot_general(a, b, (((1,), (1,)), ((), ())), preferred_element_type=F32)


def _dot_tn(a, b):
    return lax.dot_general(a, b, (((0,), (0,)), ((), ())), preferred_element_type=F32)


def _rms(x, w):
    return x * lax.rsqrt(jnp.mean(x * x, axis=-1, keepdims=True) + EPS) * w


def _split3(x):
    hi = x.astype(BF16)
    r1 = x - hi.astype(F32)
    mid = r1.astype(BF16)
    lo = (r1 - mid.astype(F32)).astype(BF16)
    return hi, mid, lo


def _cumsum_rows(tril, x):
    hi, mid, lo = _split3(x)
    return _dot(tril, hi) + _dot(tril, mid) + _dot(tril, lo)


def _tril_masks(n):
    r = lax.broadcasted_iota(jnp.int32, (n, n), 0)
    c = lax.broadcasted_iota(jnp.int32, (n, n), 1)
    return r >= c, r > c


def _norm_proj_kernel(x_ref, nw_ref, *refs, transposed):
    n_out = len(refs) // 2
    h = _rms(x_ref[...], nw_ref[...]).astype(BF16)
    for w_ref, o_ref, tr in zip(refs[:n_out], refs[n_out:], transposed):
        out = _dot_nt(w_ref[...], h) if tr else _dot(h, w_ref[...])
        o_ref[...] = out.astype(o_ref.dtype)


def _norm_proj(x, norm_w, weights, out_dtypes, transposed=None):
    n, d = x.shape
    transposed = transposed or (False,) * len(weights)
    in_specs = [pl.BlockSpec((ROW_TILE, d), lambda i: (i, 0)), _resident((1, d))]
    in_specs += [_resident(w.shape) for w in weights]
    out_shape, out_specs = [], []
    for w, dt, tr in zip(weights, out_dtypes, transposed):
        if tr:
            out_shape.append(jax.ShapeDtypeStruct((n // ROW_TILE, w.shape[0], ROW_TILE), dt))
            out_specs.append(pl.BlockSpec((None, w.shape[0], ROW_TILE), lambda i: (i, 0, 0)))
        else:
            out_shape.append(jax.ShapeDtypeStruct((n, w.shape[1]), dt))
            out_specs.append(pl.BlockSpec((ROW_TILE, w.shape[1]), lambda i: (i, 0)))
    return pl.pallas_call(
        functools.partial(_norm_proj_kernel, transposed=tuple(transposed)),
        grid=(n // ROW_TILE,), in_specs=in_specs, out_specs=out_specs,
        out_shape=out_shape, compiler_params=_cparams("parallel"), name="norm_proj",
    )(x, norm_w.reshape(1, d), *weights)


def _tail_kernel(x_ref, a_ref, b_ref, woa_ref, wob_ref, nw_ref, wg_ref, wu_ref, wd_ref, fn_ref,
                 o_ref, acc_ref, *, d_ff, final_norm):
    x1 = x_ref[...] + _dot(a_ref[...], woa_ref[...]) + _dot(b_ref[...], wob_ref[...])
    h = _rms(x1, nw_ref[...]).astype(BF16)
    acc_ref[...] = x1
    ffn = None
    for c in range(d_ff // FFN_CHUNK):
        cols = slice(c * FFN_CHUNK, (c + 1) * FFN_CHUNK)
        g = _dot(h, wg_ref[:, cols])
        u = _dot(h, wu_ref[:, cols])
        part = _dot((g * jax.nn.sigmoid(g) * u).astype(BF16), wd_ref[cols, :])
        ffn = part if ffn is None else ffn + part
    y = acc_ref[...] + ffn
    if final_norm:
        y = _rms(y, fn_ref[...])
    o_ref[...] = y


def _layer_tail(x, mix_a, mix_b, w_out, norm_w, wg, wu, wd, layer, final_w, final_norm):
    n, d = x.shape
    d_ff = wg.shape[2]
    wa, wb = mix_a.shape[1], mix_b.shape[1]
    assert wa == wb and w_out.shape == (1, wa + wb, d)
    row = lambda w: pl.BlockSpec((TAIL_ROWS, w), lambda i: (i, 0))
    pick = lambda shape, idx: pl.BlockSpec((None,) + shape, lambda i: idx, pipeline_mode=pl.Buffered(1))
    in_specs = [row(d), row(wa), row(wb), pick((wa, d), (0, 0, 0)), pick((wb, d), (0, 1, 0)), _resident((1, d)),
                pick((d, d_ff), (layer, 0, 0)), pick((d, d_ff), (layer, 0, 0)), pick((d_ff, d), (layer, 0, 0)),
                _resident((1, d))]
    return pl.pallas_call(
        functools.partial(_tail_kernel, d_ff=d_ff, final_norm=final_norm),
        grid=(n // TAIL_ROWS,), in_specs=in_specs, out_specs=row(d),
        out_shape=jax.ShapeDtypeStruct((n, d), F32),
        scratch_shapes=[pltpu.VMEM((TAIL_ROWS, d), F32)],
        compiler_params=_cparams("parallel"), name="layer_tail",
    )(x, mix_a, mix_b, w_out, w_out, norm_w.reshape(1, d), wg, wu, wd, final_w.reshape(1, d))


def _band_kernel(q_ref, *refs):
    n_blk = BAND_WIN // BAND_TQ
    k_refs, vt_refs = refs[:n_blk], refs[n_blk:2 * n_blk]
    by_offset_ref, o_ref, bias_ref, s_ref, p_ref = refs[2 * n_blk:]
    i = pl.program_id(1)

    @pl.when((pl.program_id(0) == 0) & (i == 0))
    def _():
        _fill_band_bias(by_offset_ref, bias_ref)

    lane = lax.broadcasted_iota(jnp.int32, (BAND_TQ, LANES), 1)
    low_half = lane < A_HEAD_DIM
    top_half = lax.broadcasted_iota(jnp.int32, (LANES, BAND_WIN), 0) < A_HEAD_DIM
    pairs = A_W // LANES
    n_rb = BAND_WIN // FOX_ROWS
    fill = [jnp.where(i - (n_blk - 1) + j >= 0, 0.0, NEG_BIG) for j in range(n_blk)]
    one_at = lambda n: jnp.where(lane == n, 1.0, 0.0).astype(BF16)

    v_aug = []
    for pr in range(pairs):
        cols = slice(pr * LANES, (pr + 1) * LANES)
        q = q_ref[:, cols]
        k_blocks = [r[:, cols] for r in k_refs]
        k_even = jnp.concatenate([jnp.where(low_half, kb, f.astype(BF16)) for kb, f in zip(k_blocks, fill)], axis=0)
        k_odd = jnp.concatenate([jnp.where(low_half, f.astype(BF16), kb) for kb, f in zip(k_blocks, fill)], axis=0)
        s_ref[2 * pr] = _dot_nt(k_even, jnp.where(low_half, q, one_at(A_HEAD_DIM))) + bias_ref[2 * pr]
        s_ref[2 * pr + 1] = _dot_nt(k_odd, jnp.where(low_half, one_at(0), q)) + bias_ref[2 * pr + 1]
        vt = jnp.concatenate([r[cols, :] for r in vt_refs], axis=1)
        one = jnp.ones_like(vt)
        v_aug += [jnp.where(top_half, vt, one), jnp.where(top_half, one, vt)]

    rows = lambda rb: slice(rb * FOX_ROWS, (rb + 1) * FOX_ROWS)
    col_max = []
    for h in range(A_HEADS):
        m8 = None
        for rb in range(n_rb):
            s = s_ref[h, rows(rb), :]
            b8 = jnp.maximum(jnp.maximum(s[0:8], s[8:16]), jnp.maximum(s[16:24], s[24:32]))
            m8 = b8 if m8 is None else jnp.maximum(m8, b8)
        col_max.append(jnp.max(m8, axis=0, keepdims=True))
    for h in range(A_HEADS):
        for rb in range(n_rb):
            p_ref[h, rows(rb), :] = jnp.exp2(s_ref[h, rows(rb), :] - col_max[h]).astype(BF16)

    out_top = lax.broadcasted_iota(jnp.int32, (LANES, BAND_TQ), 0) < A_HEAD_DIM
    for pr in range(pairs):
        o_even = _dot(v_aug[2 * pr], p_ref[2 * pr])
        o_odd = _dot(v_aug[2 * pr + 1], p_ref[2 * pr + 1])
        out_t = jnp.where(out_top, o_even / o_even[A_HEAD_DIM:A_HEAD_DIM + 1, :], o_odd / o_odd[0:1, :])
        o_ref[:, pr * LANES:(pr + 1) * LANES] = out_t.T.astype(o_ref.dtype)


BAND_BIAS_PERIOD = BAND_WIN + BAND_TQ


def _band_bias_by_offset(rel_bias):
    d = np.arange(BAND_BIAS_PERIOD) - BAND_WIN
    idx = np.clip(d + A_LEFT_CHUNKS * CHUNK, -A_MAX_REL, A_MAX_REL) + A_MAX_REL
    return rel_bias.astype(F32)[:, idx] * LOG2_E


def _fill_band_bias(by_offset_ref, bias_ref):
    kc = lax.broadcasted_iota(jnp.int32, (BAND_WIN, BAND_TQ), 0) // CHUNK
    qc = lax.broadcasted_iota(jnp.int32, (BAND_WIN, BAND_TQ), 1) // CHUNK
    for h in range(A_HEADS):
        spread = jnp.broadcast_to(by_offset_ref[h:h + 1, :], (BAND_WIN, BAND_BIAS_PERIOD))
        skewed = pltpu.roll(spread, 0, axis=1, stride=1, stride_axis=0)[:, BAND_WIN:]
        bias_ref[h] = jnp.where(kc >= qc, jnp.where(kc <= qc + A_LEFT_CHUNKS, skewed, NEG_BIG), NEG_BIG)


def _band_attention(q, k, v_t, rel_bias, bsz, seq_len):
    n_blk = BAND_WIN // BAND_TQ
    nq = seq_len // BAND_TQ
    per_tile = ROW_TILE // BAND_TQ
    tiles_per_seq = seq_len // ROW_TILE
    q3, k3 = (a.reshape(bsz, seq_len, A_W) for a in (q, k))
    blk = (None, BAND_TQ, A_W)
    key_blk = lambda i, j: jnp.maximum(i - (n_blk - 1) + j, 0)
    k_specs = [pl.BlockSpec(blk, lambda b, i, j=j: (b, key_blk(i, j), 0)) for j in range(n_blk)]
    vt_specs = [pl.BlockSpec((None, A_W, BAND_TQ),
                             lambda b, i, j=j: (b * tiles_per_seq + key_blk(i, j) // per_tile, 0, key_blk(i, j) % per_tile))
                for j in range(n_blk)]
    out = pl.pallas_call(
        _band_kernel, grid=(bsz, nq),
        in_specs=[pl.BlockSpec(blk, lambda b, i: (b, i, 0))] + k_specs + vt_specs
        + [_resident((A_HEADS, BAND_BIAS_PERIOD))],
        out_specs=pl.BlockSpec(blk, lambda b, i: (b, i, 0)),
        out_shape=jax.ShapeDtypeStruct((bsz, seq_len, A_W), BF16),
        scratch_shapes=[pltpu.VMEM((A_HEADS, BAND_WIN, BAND_TQ), F32)] * 2
        + [pltpu.VMEM((A_HEADS, BAND_WIN, BAND_TQ), BF16)],
        compiler_params=_cparams("arbitrary", "arbitrary"), name="band_attention",
    )(q3, *([k3] * n_blk), *([v_t] * n_blk), _band_bias_by_offset(rel_bias))
    return out.reshape(bsz * seq_len, A_W)


def _stage_conv_input(x_ref, pad_ref):
    rows = x_ref.shape[1]

    @pl.when(pl.program_id(0) == 0)
    def _():
        pad_ref[:, 0:SUBLANES, :] = jnp.zeros((pad_ref.shape[0], SUBLANES, pad_ref.shape[2]), F32)

    @pl.when(pl.program_id(0) > 0)
    def _():
        pad_ref[:, 0:SUBLANES, :] = pad_ref[:, rows:rows + SUBLANES, :]

    pad_ref[:, SUBLANES:, :] = x_ref[...]


def _conv_silu_rows(pad_ref, w_ref, b, row0, cols, bias_ref=None):
    window = pad_ref[b, pl.ds(row0, SUBLANES + CHUNK), cols]
    acc = None
    for back in range(CONV_K):
        term = w_ref[CONV_K - 1 - back:CONV_K - back, cols] * window[SUBLANES - back:SUBLANES - back + CHUNK]
        acc = term if acc is None else acc + term
    if bias_ref is not None:
        acc = acc + bias_ref[:, cols]
    return acc * jax.nn.sigmoid(acc)


def _gdn_kernel(qkv_ref, sm_ref, z_ref, cw_ref, par_ref, nw_ref, o_ref,
                pad_ref, state_ref, lhs_ref, add_ref, gl_ref):
    bsz, rows_per_step = qkv_ref.shape[0], qkv_ref.shape[1]
    n_chunks = rows_per_step // CHUNK
    heads = [(b, h) for b in range(bsz) for h in range(B_HEADS)]
    units = [(cc, b, h) for cc in range(GDN_BUILD_CHUNKS) for b, h in heads]

    @pl.when(pl.program_id(0) == 0)
    def _():
        state_ref[...] = jnp.zeros_like(state_ref)

    _stage_conv_input(qkv_ref, pad_ref)
    causal, strict = _tril_masks(CHUNK)
    tril = jnp.where(causal, 1.0, 0.0).astype(BF16)
    eye = jnp.where(causal & ~strict, 1.0, 0.0)
    a_row = -jnp.exp(par_ref[0:1, :])
    dt_bias_row = par_ref[1:2, :]

    def head_cols(part, h):
        return slice(part * B_W + h * B_HEAD_DIM, part * B_W + (h + 1) * B_HEAD_DIM)

    def build(cp, carry):
        row0 = [pl.multiple_of((cp * GDN_BUILD_CHUNKS + cc) * CHUNK, CHUNK) for cc in range(GDN_BUILD_CHUNKS)]
        sm = {(cc, b): sm_ref[b, pl.ds(row0[cc], CHUNK), :] for cc in range(GDN_BUILD_CHUNKS) for b in range(bsz)}
        beta_all = {key: jax.nn.sigmoid(x) for key, x in sm.items()}
        gc_all = {key: _cumsum_rows(tril, a_row * jax.nn.softplus(x + dt_bias_row)) for key, x in sm.items()}
        gc_all_t = {key: x.T for key, x in gc_all.items()}
        q, k, v = ([_conv_silu_rows(pad_ref, cw_ref, b, row0[cc], head_cols(part, h)) for cc, b, h in units]
                   for part in range(3))
        q = [x * lax.rsqrt(jnp.sum(x * x, axis=-1, keepdims=True) + EPS) * (B_HEAD_DIM ** -0.5) for x in q]
        k = [x * lax.rsqrt(jnp.sum(x * x, axis=-1, keepdims=True) + EPS) for x in k]
        beta = [beta_all[cc, b][:, h:h + 1] for cc, b, h in units]
        gc = [gc_all[cc, b][:, B_HEADS + h:B_HEADS + h + 1] for cc, b, h in units]
        gc_row = [gc_all_t[cc, b][B_HEADS + h:B_HEADS + h + 1, :] for cc, b, h in units]
        decay = [jnp.exp(jnp.where(causal, g - gr, NEG_BIG)) for g, gr in zip(gc, gc_row)]
        kb = [x.astype(BF16) for x in k]
        kk = [_dot_nt(x, x) for x in kb]
        qk = [_dot_nt(x.astype(BF16), y) for x, y in zip(q, kb)]
        a_mat = [jnp.where(strict, bt * x * d, 0.0) for bt, x, d in zip(beta, kk, decay)]
        inv = [eye - a for a in a_mat]
        power = a_mat
        for _ in range(5):
            power = [_dot(x, x) for x in power]
            inv = [i + _dot(i, x) for i, x in zip(inv, power)]
        exp_gc = [jnp.exp(g) for g in gc]
        rhs = [jnp.concatenate([ki * (bt * e), vi * bt], axis=1) for ki, vi, bt, e in zip(k, v, beta, exp_gc)]
        wu = [_dot(i, r).astype(BF16) for i, r in zip(inv, rhs)]
        attn = [(x * d).astype(BF16) for x, d in zip(qk, decay)]
        gc_last = [g[CHUNK - 1:CHUNK, :] for g in gc]
        k_st = [(ki * jnp.exp(gl - g)).astype(BF16) for ki, gl, g in zip(k, gc_last, gc)]
        top = [_dot_tn(x, y) for x, y in zip(k_st, wu)]
        bot = [_dot(x, y) for x, y in zip(attn, wu)]
        for i in range(len(units)):
            slot = cp * len(units) + i
            lhs_ref[slot, 0:B_HEAD_DIM, :] = (-top[i][:, :B_HEAD_DIM]).astype(BF16)
            lhs_ref[slot, B_HEAD_DIM:, :] = (q[i] * exp_gc[i] - bot[i][:, :B_HEAD_DIM]).astype(BF16)
            add_ref[slot, 0:B_HEAD_DIM, :] = top[i][:, B_HEAD_DIM:]
            add_ref[slot, B_HEAD_DIM:, :] = bot[i][:, B_HEAD_DIM:]
            gl_ref[slot] = jnp.broadcast_to(jnp.exp(gc_last[i]), (SUBLANES, LANES))
        return carry

    lax.fori_loop(0, n_chunks // GDN_BUILD_CHUNKS, build, 0)

    def scan(c, carry):
        rows = pl.ds(pl.multiple_of(c * CHUNK, CHUNK), CHUNK)
        states = [state_ref[i] for i in range(len(heads))]
        res = [_dot(lhs_ref[c * len(heads) + i], states[i].astype(BF16)) + add_ref[c * len(heads) + i]
               for i in range(len(heads))]
        for i, (b, h) in enumerate(heads):
            state_ref[i] = gl_ref[c * len(heads) + i][0:1, :] * states[i] + res[i][:B_HEAD_DIM]
            zz = z_ref[b, rows, head_cols(0, h)]
            y = _rms(res[i][B_HEAD_DIM:], nw_ref[...]) * (zz * jax.nn.sigmoid(zz))
            o_ref[b, rows, head_cols(0, h)] = y.astype(o_ref.dtype)
        return carry

    lax.fori_loop(0, n_chunks, scan, 0)


def _gated_delta(qkv_raw, small, z, conv_w, a_log, dt_bias, norm_w, bsz, seq_len):
    par = jnp.zeros((2, LANES), F32)
    par = par.at[0, B_HEADS:2 * B_HEADS].set(a_log.astype(F32))
    par = par.at[1, B_HEADS:2 * B_HEADS].set(dt_bias.astype(F32))
    blk = lambda w: pl.BlockSpec((bsz, SCAN_ROWS, w), lambda t: (0, t, 0))
    n_units = bsz * B_HEADS
    n_slots = n_units * (SCAN_ROWS // CHUNK)
    out = pl.pallas_call(
        _gdn_kernel, grid=(seq_len // SCAN_ROWS,),
        in_specs=[blk(3 * B_W), blk(LANES), blk(B_W), _resident((CONV_K, 3 * B_W)), _resident((2, LANES)),
                  _resident((1, B_HEAD_DIM))],
        out_specs=blk(B_W),
        out_shape=jax.ShapeDtypeStruct((bsz, seq_len, B_W), BF16),
        scratch_shapes=[pltpu.VMEM((bsz, SUBLANES + SCAN_ROWS, 3 * B_W), F32),
                        pltpu.VMEM((n_units, B_HEAD_DIM, B_HEAD_DIM), F32),
                        pltpu.VMEM((n_slots, B_HEAD_DIM + CHUNK, B_HEAD_DIM), BF16),
                        pltpu.VMEM((n_slots, B_HEAD_DIM + CHUNK, B_HEAD_DIM), F32),
                        pltpu.VMEM((n_slots, SUBLANES, LANES), F32)],
        compiler_params=_cparams("arbitrary"), name="gated_delta",
    )(qkv_raw.reshape(bsz, seq_len, 3 * B_W), small.reshape(bsz, seq_len, LANES),
      z.reshape(bsz, seq_len, B_W), conv_w.astype(F32), par, norm_w.reshape(1, B_HEAD_DIM).astype(F32))
    return out.reshape(bsz * seq_len, B_W)


def _ssd_kernel(xbc_ref, sm_ref, z_ref, cw_ref, cb_ref, par_ref, skip_ref, nw_ref, o_ref, pad_ref, state_ref):
    bsz, rows_per_step = xbc_ref.shape[0], xbc_ref.shape[1]
    group_w = C_W // C_GROUPS
    heads_per_group = C_HEADS // C_GROUPS
    pairs_per_group = heads_per_group // 2

    @pl.when(pl.program_id(0) == 0)
    def _():
        state_ref[...] = jnp.zeros_like(state_ref)

    _stage_conv_input(xbc_ref, pad_ref)
    causal, _ = _tril_masks(CHUNK)
    tril = jnp.where(causal, 1.0, 0.0).astype(BF16)
    a_row = -jnp.exp(par_ref[0:1, :])
    dt_bias_row = par_ref[1:2, :]
    lane = lax.broadcasted_iota(jnp.int32, (CHUNK, LANES), 1)
    low_half = lane < C_HEAD_DIM

    def per_head_lanes(cols, h0):
        return jnp.where(low_half, cols[:, h0:h0 + 1], cols[:, h0 + 1:h0 + 2])

    def pair_cols(pr):
        return slice(pr * LANES, (pr + 1) * LANES)

    def chunk_pair(cp, carry):
        units = [(cc, b) for cc in range(SSD_CHUNKS) for b in range(bsz)]
        groups = [(u, g) for u in units for g in range(C_GROUPS)]
        row0 = [pl.multiple_of((cp * SSD_CHUNKS + cc) * CHUNK, CHUNK) for cc in range(SSD_CHUNKS)]
        conv = lambda u, cols: _conv_silu_rows(pad_ref, cw_ref, u[1], row0[u[0]], cols, cb_ref)
        sm = {u: sm_ref[u[1], pl.ds(row0[u[0]], CHUNK), :] for u in units}
        dt_all = {u: jax.nn.softplus(sm[u] + dt_bias_row) for u in units}
        da_cs = {u: _cumsum_rows(tril, dt_all[u] * a_row) for u in units}
        da_cs_t = {u: da_cs[u].T for u in units}
        dt_t = {u: dt_all[u].T for u in units}
        da_last = {u: da_cs[u][CHUNK - 1:CHUNK, :] for u in units}
        exp_da = {u: jnp.exp(da_cs[u]) for u in units}
        to_end = {u: jnp.exp(da_last[u] - da_cs[u]) * dt_all[u] for u in units}
        chunk_decay = {u: jnp.exp(da_last[u]) for u in units}
        x2 = {(u, pr): conv(u, pair_cols(pr)) for u in units for pr in range(C_HEADS // 2)}
        bm = {(u, g): conv(u, slice(C_W + g * C_STATE, C_W + (g + 1) * C_STATE)).astype(BF16) for u, g in groups}
        cm = {(u, g): conv(u, slice(C_W + C_BC + g * C_STATE, C_W + C_BC + (g + 1) * C_STATE)).astype(BF16)
              for u, g in groups}
        cb = {key: _dot_nt(cm[key], bm[key]) for key in groups}
        xw = {(u, g): jnp.concatenate(
            [(x2[u, g * pairs_per_group + j] * per_head_lanes(to_end[u], g * heads_per_group + 2 * j)).astype(BF16)
             for j in range(pairs_per_group)], axis=1) for u, g in groups}
        inflow = {key: _dot_tn(bm[key], xw[key]) for key in groups}
        y_diag = {}
        for u in units:
            for h in range(C_HEADS):
                seg = da_cs[u][:, h:h + 1] - da_cs_t[u][h:h + 1, :]
                wts = cb[u, h // heads_per_group] * jnp.exp(jnp.where(causal, seg, NEG_BIG)) * dt_t[u][h:h + 1, :]
                y_diag[u, h] = _dot(wts.astype(BF16), x2[u, h // 2].astype(BF16))
        for u in units:
            cc, b = u
            rows = pl.ds(row0[cc], CHUNK)
            for g in range(C_GROUPS):
                gl = slice(g * group_w, (g + 1) * group_w)
                prev = state_ref[b, :, gl]
                y_off = _dot(cm[u, g], prev.astype(BF16))
                dec = jnp.concatenate(
                    [jnp.where(low_half[0:1], chunk_decay[u][:, h0:h0 + 1], chunk_decay[u][:, h0 + 1:h0 + 2])
                     for h0 in range(g * heads_per_group, (g + 1) * heads_per_group, 2)], axis=1)
                state_ref[b, :, gl] = prev * dec + inflow[u, g]
                ys = []
                for j in range(pairs_per_group):
                    pr = g * pairs_per_group + j
                    h0 = 2 * pr
                    y = jnp.where(low_half, y_diag[u, h0], y_diag[u, h0 + 1])
                    y = y + y_off[:, pair_cols(j)] * per_head_lanes(exp_da[u], h0) + skip_ref[:, pair_cols(pr)] * x2[u, pr]
                    zz = z_ref[b, rows, pair_cols(pr)]
                    ys.append(y * (zz * jax.nn.sigmoid(zz)))
                yg = jnp.concatenate(ys, axis=1)
                o_ref[b, rows, gl] = _rms(yg, nw_ref[:, gl]).astype(o_ref.dtype)
        return carry

    lax.fori_loop(0, rows_per_step // (CHUNK * SSD_CHUNKS), chunk_pair, 0)


def _ssd(xbc_raw, small, z, conv_w, conv_b, dt_bias, a_log, d_skip, norm_w, bsz, seq_len):
    par = jnp.zeros((2, LANES), F32)
    par = par.at[0, :C_HEADS].set(a_log.astype(F32))
    par = par.at[1, :C_HEADS].set(dt_bias.astype(F32))
    skip = jnp.repeat(d_skip.astype(F32), C_HEAD_DIM).reshape(1, C_W)
    width = C_W + 2 * C_BC
    blk = lambda w: pl.BlockSpec((bsz, SCAN_ROWS, w), lambda t: (0, t, 0))
    out = pl.pallas_call(
        _ssd_kernel, grid=(seq_len // SCAN_ROWS,),
        in_specs=[blk(width), blk(LANES), blk(C_W), _resident((CONV_K, width)), _resident((1, width)),
                  _resident((2, LANES)), _resident((1, C_W)), _resident((1, C_W))],
        out_specs=blk(C_W),
        out_shape=jax.ShapeDtypeStruct((bsz, seq_len, C_W), BF16),
        scratch_shapes=[pltpu.VMEM((bsz, SUBLANES + SCAN_ROWS, width), F32),
                        pltpu.VMEM((bsz, C_STATE, C_W), F32)],
        compiler_params=_cparams("arbitrary"), name="ssd",
    )(xbc_raw.reshape(bsz, seq_len, width), small.reshape(bsz, seq_len, LANES), z.reshape(bsz, seq_len, C_W),
      conv_w.astype(F32), conv_b.astype(F32).reshape(1, width), par, skip, norm_w.reshape(1, C_W).astype(F32))
    return out.reshape(bsz * seq_len, C_W)


FOX_F_PIECES = 3


def _fox_f_lane(h):
    return (h // 2) * LANES + (D_HEAD_DIM if h % 2 == 0 else 0)


def _fox_prep_kernel(sm_ref, fb_ref, place_ref, fk_ref, carry_ref):
    tm = sm_ref.shape[0]

    @pl.when(pl.program_id(1) == 0)
    def _():
        carry_ref[...] = jnp.zeros_like(carry_ref)

    causal, _ = _tril_masks(tm)
    tril = jnp.where(causal, 1.0, 0.0).astype(BF16)
    log_f = jax.nn.log_sigmoid(sm_ref[...] + fb_ref[...])
    f_cum = _cumsum_rows(tril, log_f) + carry_ref[...]
    carry_ref[...] = f_cum[tm - 1:tm, :]
    pieces = jnp.concatenate(_split3(-LOG2_E * f_cum), axis=1)
    fk_ref[...] = _dot(pieces, place_ref[...]).astype(fk_ref.dtype)


def _fox_prep(small, f_bias, bsz, seq_len):
    fb = jnp.zeros((1, LANES), F32).at[0, D_HEADS:2 * D_HEADS].set(f_bias.astype(F32))
    place = np.zeros((FOX_F_PIECES * LANES, LANES), np.float32)
    for h in range(D_HEADS):
        for piece in range(FOX_F_PIECES):
            place[piece * LANES + D_HEADS + h, piece * D_HEADS + h] = 1.0
    nt = seq_len // FOX_T
    return pl.pallas_call(
        _fox_prep_kernel, grid=(bsz, nt),
        in_specs=[pl.BlockSpec((None, FOX_T, LANES), lambda b, t: (b, t, 0)), _resident((1, LANES)),
                  _resident(place.shape)],
        out_specs=pl.BlockSpec((None, FOX_T, LANES), lambda b, t: (b, t, 0)),
        out_shape=jax.ShapeDtypeStruct((bsz, seq_len, LANES), BF16),
        scratch_shapes=[pltpu.VMEM((1, LANES), F32)],
        compiler_params=_cparams("parallel", "arbitrary"), name="fox_prep",
    )(small.reshape(bsz, seq_len, LANES), fb, jnp.asarray(place, BF16))


def _fox_kernel(q_ref, k_ref, vt_ref, fk_ref, place_ref, o_ref,
                kk_ref, vv_ref, sa_ref, sb_ref, pa_ref, pb_ref, aa_ref, ab_ref, m_ref, acc_ref):
    i = pl.program_id(2)
    t = FOX_T
    nt = k_ref.shape[0] // t
    n_rb = t // FOX_ROWS
    lane = lax.broadcasted_iota(jnp.int32, (t, LANES), 1)
    low_half = lane < D_HEAD_DIM
    top_half = lax.broadcasted_iota(jnp.int32, (LANES, t), 0) < D_HEAD_DIM
    pair_lanes = lambda pr: slice(pr * LANES, (pr + 1) * LANES)

    @pl.when(i == 0)
    def _():
        for jt in range(nt):
            rows = slice(jt * t, (jt + 1) * t)
            for pr in range(FOX_PAIRS):
                kt = k_ref[rows, pair_lanes(pr)]
                ft = _dot(fk_ref[rows, :], place_ref[:, pair_lanes(pr)]).astype(BF16)
                kk_ref[2 * pr, jt] = jnp.where(low_half, kt, ft)
                kk_ref[2 * pr + 1, jt] = jnp.where(low_half, ft, kt)
                vt = vt_ref[jt, pair_lanes(pr), :]
                one = jnp.ones_like(vt)
                vv_ref[2 * pr, jt] = jnp.where(top_half, vt, one)
                vv_ref[2 * pr + 1, jt] = jnp.where(top_half, one, vt)

    ones_upto = lambda n: jnp.where(lane < n, 1.0, 0.0).astype(BF16)
    qs = []
    for pr in range(FOX_PAIRS):
        q = q_ref[:, pair_lanes(pr)]
        qs += [jnp.where(low_half, q, ones_upto(_fox_f_lane(0) + FOX_F_PIECES)),
               jnp.where(low_half, ones_upto(_fox_f_lane(1) + FOX_F_PIECES), q)]
    key_in_blk = lax.broadcasted_iota(jnp.int32, (FOX_ROWS, t), 0)
    query = lax.broadcasted_iota(jnp.int32, (FOX_ROWS, t), 1)

    m_ref[...] = jnp.full(m_ref.shape, NEG_BIG, F32)
    acc_ref[...] = jnp.zeros_like(acc_ref)

    def scores(j, s_ref, hs):
        jj = jnp.maximum(j, 0)
        for h in hs:
            s_ref[h] = _dot_nt(kk_ref[h, jj], qs[h])

    def softmax_cols(s_ref, p_ref, a_ref, diagonal, hs):
        def block(h, rb):
            s = s_ref[h, rb * FOX_ROWS:(rb + 1) * FOX_ROWS, :]
            if diagonal:
                s = jnp.where(key_in_blk + rb * FOX_ROWS <= query, s, NEG_BIG)
            return s

        tile_max = {}
        for h in hs:
            m8 = None
            for rb in range(n_rb):
                s = block(h, rb)
                b8 = jnp.maximum(jnp.maximum(s[0:8], s[8:16]), jnp.maximum(s[16:24], s[24:32]))
                m8 = b8 if m8 is None else jnp.maximum(m8, b8)
            tile_max[h] = jnp.max(m8, axis=0, keepdims=True)
        for h in hs:
            m_old = m_ref[h]
            m_new = jnp.maximum(m_old, tile_max[h])
            m_ref[h] = m_new
            a_ref[h] = jnp.exp2(m_old - m_new)
            for rb in range(n_rb):
                p_ref[h, rb * FOX_ROWS:(rb + 1) * FOX_ROWS, :] = jnp.exp2(block(h, rb) - m_new).astype(BF16)

    def accumulate(j, p_ref, a_ref, hs):
        jj = jnp.maximum(j, 0)
        for h in hs:
            acc_ref[h] = a_ref[h] * acc_ref[h] + _dot(vv_ref[h, jj], p_ref[h])

    pair_heads = [(2 * pr, 2 * pr + 1) for pr in range(FOX_PAIRS)]
    for hs in pair_heads:
        scores(i, sb_ref, hs)
        scores(i - 1, sa_ref, hs)
    for hs in pair_heads:
        softmax_cols(sb_ref, pb_ref, ab_ref, True, hs)

    def pair(n, carry):
        j = i - 1 - 2 * n
        for hs in pair_heads:
            scores(j - 1, sb_ref, hs)
            softmax_cols(sa_ref, pa_ref, aa_ref, False, hs)
            accumulate(j + 1, pb_ref, ab_ref, hs)
        for hs in pair_heads:
            scores(j - 2, sa_ref, hs)
            softmax_cols(sb_ref, pb_ref, ab_ref, False, hs)
            accumulate(j, pa_ref, aa_ref, hs)
        return carry

    n_pairs = i // 2
    lax.fori_loop(0, n_pairs, pair, 0)
    pending = i - 2 * n_pairs

    @pl.when(pending == 1)
    def _():
        for hs in pair_heads:
            softmax_cols(sa_ref, pa_ref, aa_ref, False, hs)
            accumulate(1, pb_ref, ab_ref, hs)
            accumulate(0, pa_ref, aa_ref, hs)

    @pl.when(pending == 0)
    def _():
        for hs in pair_heads:
            accumulate(0, pb_ref, ab_ref, hs)

    for pr in range(FOX_PAIRS):
        acc0, acc1 = acc_ref[2 * pr], acc_ref[2 * pr + 1]
        out_t = jnp.where(top_half, acc0 / acc0[D_HEAD_DIM:D_HEAD_DIM + 1, :], acc1 / acc1[0:1, :])
        o_ref[:, pair_lanes(pr)] = out_t.T.astype(o_ref.dtype)


def _fox_attention(q, k, v_t, fk, bsz, seq_len):
    nt = seq_len // FOX_T
    width = FOX_PAIRS * LANES
    groups = D_W // width
    n_heads = 2 * FOX_PAIRS
    q3, k3 = (a.reshape(bsz, seq_len, D_W) for a in (q, k))
    whole_seq = pl.BlockSpec((None, seq_len, width), lambda b, g, i: (b, 0, g))
    place = np.zeros((LANES, D_W), np.float32)
    for h in range(D_HEADS):
        for piece in range(FOX_F_PIECES):
            place[piece * D_HEADS + h, _fox_f_lane(h) + piece] = 1.0
    out = pl.pallas_call(
        _fox_kernel, grid=(bsz, groups, nt),
        in_specs=[pl.BlockSpec((None, FOX_T, width), lambda b, g, i: (b, i, g)), whole_seq,
                  pl.BlockSpec((nt, width, FOX_T), lambda b, g, i: (b, g, 0)),
                  pl.BlockSpec((None, seq_len, LANES), lambda b, g, i: (b, 0, 0)),
                  pl.BlockSpec((LANES, width), lambda b, g, i: (0, g))],
        out_specs=pl.BlockSpec((None, FOX_T, width), lambda b, g, i: (b, i, g)),
        out_shape=jax.ShapeDtypeStruct((bsz, seq_len, D_W), BF16),
        scratch_shapes=[pltpu.VMEM((n_heads, nt, FOX_T, LANES), BF16), pltpu.VMEM((n_heads, nt, LANES, FOX_T), BF16)]
        + [pltpu.VMEM((n_heads, FOX_T, FOX_T), F32)] * 2 + [pltpu.VMEM((n_heads, FOX_T, FOX_T), BF16)] * 2
        + [pltpu.VMEM((n_heads, 1, FOX_T), F32)] * 3 + [pltpu.VMEM((n_heads, LANES, FOX_T), F32)],
        compiler_params=_cparams("parallel", "parallel", "arbitrary"), name="fox_attention",
    )(q3, k3, v_t, fk, jnp.asarray(place, BF16))
    return out.reshape(bsz * seq_len, D_W)


def _pad_cols(w, width=LANES):
    return jnp.pad(w, ((0, 0), (0, width - w.shape[1])))


def kernel(x, norm_mix, norm_ffn, norm_final, ffn_w_gate, ffn_w_up, ffn_w_down, ab_w_in, ab_rel_bias, ab_conv_w, ab_a_log, ab_dt_bias, ab_norm_w, ab_w_out, cd_w_in, cd_conv_w, cd_conv_b, cd_dt_bias, cd_a_log, cd_d_skip, cd_norm_w, cd_f_bias, cd_w_out):
    bsz, seq_len, d = x.shape
    n = bsz * seq_len
    xf = x.reshape(n, d)
    bf = lambda w: w.astype(BF16)

    w_in = ab_w_in[0]
    o = np.cumsum([0, A_W, A_W, A_W, 3 * B_W, B_HEADS, B_HEADS, B_W])
    weights = [bf(w_in[:, o[0]:o[1]] * (LOG2_E * A_HEAD_DIM ** -0.5)), bf(w_in[:, o[1]:o[2]]), bf(w_in[:, o[2]:o[3]].T),
               bf(w_in[:, o[3]:o[4]]), bf(_pad_cols(w_in[:, o[4]:o[6]])), bf(w_in[:, o[6]:o[7]])]
    a_q, a_k, a_vt, b_qkv, b_small, b_z = _norm_proj(
        xf, norm_mix[0], weights, [BF16, BF16, BF16, F32, F32, F32], transposed=(False, False, True) + (False,) * 3)
    o_a = _band_attention(a_q, a_k, a_vt, ab_rel_bias[0], bsz, seq_len)
    o_b = _gated_delta(b_qkv, b_small, b_z, ab_conv_w[0], ab_a_log[0], ab_dt_bias[0], ab_norm_w[0], bsz, seq_len)
    ffn_w = (bf(ffn_w_gate), bf(ffn_w_up), bf(ffn_w_down))
    xf = _layer_tail(xf, o_a, o_b, bf(ab_w_out), norm_ffn[0], *ffn_w, 0, norm_final, False)

    w_in = cd_w_in[0]
    o = np.cumsum([0, C_W, C_W + 2 * C_BC, C_HEADS, D_W, D_W, D_W, D_HEADS])
    small_w = jnp.concatenate([w_in[:, o[2]:o[3]], w_in[:, o[6]:o[7]]], axis=1)
    weights = [bf(w_in[:, o[0]:o[1]]), bf(w_in[:, o[1]:o[2]]), bf(_pad_cols(small_w)),
               bf(w_in[:, o[3]:o[4]] * (LOG2_E * D_HEAD_DIM ** -0.5)), bf(w_in[:, o[4]:o[5]]), bf(w_in[:, o[5]:o[6]].T)]
    c_z, c_xbc, cd_small, d_q, d_k, d_vt = _norm_proj(
        xf, norm_mix[1], weights, [F32, F32, F32, BF16, BF16, BF16], transposed=(False,) * 5 + (True,))
    y_c = _ssd(c_xbc, cd_small, c_z, cd_conv_w[0], cd_conv_b[0], cd_dt_bias[0], cd_a_log[0], cd_d_skip[0],
               cd_norm_w[0], bsz, seq_len)
    fk = _fox_prep(cd_small, cd_f_bias[0], bsz, seq_len)
    o_d = _fox_attention(d_q, d_k, d_vt, fk, bsz, seq_len)
    xf = _layer_tail(xf, y_c, o_d, bf(cd_w_out), norm_ffn[1], *ffn_w, 1, norm_final, True)
    return xf.reshape(bsz, seq_len, d)
```
